```python
import math
import jax, jax.numpy as jnp
from jax import lax
import numpy as np

D_MODEL = 4096
BATCH = 1
SEQ = 8192
DEPTH = 1

D_MIX = D_MODEL
POOL_WIDTH = D_MIX // 4
POOL_WINDOWS = (2, 4, 8, 16)
POOL_GROUP = POOL_WIDTH // len(POOL_WINDOWS)
HEAD_DIM = 128
NSA_HEADS = (D_MIX - POOL_WIDTH) // HEAD_DIM
KV_GROUPS = 4
HEADS_PER_GROUP = NSA_HEADS // KV_GROUPS
KV_WIDTH = KV_GROUPS * HEAD_DIM
CMP_BLOCK = 32
CMP_STRIDE = 16
SLC_BLOCK = 64
N_SELECT = 16
N_LOCAL_FORCED = 2
WINDOW = 512
N_BRANCH = 3
Q_BLOCK = 128
REL_BUCKETS = 32
REL_MAX_DIST = 128
D_FF = 11008
CONV_WIDTH = 3
RMS_EPS = 1e-6
NEG = -1e30
N_IN = POOL_WIDTH + NSA_HEADS * HEAD_DIM + 6 * KV_WIDTH + NSA_HEADS * N_BRANCH

kernel_name = "hybrid_pool_nsa_convffn"


def _in_splits():
    sizes = [POOL_WIDTH, NSA_HEADS * HEAD_DIM] + [KV_WIDTH] * 6
    out, acc = [], 0
    for s in sizes:
        acc += s
        out.append(acc)
    return out


def _slc_overlap_weights():
    r = SLC_BLOCK // CMP_STRIDE
    lead = -(-CMP_BLOCK // CMP_STRIDE) - 1
    out = []
    for o in range(-lead, r):
        s0 = o * CMP_STRIDE
        ov = max(0, min(s0 + CMP_BLOCK, SLC_BLOCK) - max(s0, 0))
        if ov > 0:
            out.append((o, ov / CMP_STRIDE))
    return lead, out


def rms_norm(x, g):
    xf = x.astype(jnp.float32)
    y = xf * lax.rsqrt(jnp.mean(xf * xf, axis=-1, keepdims=True) + RMS_EPS)
    return (y * g.astype(jnp.float32)).astype(x.dtype)


def rel_bucket(dist):
    n = jnp.maximum(dist, 0)
    max_exact = REL_BUCKETS // 2
    nf = jnp.maximum(n, 1).astype(jnp.float32)
    large = max_exact + (jnp.log(nf / max_exact) / math.log(REL_MAX_DIST / max_exact)
                         * (REL_BUCKETS - max_exact)).astype(jnp.int32)
    large = jnp.minimum(large, REL_BUCKETS - 1)
    return jnp.where(n < max_exact, n, large)


def masked_softmax(s, valid):
    p = jax.nn.softmax(jnp.where(valid, s, NEG), axis=-1)
    return jnp.where(valid, p, 0.0)


def multiscale_pool(u, pool_w, pool_scale):
    B, S, _ = u.shape
    ug = u.astype(jnp.float32).reshape(B, S, len(POOL_WINDOWS), POOL_GROUP)
    c = jnp.cumsum(ug, axis=1)
    c = jnp.concatenate([jnp.zeros_like(c[:, :1]), c], axis=1)
    t = jnp.arange(S)
    outs = []
    for gi, w in enumerate(POOL_WINDOWS):
        lo = jnp.maximum(t + 1 - w, 0)
        cnt = (t + 1 - lo).astype(jnp.float32)
        mean = (c[:, 1:, gi] - c[:, lo, gi]) / cnt[None, :, None]
        outs.append(mean - ug[:, :, gi])
    d = jnp.stack(outs, axis=2).astype(u.dtype)
    y = jnp.einsum('bsgc,gce->bsge', d, pool_w)
    return y.reshape(B, S, POOL_WIDTH) * pool_scale


def compress(kv, pe, w1, w2):
    B, S, G, Dh = kv.shape
    r = CMP_BLOCK // CMP_STRIDE
    ch = kv.reshape(B, S // CMP_STRIDE, CMP_STRIDE, G, Dh)
    nc = S // CMP_STRIDE - r + 1
    blocks = jnp.concatenate([ch[:, j:j + nc] for j in range(r)], axis=2)
    blocks = blocks + pe[None, None, :, None, :]
    h = jax.nn.silu(jnp.einsum('bnlgd,lde->bnge', blocks, w1))
    return jnp.einsum('bnge,ef->bngf', h, w2)


def native_sparse_attention(q, k_cmp, v_cmp, k_slc, v_slc, k_win, v_win, gate_logits,
                            pe_k, w1_k, w2_k, pe_v, w1_v, w2_v, rel_bias):
    B, S, H, Dh = q.shape
    G, HPG = KV_GROUPS, HEADS_PER_GROUP
    dt = q.dtype
    kc = compress(k_cmp, pe_k, w1_k, w2_k)
    vc = compress(v_cmp, pe_v, w1_v, w2_v)
    NC = kc.shape[1]
    NS = S // SLC_BLOCK
    n_sel = min(N_SELECT, NS)
    r = SLC_BLOCK // CMP_STRIDE
    lead, slc_w = _slc_overlap_weights()
    right = r * NS + lead - (NC + lead)
    ks_t = k_slc.reshape(B, NS, SLC_BLOCK, G, Dh).transpose(0, 3, 1, 2, 4)
    vs_t = v_slc.reshape(B, NS, SLC_BLOCK, G, Dh).transpose(0, 3, 1, 2, 4)
    kw_pad = jnp.pad(k_win, ((0, 0), (WINDOW, 0), (0, 0), (0, 0)))
    vw_pad = jnp.pad(v_win, ((0, 0), (WINDOW, 0), (0, 0), (0, 0)))
    gates = jax.nn.sigmoid(gate_logits.astype(jnp.float32)).astype(dt)
    gates = gates.reshape(B, S, G, HPG, N_BRANCH)
    tbl = rel_bias.astype(jnp.float32).T.reshape(G, HPG, REL_BUCKETS)
    scale = HEAD_DIM ** -0.5
    nq = S // Q_BLOCK
    cmp_end = jnp.arange(NC) * CMP_STRIDE + CMP_BLOCK - 1
    jj = jnp.arange(NS)
    bi = jnp.arange(B)[:, None, None, None]
    gi = jnp.arange(G)[None, :, None, None]
    g6 = jnp.arange(G)[None, :, None, None, None, None]
    h6 = jnp.arange(HPG)[None, None, :, None, None, None]

    def block(args):
        i, qb, gb = args
        t = i * Q_BLOCK + jnp.arange(Q_BLOCK)
        qs = qb * scale
        dist_c = t[:, None] - cmp_end[None, :]
        s_c = jnp.einsum('bqghd,bngd->bghqn', qs, kc).astype(jnp.float32)
        s_c = s_c + tbl[:, :, rel_bucket(dist_c)]
        p_c = masked_softmax(s_c, dist_c >= 0)
        o_cmp = jnp.einsum('bghqn,bngd->bqghd', p_c.astype(dt), vc)
        imp = jnp.pad(p_c.sum(axis=2), ((0, 0), (0, 0), (0, 0), (lead, right)))
        slc = sum(w * lax.slice_in_dim(imp, o + lead, o + lead + r * (NS - 1) + 1, r, axis=-1)
                  for o, w in slc_w)
        cur = (t // SLC_BLOCK)[:, None]
        future = (jj[None, :] * SLC_BLOCK) > t[:, None]
        forced = (jj[None, :] == 0) | ((cur - jj[None, :] >= 0) & (cur - jj[None, :] < N_LOCAL_FORCED))
        score = jnp.where(forced, 1e9, jnp.where(future, -1e9, slc))
        _, idx = lax.top_k(score, n_sel)
        kg = ks_t[bi, gi, idx]
        vg = vs_t[bi, gi, idx]
        pos = idx[..., None] * SLC_BLOCK + jnp.arange(SLC_BLOCK)
        dist_s = t[None, None, :, None, None] - pos
        bias_s = tbl[g6, h6, rel_bucket(dist_s)[:, :, None]]
        s_s = jnp.einsum('bqghd,bgqkld->bghqkl', qs, kg).astype(jnp.float32) + bias_s
        s_s = jnp.where((dist_s >= 0)[:, :, None], s_s, NEG)
        shp = s_s.shape
        p_s = jax.nn.softmax(s_s.reshape(shp[:4] + (n_sel * SLC_BLOCK,)), axis=-1).reshape(shp)
        o_slc = jnp.einsum('bghqkl,bgqkld->bqghd', p_s.astype(dt), vg)
        kw = lax.dynamic_slice_in_dim(kw_pad, i * Q_BLOCK, Q_BLOCK + WINDOW, axis=1)
        vw = lax.dynamic_slice_in_dim(vw_pad, i * Q_BLOCK, Q_BLOCK + WINDOW, axis=1)
        kpos = i * Q_BLOCK - WINDOW + jnp.arange(Q_BLOCK + WINDOW)
        dist_w = t[:, None] - kpos[None, :]
        valid_w = (dist_w >= 0) & (dist_w < WINDOW) & (kpos[None, :] >= 0)
        s_w = jnp.einsum('bqghd,bkgd->bghqk', qs, kw).astype(jnp.float32)
        s_w = s_w + tbl[:, :, rel_bucket(dist_w)]
        p_w = masked_softmax(s_w, valid_w)
        o_win = jnp.einsum('bghqk,bkgd->bqghd', p_w.astype(dt), vw)
        o = gb[..., 0:1] * o_cmp + gb[..., 1:2] * o_slc + gb[..., 2:3] * o_win
        return o.reshape(B, Q_BLOCK, H * Dh)

    q_blocks = q.reshape(B, nq, Q_BLOCK, G, HPG, Dh).transpose(1, 0, 2, 3, 4, 5)
    g_blocks = gates.reshape(B, nq, Q_BLOCK, G, HPG, N_BRANCH).transpose(1, 0, 2, 3, 4, 5)
    out = lax.map(block, (jnp.arange(nq, dtype=jnp.int32), q_blocks, g_blocks))
    return out.transpose(1, 0, 2, 3).reshape(B, S, H * Dh)


def conv_ffn(h, w_up, conv_w, conv_b, w_down):
    S = h.shape[1]
    u = h @ w_up
    up = jnp.pad(u, ((0, 0), (CONV_WIDTH - 1, 0), (0, 0)))
    c = conv_b + sum(conv_w[k] * up[:, k:k + S] for k in range(CONV_WIDTH))
    a, b = jnp.split(c, 2, axis=-1)
    return (jax.nn.silu(a) * b) @ w_down


def setup_inputs(seed: int = 0) -> dict:
    key = jax.random.key(seed)
    ks = jax.random.split(key, 20)
    f = jnp.float32

    def nrm(k, shape, s):
        return jax.random.normal(k, shape, f) * s

    L = DEPTH
    return {
        "x": nrm(ks[0], (BATCH, SEQ, D_MODEL), 1.0),
        "norm_mix_g": 1.0 + nrm(ks[1], (L, D_MODEL), 0.02),
        "w_in": nrm(ks[2], (L, D_MODEL, N_IN), D_MODEL ** -0.5),
        "pool_w": nrm(ks[3], (L, len(POOL_WINDOWS), POOL_GROUP, POOL_GROUP), POOL_GROUP ** -0.5),
        "pool_scale": 1.0 + nrm(ks[4], (L, POOL_WIDTH), 0.02),
        "cmp_pe_k": nrm(ks[5], (L, CMP_BLOCK, HEAD_DIM), 0.02),
        "cmp_w1_k": nrm(ks[6], (L, CMP_BLOCK, HEAD_DIM, HEAD_DIM), (CMP_BLOCK * HEAD_DIM) ** -0.5),
        "cmp_w2_k": nrm(ks[7], (L, HEAD_DIM, HEAD_DIM), HEAD_DIM ** -0.5),
        "cmp_pe_v": nrm(ks[8], (L, CMP_BLOCK, HEAD_DIM), 0.02),
        "cmp_w1_v": nrm(ks[9], (L, CMP_BLOCK, HEAD_DIM, HEAD_DIM), (CMP_BLOCK * HEAD_DIM) ** -0.5),
        "cmp_w2_v": nrm(ks[10], (L, HEAD_DIM, HEAD_DIM), HEAD_DIM ** -0.5),
        "rel_bias": nrm(ks[11], (REL_BUCKETS, NSA_HEADS), 0.5),
        "w_out": nrm(ks[12], (L, D_MIX, D_MODEL), D_MIX ** -0.5),
        "norm_ffn_g": 1.0 + nrm(ks[13], (L, D_MODEL), 0.02),
        "w_up": nrm(ks[14], (L, D_MODEL, 2 * D_FF), D_MODEL ** -0.5),
        "conv_w": nrm(ks[15], (L, CONV_WIDTH, 2 * D_FF), CONV_WIDTH ** -0.5),
        "conv_b": nrm(ks[16], (L, 2 * D_FF), 0.02),
        "w_down": nrm(ks[17], (L, D_FF, D_MODEL), D_FF ** -0.5),
        "norm_final_g": 1.0 + nrm(ks[18], (D_MODEL,), 0.02),
    }


def reference(x, norm_mix_g, w_in, pool_w, pool_scale, cmp_pe_k, cmp_w1_k, cmp_w2_k,
              cmp_pe_v, cmp_w1_v, cmp_w2_v, rel_bias, w_out, norm_ffn_g, w_up, conv_w,
              conv_b, w_down, norm_final_g):
    B, S, _ = x.shape
    splits = _in_splits()
    h = x
    for l in range(DEPTH):
        hn = rms_norm(h, norm_mix_g[l])
        proj = hn @ w_in[l]
        u_pool, q, kc, vc, ksl, vsl, kw, vw, gl = jnp.split(proj, splits, axis=-1)
        kvs = lambda t: t.reshape(B, S, KV_GROUPS, HEAD_DIM)
        y_pool = multiscale_pool(u_pool, pool_w[l], pool_scale[l])
        y_nsa = native_sparse_attention(
            q.reshape(B, S, NSA_HEADS, HEAD_DIM), kvs(kc), kvs(vc), kvs(ksl), kvs(vsl),
            kvs(kw), kvs(vw), gl, cmp_pe_k[l], cmp_w1_k[l], cmp_w2_k[l],
            cmp_pe_v[l], cmp_w1_v[l], cmp_w2_v[l], rel_bias)
        h = h + jnp.concatenate([y_nsa, y_pool], axis=-1) @ w_out[l]
        h = h + conv_ffn(rms_norm(h, norm_ffn_g[l]), w_up[l], conv_w[l], conv_b[l], w_down[l])
    return rms_norm(h, norm_final_g)
```

```python
import functools
import math

import numpy as np
import jax
import jax.numpy as jnp
from jax import lax
from jax.experimental import pallas as pl
from jax.experimental.pallas import tpu as pltpu

POOL_WINDOWS = (2, 4, 8, 16)
POOL_GROUP = 256
POOL_WIDTH = POOL_GROUP * len(POOL_WINDOWS)
HEAD_DIM = 128
KV_GROUPS = 4
HEADS_PER_GROUP = 6
NSA_HEADS = KV_GROUPS * HEADS_PER_GROUP
Q_WIDTH = NSA_HEADS * HEAD_DIM
KV_WIDTH = KV_GROUPS * HEAD_DIM
CMP_BLOCK = 32
CMP_STRIDE = 16
SLC_BLOCK = 64
N_SELECT = 16
N_LOCAL_FORCED = 2
WINDOW = 512
N_BRANCH = 3
REL_BUCKETS = 32
REL_MAX_DIST = 128
CONV_WIDTH = 3
RMS_EPS = 1e-6
NEG = -1e30

LANES = 128
SUBLANES = 8
VMEM_LIMIT_BYTES = 56 * 1024 * 1024

CDT = jnp.bfloat16
QB = 128
PAD_ROWS = WINDOW
FAR_CHUNK = 256
NEAR_KEYS = 2 * QB
WIN_KEYS = WINDOW + QB
BAND = 16
BAND_LEAD = 9


def _cparams(*sem):
    return pltpu.CompilerParams(dimension_semantics=sem, vmem_limit_bytes=VMEM_LIMIT_BYTES)


def _dot(a, b):
    return jnp.dot(a, b, preferred_element_type=jnp.float32)


def _dot_nt(a, b):
    return lax.dot_general(a, b, (((1,), (1,)), ((), ())), preferred_element_type=jnp.float32)


def _rmsnorm_body(x_ref, g_ref, o_ref):
    x = x_ref[...]
    y = x * lax.rsqrt(jnp.mean(x * x, axis=-1, keepdims=True) + RMS_EPS)
    o_ref[...] = (y * g_ref[...]).astype(o_ref.dtype)


def _rmsnorm(x, g, out_dtype, bm=256):
    S, D = x.shape
    return pl.pallas_call(
        _rmsnorm_body,
        grid=(S // bm,),
        in_specs=[pl.BlockSpec((bm, D), lambda i: (i, 0)), pl.BlockSpec((1, D), lambda i: (0, 0))],
        out_specs=pl.BlockSpec((bm, D), lambda i: (i, 0)),
        out_shape=jax.ShapeDtypeStruct((S, D), out_dtype),
        compiler_params=_cparams("parallel"),
        name="rmsnorm",
    )(x, g.reshape(1, D))


def _proj_body(a_ref, w_ref, o_ref, *, pad_tiles, n_scaled, scale, sigmoid):
    i = pl.program_id(0)
    j = pl.program_id(1)

    @pl.when(i < pad_tiles)
    def _():
        o_ref[...] = jnp.zeros_like(o_ref)

    @pl.when(i >= pad_tiles)
    def _():
        r = _dot(a_ref[...], w_ref[...])
        if n_scaled:
            r = r * jnp.where(j < n_scaled, jnp.float32(scale), jnp.float32(1.0))
        if sigmoid:
            r = jax.nn.sigmoid(r)
        o_ref[...] = r.astype(o_ref.dtype)


def _proj(a, w, out_dtype, *, pad_tiles=0, n_scaled=0, scale=1.0, sigmoid=False, bm=512, bn=512, name="proj"):
    S, D = a.shape
    N = w.shape[1]
    body = functools.partial(_proj_body, pad_tiles=pad_tiles, n_scaled=n_scaled, scale=scale, sigmoid=sigmoid)
    return pl.pallas_call(
        body,
        grid=(S // bm + pad_tiles, N // bn),
        in_specs=[pl.BlockSpec((bm, D), lambda i, j: (jnp.maximum(i - pad_tiles, 0), 0)),
                  pl.BlockSpec((D, bn), lambda i, j: (0, j))],
        out_specs=pl.BlockSpec((bm, bn), lambda i, j: (i, j)),
        out_shape=jax.ShapeDtypeStruct((S + pad_tiles * bm, N), out_dtype),
        compiler_params=_cparams("parallel", "arbitrary"),
        name=name,
    )(a, w)


POOL_HALO = 16


def _pool_body(u_ref, halo_ref, w_ref, s_ref, o_ref):
    i = pl.program_id(0)
    bm = u_ref.shape[0]
    u = u_ref[...]
    halo = jnp.where(i > 0, halo_ref[...], 0.0)
    ext = jnp.concatenate([halo, u], axis=0)
    t = i * bm + lax.broadcasted_iota(jnp.int32, (bm, 1), 0)
    acc = ext
    sums = {}
    shift = 1
    while shift < POOL_WINDOWS[-1]:
        acc = acc + pltpu.roll(acc, shift, axis=0)
        shift *= 2
        sums[shift] = acc
    for gi, w in enumerate(POOL_WINDOWS):
        cols = slice(gi * POOL_GROUP, (gi + 1) * POOL_GROUP)
        cnt = jnp.minimum(t + 1, w).astype(jnp.float32)
        d = sums[w][POOL_HALO:, cols] / cnt - u[:, cols]
        y = _dot(d.astype(CDT), w_ref[gi])
        o_ref[:, cols] = (y * s_ref[:, cols]).astype(o_ref.dtype)


def _pool(u, pool_w, pool_scale, bm=512):
    S = u.shape[0]
    return pl.pallas_call(
        _pool_body,
        grid=(S // bm,),
        in_specs=[pl.BlockSpec((bm, POOL_WIDTH), lambda i: (i, 0)),
                  pl.BlockSpec((POOL_HALO, POOL_WIDTH), lambda i: (jnp.maximum(i * (bm // POOL_HALO) - 1, 0), 0)),
                  pl.BlockSpec((len(POOL_WINDOWS), POOL_GROUP, POOL_GROUP), lambda i: (0, 0, 0)),
                  pl.BlockSpec((1, POOL_WIDTH), lambda i: (0, 0))],
        out_specs=pl.BlockSpec((bm, POOL_WIDTH), lambda i: (i, 0)),
        out_shape=jax.ShapeDtypeStruct((S, POOL_WIDTH), CDT),
        compiler_params=_cparams("parallel"),
        name="pool",
    )(u, u, pool_w.astype(CDT), pool_scale.reshape(1, POOL_WIDTH))


def _compress_body(x_ref, w1_ref, pe_ref, w2_ref, o_ref, acc_ref, pe_acc_ref, *, pad_chunks):
    l = pl.program_id(2)

    @pl.when(l == 0)
    def _():
        acc_ref[...] = jnp.zeros_like(acc_ref)
        pe_acc_ref[...] = jnp.zeros_like(pe_acc_ref)

    w_lo = w1_ref[0, 0]
    w_hi = w1_ref[1, 0]
    x = x_ref[...]
    acc_ref[:, :HEAD_DIM] += _dot(x, w_lo)
    acc_ref[:, HEAD_DIM:] += _dot(x, w_hi)
    pe_lo = jnp.broadcast_to(pe_ref[0, 0], (SUBLANES, HEAD_DIM))
    pe_hi = jnp.broadcast_to(pe_ref[1, 0], (SUBLANES, HEAD_DIM))
    pe_acc_ref[...] += _dot(pe_lo, w_lo) + _dot(pe_hi, w_hi)

    @pl.when(l == CMP_STRIDE - 1)
    def _():
        n = o_ref.shape[0]
        a = acc_ref[pad_chunks:pad_chunks + n, :HEAD_DIM]
        b = acc_ref[:, HEAD_DIM:]
        b_next = pltpu.roll(b, b.shape[0] - 1, axis=0)[pad_chunks:pad_chunks + n]
        pre = a + b_next + pe_acc_ref[0:1, :]
        h = pre * jax.nn.sigmoid(pre)
        o_ref[...] = _dot(h.astype(CDT), w2_ref[0]).astype(o_ref.dtype)


def _compress(qkv, w1, pe, w2):
    rows = qkv.shape[0]
    chunks = rows // CMP_STRIDE
    pad_chunks = PAD_ROWS // CMP_STRIDE
    n_out = chunks - pad_chunks
    width = qkv.shape[1] // LANES
    x = qkv.reshape(chunks, CMP_STRIDE * qkv.shape[1])
    first = Q_WIDTH // LANES
    body = functools.partial(_compress_body, pad_chunks=pad_chunks)
    return pl.pallas_call(
        body,
        grid=(2, KV_GROUPS, CMP_STRIDE),
        in_specs=[pl.BlockSpec((chunks, LANES), lambda kv, g, l: (0, l * width + first + kv * KV_GROUPS + g)),
                  pl.BlockSpec((2, 1, HEAD_DIM, HEAD_DIM), lambda kv, g, l: (kv, l, 0, 0)),
                  pl.BlockSpec((2, 1, 1, HEAD_DIM), lambda kv, g, l: (kv, l, 0, 0)),
                  pl.BlockSpec((1, HEAD_DIM, HEAD_DIM), lambda kv, g, l: (kv, 0, 0))],
        out_specs=pl.BlockSpec((None, None, n_out, HEAD_DIM), lambda kv, g, l: (kv, g, 0, 0)),
        out_shape=jax.ShapeDtypeStruct((2, KV_GROUPS, n_out, HEAD_DIM), CDT),
        scratch_shapes=[pltpu.VMEM((chunks, 2 * HEAD_DIM), jnp.float32),
                        pltpu.VMEM((SUBLANES, HEAD_DIM), jnp.float32)],
        compiler_params=_cparams("parallel", "parallel", "arbitrary"),
        name="compress",
    )(x, w1, pe, w2)


def _rel_bucket_np(dist):
    n = np.maximum(dist, 0)
    max_exact = REL_BUCKETS // 2
    nf = np.maximum(n, 1).astype(np.float32)
    large = max_exact + (np.log(nf / max_exact) / math.log(REL_MAX_DIST / max_exact)
                         * (REL_BUCKETS - max_exact)).astype(np.int32)
    large = np.minimum(large, REL_BUCKETS - 1)
    return np.where(n < max_exact, n, large).astype(np.int32)


def _bias_index_tiles():
    q = np.arange(QB)[:, None]
    dist = QB + q - np.arange(NEAR_KEYS)[None, :]
    near = np.where(dist >= 0, _rel_bucket_np(dist), -1)
    dist = WINDOW + q - np.arange(WIN_KEYS)[None, :]
    win = np.where((dist >= 0) & (dist < WINDOW), _rel_bucket_np(dist), -1)
    w = np.arange(LANES)[None, :]
    dist = q - CMP_STRIDE * (w - BAND_LEAD) - (CMP_BLOCK - 1)
    band = np.where((dist >= 0) & (w < BAND), _rel_bucket_np(dist), -1)
    return np.concatenate([near, win, band], axis=1).astype(np.int32)


def _bias_body(tbl_ref, idx_ref, o_ref):
    h = pl.program_id(0)
    idx = idx_ref[...]
    far = tbl_ref[REL_BUCKETS - 1, h]
    val = jnp.full(idx.shape, NEG, jnp.float32)
    for b in range(REL_BUCKETS):
        val = jnp.where(idx == b, tbl_ref[b, h] - far, val)
    o_ref[...] = val


def _bias_tiles(rel_bias):
    idx = jnp.asarray(_bias_index_tiles())
    width = idx.shape[1]
    return pl.pallas_call(
        _bias_body,
        grid=(NSA_HEADS,),
        in_specs=[pl.BlockSpec(memory_space=pltpu.SMEM),
                  pl.BlockSpec((QB, width), lambda h: (0, 0))],
        out_specs=pl.BlockSpec((None, QB, width), lambda h: (h, 0, 0)),
        out_shape=jax.ShapeDtypeStruct((NSA_HEADS, QB, width), jnp.float32),
        compiler_params=_cparams("arbitrary"),
        name="bias_tiles",
    )(rel_bias, idx)


def _softmax_rows(s):
    m = jnp.max(s, axis=-1, keepdims=True)
    p = jnp.where(s > 0.5 * NEG, jnp.exp(s - m), 0.0)
    l = jnp.sum(p, axis=-1, keepdims=True)
    return p / jnp.maximum(l, 1e-30)


def _split3(x):
    hi = x.astype(CDT)
    r = x - hi.astype(jnp.float32)
    mid = r.astype(CDT)
    lo = (r - mid.astype(jnp.float32)).astype(CDT)
    return hi, mid, lo


def _attn_body(q_ref, gate_ref, kc_ref, vc_ref, ks_ref, vs_ref, kw_ref, vw_ref, blk_ref, tab_ref, wov_ref,
               o_ref, kaug_ref):
    i = pl.program_id(1)
    H = HEADS_PER_GROUP
    n_cmp = kc_ref.shape[0]
    n_slc = wov_ref.shape[1]

    @pl.when(i == 0)
    def _():
        kaug_ref[:, :HEAD_DIM] = ks_ref[...]
        kaug_ref[:, HEAD_DIM:] = blk_ref[...]

    qb = q_ref[...]
    qs = jnp.concatenate([qb[:, h * HEAD_DIM:(h + 1) * HEAD_DIM] for h in range(H)], axis=0)
    near_tab = lambda h: tab_ref[h, :, :NEAR_KEYS]
    win_tab = lambda h: tab_ref[h, :, NEAR_KEYS:NEAR_KEYS + WIN_KEYS]
    band_tab = lambda h: tab_ref[h, :, NEAR_KEYS + WIN_KEYS:]
    rows = lambda x, h: x[h * QB:(h + 1) * QB]

    s_c = _dot_nt(qs, kc_ref[...])
    n_idx = lax.broadcasted_iota(jnp.int32, (1, n_cmp), 1)
    band_lo = 8 * i - BAND_LEAD
    in_band = (n_idx >= band_lo) & (n_idx < band_lo + BAND)
    base = jnp.where(n_idx >= band_lo + BAND, NEG, 0.0)
    shift = jnp.bitwise_and(band_lo, LANES - 1)
    imp = jnp.zeros((QB, n_cmp), jnp.float32)
    p_c = []
    for h in range(H):
        rolled = pltpu.roll(band_tab(h), shift, axis=1)
        bias = jnp.where(in_band, jnp.concatenate([rolled] * (n_cmp // LANES), axis=1), base)
        p = _softmax_rows(rows(s_c, h) + bias)
        imp = imp + p
        p_c.append(p.astype(CDT))
    o_cmp = _dot(jnp.concatenate(p_c, axis=0), vc_ref[...])

    wov = wov_ref[...]
    slc = sum(_dot(part, wov) for part in _split3(imp))
    t = i * QB + lax.broadcasted_iota(jnp.int32, (QB, 1), 0)
    j_idx = lax.broadcasted_iota(jnp.int32, (QB, n_slc), 1)
    cur = t // SLC_BLOCK
    forced = (j_idx == 0) | ((cur - j_idx >= 0) & (cur - j_idx < N_LOCAL_FORCED))
    score = jnp.where(forced, 1e9, jnp.where(j_idx > cur, -1e9, slc))
    picked = jnp.zeros((QB, n_slc), jnp.bool_)
    for _ in range(min(N_SELECT, n_slc)):
        m = jnp.max(score, axis=-1, keepdims=True)
        first = jnp.min(jnp.where(score == m, j_idx, n_slc), axis=-1, keepdims=True)
        hit = j_idx == first
        picked = picked | hit
        score = jnp.where(hit, -3e38, score)
    drop_near = jnp.where(picked, 0.0, 1.0)
    drop_far = jnp.where(picked & (j_idx < 2 * (i - 1)), 0.0, 1.0)

    def augment(drop):
        return jnp.concatenate([qs, jnp.concatenate([drop.astype(CDT)] * H, axis=0)], axis=1)

    near0 = pl.multiple_of(PAD_ROWS + (i - 1) * QB, QB)
    s_n = _dot_nt(augment(drop_near), kaug_ref[pl.ds(near0, NEAR_KEYS), :])
    s_n = jnp.concatenate([rows(s_n, h) + near_tab(h) for h in range(H)], axis=0)
    m0 = jnp.max(s_n, axis=-1, keepdims=True)
    p = jnp.exp(s_n - m0)
    l0 = jnp.sum(p, axis=-1, keepdims=True)
    acc0 = _dot(p.astype(CDT), vs_ref[pl.ds(near0, NEAR_KEYS), :])
    q_far = augment(drop_far)

    def far_step(c, carry):
        m, l, acc = carry
        k0 = pl.multiple_of(PAD_ROWS + c * FAR_CHUNK, FAR_CHUNK)
        s = _dot_nt(q_far, kaug_ref[pl.ds(k0, FAR_CHUNK), :])
        m_new = jnp.maximum(m, jnp.max(s, axis=-1, keepdims=True))
        alpha = jnp.exp(m - m_new)
        p = jnp.exp(s - m_new)
        l = alpha * l + jnp.sum(p, axis=-1, keepdims=True)
        acc = alpha * acc + _dot(p.astype(CDT), vs_ref[pl.ds(k0, FAR_CHUNK), :])
        return m_new, l, acc

    n_far = (jnp.maximum(i - 1, 0) * QB + FAR_CHUNK - 1) // FAR_CHUNK
    _, l_s, acc_s = lax.fori_loop(0, n_far, far_step, (m0, l0, acc0))
    o_slc = acc_s / l_s

    win0 = pl.multiple_of(i * QB, QB)
    s_w = _dot_nt(qs, kw_ref[pl.ds(win0, WIN_KEYS), :])
    r_idx = lax.broadcasted_iota(jnp.int32, (1, WIN_KEYS), 1)
    before_start = jnp.where(win0 - WINDOW + r_idx >= 0, 0.0, NEG)
    p_w = jnp.concatenate([_softmax_rows(rows(s_w, h) + win_tab(h) + before_start) for h in range(H)], axis=0)
    o_win = _dot(p_w.astype(CDT), vw_ref[pl.ds(win0, WIN_KEYS), :])

    gates = gate_ref[...]
    for h in range(H):
        g = lambda b: gates[:, h * N_BRANCH + b:h * N_BRANCH + b + 1]
        o = g(0) * rows(o_cmp, h) + g(1) * rows(o_slc, h) + g(2) * rows(o_win, h)
        o_ref[:, h * HEAD_DIM:(h + 1) * HEAD_DIM] = o.astype(o_ref.dtype)


def _block_membership(rows, n_slc):
    key = np.arange(rows)[:, None] - PAD_ROWS
    j = np.arange(n_slc)[None, :]
    member = np.where(key >= 0, key // SLC_BLOCK == j, True)
    return jnp.asarray(np.where(member, NEG, 0.0), dtype=CDT)


def _overlap_weights(n_cmp, n_slc):
    r = SLC_BLOCK // CMP_STRIDE
    lead = -(-CMP_BLOCK // CMP_STRIDE) - 1
    w = np.zeros((n_cmp, n_slc), np.float32)
    for o in range(-lead, r):
        s0 = o * CMP_STRIDE
        ov = max(0, min(s0 + CMP_BLOCK, SLC_BLOCK) - max(s0, 0))
        for j in range(n_slc):
            n = r * j + o
            if ov > 0 and 0 <= n < n_cmp - 1:
                w[n, j] = ov / CMP_STRIDE
    return jnp.asarray(w, dtype=CDT)


def _attention(qkv, gates, kvc, tabs):
    rows = qkv.shape[0]
    S = rows - PAD_ROWS
    n_cmp = kvc.shape[2]
    n_slc = S // SLC_BLOCK
    first = Q_WIDTH // LANES
    G = KV_GROUPS
    kv_spec = lambda which: pl.BlockSpec((rows, LANES), lambda g, i: (0, first + which * G + g))
    width = tabs.shape[2]
    return pl.pallas_call(
        _attn_body,
        grid=(G, S // QB),
        in_specs=[pl.BlockSpec((QB, HEADS_PER_GROUP * HEAD_DIM), lambda g, i: (i + PAD_ROWS // QB, g)),
                  pl.BlockSpec((QB, LANES), lambda g, i: (i, g)),
                  pl.BlockSpec((None, None, n_cmp, HEAD_DIM), lambda g, i: (0, g, 0, 0)),
                  pl.BlockSpec((None, None, n_cmp, HEAD_DIM), lambda g, i: (1, g, 0, 0)),
                  kv_spec(2), kv_spec(3), kv_spec(4), kv_spec(5),
                  pl.BlockSpec((rows, n_slc), lambda g, i: (0, 0)),
                  pl.BlockSpec((HEADS_PER_GROUP, QB, width), lambda g, i: (g, 0, 0)),
                  pl.BlockSpec((n_cmp, n_slc), lambda g, i: (0, 0))],
        out_specs=pl.BlockSpec((QB, HEADS_PER_GROUP * HEAD_DIM), lambda g, i: (i, g)),
        out_shape=jax.ShapeDtypeStruct((S, Q_WIDTH), CDT),
        scratch_shapes=[pltpu.VMEM((rows, HEAD_DIM + n_slc), CDT)],
        compiler_params=_cparams("parallel", "arbitrary"),
        name="nsa",
    )(qkv, gates, kvc, kvc, qkv, qkv, qkv, qkv, _block_membership(rows, n_slc), tabs,
      _overlap_weights(n_cmp, n_slc))


def _outproj_body(a1_ref, a2_ref, w1_ref, w2_ref, x_ref, o_ref):
    o_ref[...] = x_ref[...] + _dot(a1_ref[...], w1_ref[...]) + _dot(a2_ref[...], w2_ref[...])


def _outproj(y_nsa, y_pool, w_out, x, bm=512, bn=512):
    S, D = x.shape
    k1, k2 = y_nsa.shape[1], y_pool.shape[1]
    return pl.pallas_call(
        _outproj_body,
        grid=(S // bm, D // bn),
        in_specs=[pl.BlockSpec((bm, k1), lambda i, j: (i, 0)),
                  pl.BlockSpec((bm, k2), lambda i, j: (i, 0)),
                  pl.BlockSpec((k1, bn), lambda i, j: (0, j)),
                  pl.BlockSpec((k2, bn), lambda i, j: (k1 // k2, j)),
                  pl.BlockSpec((bm, bn), lambda i, j: (i, j))],
        out_specs=pl.BlockSpec((bm, bn), lambda i, j: (i, j)),
        out_shape=jax.ShapeDtypeStruct((S, D), jnp.float32),
        compiler_params=_cparams("parallel", "arbitrary"),
        name="outproj",
    )(y_nsa, y_pool, w_out, w_out, x)


CONV_HALO = 8


def _ffn_up_body(a_ref, halo_ref, wa_ref, wb_ref, cwa_ref, cwb_ref, cba_ref, cbb_ref, o_ref):
    i = pl.program_id(0)
    a = a_ref[...]
    halo = jnp.where(i > 0, halo_ref[...], jnp.zeros_like(halo_ref))

    def conv(w_ref, cw_ref, cb_ref):
        u = jnp.concatenate([_dot(halo, w_ref[...]), _dot(a, w_ref[...])], axis=0)
        c = cb_ref[...] + cw_ref[CONV_WIDTH - 1:CONV_WIDTH, :] * u[CONV_HALO:]
        for k in range(1, CONV_WIDTH):
            c = c + cw_ref[CONV_WIDTH - 1 - k:CONV_WIDTH - k, :] * pltpu.roll(u, k, axis=0)[CONV_HALO:]
        return c

    ca = conv(wa_ref, cwa_ref, cba_ref)
    cb = conv(wb_ref, cwb_ref, cbb_ref)
    o_ref[...] = (ca * jax.nn.sigmoid(ca) * cb).astype(o_ref.dtype)


def _ffn_up(hn, w_up, conv_w, conv_b, bm=512, bn=256):
    S, D = hn.shape
    F = w_up.shape[1] // 2
    nb = F // bn
    return pl.pallas_call(
        _ffn_up_body,
        grid=(S // bm, nb),
        in_specs=[pl.BlockSpec((bm, D), lambda i, j: (i, 0)),
                  pl.BlockSpec((CONV_HALO, D), lambda i, j: (jnp.maximum(i * (bm // CONV_HALO) - 1, 0), 0)),
                  pl.BlockSpec((D, bn), lambda i, j: (0, j)),
                  pl.BlockSpec((D, bn), lambda i, j: (0, nb + j)),
                  pl.BlockSpec((CONV_WIDTH, bn), lambda i, j: (0, j)),
                  pl.BlockSpec((CONV_WIDTH, bn), lambda i, j: (0, nb + j)),
                  pl.BlockSpec((1, bn), lambda i, j: (0, j)),
                  pl.BlockSpec((1, bn), lambda i, j: (0, nb + j))],
        out_specs=pl.BlockSpec((bm, bn), lambda i, j: (i, j)),
        out_shape=jax.ShapeDtypeStruct((S, F), CDT),
        compiler_params=_cparams("parallel", "arbitrary"),
        name="ffn_up",
    )(hn, hn, w_up, w_up, conv_w, conv_w, conv_b, conv_b)


def _ffn_down_body(a_ref, w_ref, h_ref, g_ref, o_ref, acc_ref):
    k = pl.program_id(1)

    @pl.when(k == 0)
    def _():
        acc_ref[...] = h_ref[...]

    acc_ref[...] += _dot(a_ref[...], w_ref[...])

    @pl.when(k == pl.num_programs(1) - 1)
    def _():
        x = acc_ref[...]
        y = x * lax.rsqrt(jnp.mean(x * x, axis=-1, keepdims=True) + RMS_EPS)
        o_ref[...] = y * g_ref[...]


def _ffn_down(act, w_down, h, g, bm=512, bk=256):
    S, F = act.shape
    D = h.shape[1]
    return pl.pallas_call(
        _ffn_down_body,
        grid=(S // bm, F // bk),
        in_specs=[pl.BlockSpec((bm, bk), lambda i, k: (i, k)),
                  pl.BlockSpec((bk, D), lambda i, k: (k, 0)),
                  pl.BlockSpec((bm, D), lambda i, k: (i, 0)),
                  pl.BlockSpec((1, D), lambda i, k: (0, 0))],
        out_specs=pl.BlockSpec((bm, D), lambda i, k: (i, 0)),
        out_shape=jax.ShapeDtypeStruct((S, D), jnp.float32),
        scratch_shapes=[pltpu.VMEM((bm, D), jnp.float32)],
        compiler_params=_cparams("parallel", "arbitrary"),
        name="ffn_down",
    )(act, w_down, h, g.reshape(1, D))


def _gate_weight(w_gate):
    D = w_gate.shape[0]
    per = HEADS_PER_GROUP * N_BRANCH
    w = w_gate.reshape(D, KV_GROUPS, per)
    w = jnp.pad(w, ((0, 0), (0, 0), (0, LANES - per)))
    return w.reshape(D, KV_GROUPS * LANES)


def _nsa_branch(hn, w_in, cmp_w1_k, cmp_pe_k, cmp_w2_k, cmp_w1_v, cmp_pe_v, cmp_w2_v, rel_bias):
    qkv_cols = Q_WIDTH + 6 * KV_WIDTH
    w_qkv = w_in[:, POOL_WIDTH:POOL_WIDTH + qkv_cols].astype(CDT)
    w_gate = _gate_weight(w_in[:, POOL_WIDTH + qkv_cols:]).astype(CDT)
    bm = 512
    qkv = _proj(hn, w_qkv, CDT, pad_tiles=PAD_ROWS // bm, n_scaled=Q_WIDTH // 512, scale=HEAD_DIM ** -0.5,
                bm=bm, name="proj_qkv")
    gates = _proj(hn, w_gate, jnp.float32, sigmoid=True, bm=bm, name="proj_gate")
    half = lambda w: w.reshape(2, CMP_STRIDE, *w.shape[1:])
    w1 = jnp.stack([half(cmp_w1_k), half(cmp_w1_v)]).astype(CDT)
    pe = jnp.stack([half(cmp_pe_k), half(cmp_pe_v)]).astype(CDT)
    w1 = w1.reshape(2 * 2, CMP_STRIDE, HEAD_DIM, HEAD_DIM)
    pe = pe.reshape(2 * 2, CMP_STRIDE, 1, HEAD_DIM)
    w2 = jnp.stack([cmp_w2_k, cmp_w2_v]).astype(CDT)
    kvc = _compress(qkv, w1, pe, w2)
    tabs = _bias_tiles(rel_bias)
    return _attention(qkv, gates, kvc, tabs)


def kernel(x, norm_mix_g, w_in, pool_w, pool_scale, cmp_pe_k, cmp_w1_k, cmp_w2_k, cmp_pe_v, cmp_w1_v, cmp_w2_v,
           rel_bias, w_out, norm_ffn_g, w_up, conv_w, conv_b, w_down, norm_final_g):
    B, S, D = x.shape
    assert B == 1 and w_in.shape[0] == 1, "single sequence, single layer"
    h = x[0]
    hn = _rmsnorm(h, norm_mix_g[0], CDT)
    u_pool = _proj(hn, w_in[0][:, :POOL_WIDTH].astype(CDT), jnp.float32, name="proj_pool")
    y_pool = _pool(u_pool, pool_w[0], pool_scale[0])
    y_nsa = _nsa_branch(hn, w_in[0], cmp_w1_k[0], cmp_pe_k[0], cmp_w2_k[0],
                        cmp_w1_v[0], cmp_pe_v[0], cmp_w2_v[0], rel_bias)
    h = _outproj(y_nsa, y_pool, w_out[0].astype(CDT), h)
    hn = _rmsnorm(h, norm_ffn_g[0], CDT)
    act = _ffn_up(hn, w_up[0].astype(CDT), conv_w[0], conv_b[0].reshape(1, -1))
    out = _ffn_down(act, w_down[0].astype(CDT), h, norm_final_g)
    return out[None]
```

```python
import functools
import math

import numpy as np
import jax
import jax.numpy as jnp
from jax import lax
from jax.experimental import pallas as pl
from jax.experimental.pallas import tpu as pltpu

POOL_WINDOWS = (2, 4, 8, 16)
POOL_GROUP = 256
POOL_WIDTH = POOL_GROUP * len(POOL_WINDOWS)
HEAD_DIM = 128
KV_GROUPS = 4
HEADS_PER_GROUP = 6
NSA_HEADS = KV_GROUPS * HEADS_PER_GROUP
Q_WIDTH = NSA_HEADS * HEAD_DIM
KV_WIDTH = KV_GROUPS * HEAD_DIM
CMP_BLOCK = 32
CMP_STRIDE = 16
SLC_BLOCK = 64
N_SELECT = 16
N_LOCAL_FORCED = 2
WINDOW = 512
N_BRANCH = 3
REL_BUCKETS = 32
REL_MAX_DIST = 128
CONV_WIDTH = 3
RMS_EPS = 1e-6
NEG = -1e30

LANES = 128
SUBLANES = 8
VMEM_LIMIT_BYTES = 56 * 1024 * 1024

CDT = jnp.bfloat16
QB = 128
PAD_ROWS = WINDOW
FAR_CHUNK = 256
NEAR_KEYS = 2 * QB
WIN_KEYS = WINDOW + QB
BAND = 16
BAND_LEAD = 9


def _cparams(*sem, flags=None):
    return pltpu.CompilerParams(dimension_semantics=sem, vmem_limit_bytes=VMEM_LIMIT_BYTES, flags=flags)


def _dot(a, b):
    return jnp.dot(a, b, preferred_element_type=jnp.float32)


def _dot_nt(a, b):
    return lax.dot_general(a, b, (((1,), (1,)), ((), ())), preferred_element_type=jnp.float32)


def _rmsnorm_body(x_ref, g_ref, o_ref):
    x = x_ref[...]
    y = x * lax.rsqrt(jnp.mean(x * x, axis=-1, keepdims=True) + RMS_EPS)
    o_ref[...] = (y * g_ref[...]).astype(o_ref.dtype)


def _rmsnorm(x, g, out_dtype, bm=256):
    S, D = x.shape
    return pl.pallas_call(
        _rmsnorm_body,
        grid=(S // bm,),
        in_specs=[pl.BlockSpec((bm, D), lambda i: (i, 0)), pl.BlockSpec((1, D), lambda i: (0, 0))],
        out_specs=pl.BlockSpec((bm, D), lambda i: (i, 0)),
        out_shape=jax.ShapeDtypeStruct((S, D), out_dtype),
        compiler_params=_cparams("parallel"),
        name="rmsnorm",
    )(x, g.reshape(1, D))


def _proj_body(a_ref, w_ref, o_ref, *, pad_tiles, n_scaled, scale, sigmoid):
    i = pl.program_id(0)
    j = pl.program_id(1)

    @pl.when(i < pad_tiles)
    def _():
        o_ref[...] = jnp.zeros_like(o_ref)

    @pl.when(i >= pad_tiles)
    def _():
        r = _dot(a_ref[...], w_ref[...])
        if n_scaled:
            r = r * jnp.where(j < n_scaled, jnp.float32(scale), jnp.float32(1.0))
        if sigmoid:
            r = jax.nn.sigmoid(r)
        o_ref[...] = r.astype(o_ref.dtype)


def _proj(a, w, out_dtype, *, pad_tiles=0, scaled_cols=0, scale=1.0, sigmoid=False, bm=512, bn=1024, name="proj"):
    S, D = a.shape
    N = w.shape[1]
    bn = min(bn, N)
    assert scaled_cols % bn == 0
    n_scaled = scaled_cols // bn
    body = functools.partial(_proj_body, pad_tiles=pad_tiles, n_scaled=n_scaled, scale=scale, sigmoid=sigmoid)
    return pl.pallas_call(
        body,
        grid=(S // bm + pad_tiles, N // bn),
        in_specs=[pl.BlockSpec((bm, D), lambda i, j: (jnp.maximum(i - pad_tiles, 0), 0)),
                  pl.BlockSpec((D, bn), lambda i, j: (0, j))],
        out_specs=pl.BlockSpec((bm, bn), lambda i, j: (i, j)),
        out_shape=jax.ShapeDtypeStruct((S + pad_tiles * bm, N), out_dtype),
        compiler_params=_cparams("parallel", "arbitrary"),
        name=name,
    )(a, w)


POOL_HALO = 16


def _pool_body(u_ref, halo_ref, w_ref, s_ref, o_ref):
    i = pl.program_id(0)
    bm = u_ref.shape[0]
    u = u_ref[...]
    halo = jnp.where(i > 0, halo_ref[...], 0.0)
    ext = jnp.concatenate([halo, u], axis=0)
    t = i * bm + lax.broadcasted_iota(jnp.int32, (bm, 1), 0)
    acc = ext
    sums = {}
    shift = 1
    while shift < POOL_WINDOWS[-1]:
        acc = acc + pltpu.roll(acc, shift, axis=0)
        shift *= 2
        sums[shift] = acc
    for gi, w in enumerate(POOL_WINDOWS):
        cols = slice(gi * POOL_GROUP, (gi + 1) * POOL_GROUP)
        cnt = jnp.minimum(t + 1, w).astype(jnp.float32)
        d = sums[w][POOL_HALO:, cols] / cnt - u[:, cols]
        y = _dot(d.astype(CDT), w_ref[gi])
        o_ref[:, cols] = (y * s_ref[:, cols]).astype(o_ref.dtype)


def _pool(u, pool_w, pool_scale, bm=512):
    S = u.shape[0]
    return pl.pallas_call(
        _pool_body,
        grid=(S // bm,),
        in_specs=[pl.BlockSpec((bm, POOL_WIDTH), lambda i: (i, 0)),
                  pl.BlockSpec((POOL_HALO, POOL_WIDTH), lambda i: (jnp.maximum(i * (bm // POOL_HALO) - 1, 0), 0)),
                  pl.BlockSpec((len(POOL_WINDOWS), POOL_GROUP, POOL_GROUP), lambda i: (0, 0, 0)),
                  pl.BlockSpec((1, POOL_WIDTH), lambda i: (0, 0))],
        out_specs=pl.BlockSpec((bm, POOL_WIDTH), lambda i: (i, 0)),
        out_shape=jax.ShapeDtypeStruct((S, POOL_WIDTH), CDT),
        compiler_params=_cparams("parallel"),
        name="pool",
    )(u, u, pool_w.astype(CDT), pool_scale.reshape(1, POOL_WIDTH))


def _compress_body(x_ref, w1_ref, pe_ref, w2_ref, o_ref, acc_ref, pe_acc_ref, *, pad_chunks):
    l = pl.program_id(2)

    @pl.when(l == 0)
    def _():
        acc_ref[...] = jnp.zeros_like(acc_ref)
        pe_acc_ref[...] = jnp.zeros_like(pe_acc_ref)

    w_lo = w1_ref[0, 0]
    w_hi = w1_ref[1, 0]
    x = x_ref[...]
    acc_ref[:, :HEAD_DIM] += _dot(x, w_lo)
    acc_ref[:, HEAD_DIM:] += _dot(x, w_hi)
    pe_lo = jnp.broadcast_to(pe_ref[0, 0], (SUBLANES, HEAD_DIM))
    pe_hi = jnp.broadcast_to(pe_ref[1, 0], (SUBLANES, HEAD_DIM))
    pe_acc_ref[...] += _dot(pe_lo, w_lo) + _dot(pe_hi, w_hi)

    @pl.when(l == CMP_STRIDE - 1)
    def _():
        n = o_ref.shape[0]
        a = acc_ref[pad_chunks:pad_chunks + n, :HEAD_DIM]
        b = acc_ref[:, HEAD_DIM:]
        b_next = pltpu.roll(b, b.shape[0] - 1, axis=0)[pad_chunks:pad_chunks + n]
        pre = a + b_next + pe_acc_ref[0:1, :]
        h = pre * jax.nn.sigmoid(pre)
        o_ref[...] = _dot(h.astype(CDT), w2_ref[0]).astype(o_ref.dtype)


def _compress(qkv, w1, pe, w2):
    rows = qkv.shape[0]
    chunks = rows // CMP_STRIDE
    pad_chunks = PAD_ROWS // CMP_STRIDE
    n_out = chunks - pad_chunks
    width = qkv.shape[1] // LANES
    x = qkv.reshape(chunks, CMP_STRIDE * qkv.shape[1])
    first = Q_WIDTH // LANES
    body = functools.partial(_compress_body, pad_chunks=pad_chunks)
    return pl.pallas_call(
        body,
        grid=(2, KV_GROUPS, CMP_STRIDE),
        in_specs=[pl.BlockSpec((chunks, LANES), lambda kv, g, l: (0, l * width + first + kv * KV_GROUPS + g)),
                  pl.BlockSpec((2, 1, HEAD_DIM, HEAD_DIM), lambda kv, g, l: (kv, l, 0, 0)),
                  pl.BlockSpec((2, 1, 1, HEAD_DIM), lambda kv, g, l: (kv, l, 0, 0)),
                  pl.BlockSpec((1, HEAD_DIM, HEAD_DIM), lambda kv, g, l: (kv, 0, 0))],
        out_specs=pl.BlockSpec((None, None, n_out, HEAD_DIM), lambda kv, g, l: (kv, g, 0, 0)),
        out_shape=jax.ShapeDtypeStruct((2, KV_GROUPS, n_out, HEAD_DIM), CDT),
        scratch_shapes=[pltpu.VMEM((chunks, 2 * HEAD_DIM), jnp.float32),
                        pltpu.VMEM((SUBLANES, HEAD_DIM), jnp.float32)],
        compiler_params=_cparams("parallel", "parallel", "arbitrary"),
        name="compress",
    )(x, w1, pe, w2)


def _rel_bucket_np(dist):
    n = np.maximum(dist, 0)
    max_exact = REL_BUCKETS // 2
    nf = np.maximum(n, 1).astype(np.float32)
    large = max_exact + (np.log(nf / max_exact) / math.log(REL_MAX_DIST / max_exact)
                         * (REL_BUCKETS - max_exact)).astype(np.int32)
    large = np.minimum(large, REL_BUCKETS - 1)
    return np.where(n < max_exact, n, large).astype(np.int32)


def _bias_index_tiles():
    q = np.arange(QB)[:, None]
    dist = QB + q - np.arange(NEAR_KEYS)[None, :]
    near = np.where(dist >= 0, _rel_bucket_np(dist), -1)
    dist = WINDOW + q - np.arange(WIN_KEYS)[None, :]
    win = np.where((dist >= 0) & (dist < WINDOW), _rel_bucket_np(dist), -1)
    w = np.arange(LANES)[None, :]
    dist = q - CMP_STRIDE * (w - BAND_LEAD) - (CMP_BLOCK - 1)
    band = np.where((dist >= 0) & (w < BAND), _rel_bucket_np(dist), -1)
    return np.concatenate([near, win, band], axis=1).astype(np.int32)


def _bias_body(tbl_ref, idx_ref, o_ref):
    h = pl.program_id(0)
    idx = idx_ref[...]
    far = tbl_ref[REL_BUCKETS - 1, h]
    val = jnp.full(idx.shape, NEG, jnp.float32)
    for b in range(REL_BUCKETS):
        val = jnp.where(idx == b, tbl_ref[b, h] - far, val)
    o_ref[...] = val


def _bias_tiles(rel_bias):
    idx = jnp.asarray(_bias_index_tiles())
    width = idx.shape[1]
    return pl.pallas_call(
        _bias_body,
        grid=(NSA_HEADS,),
        in_specs=[pl.BlockSpec(memory_space=pltpu.SMEM),
                  pl.BlockSpec((QB, width), lambda h: (0, 0))],
        out_specs=pl.BlockSpec((None, QB, width), lambda h: (h, 0, 0)),
        out_shape=jax.ShapeDtypeStruct((NSA_HEADS, QB, width), jnp.float32),
        compiler_params=_cparams("arbitrary"),
        name="bias_tiles",
    )(rel_bias, idx)


def _softmax_rows(s):
    m = jnp.max(s, axis=-1, keepdims=True)
    p = jnp.exp(s - m)
    l = jnp.sum(p, axis=-1, keepdims=True)
    return p * jnp.where(m > 0.5 * NEG, 1.0 / l, 0.0)


def _split3(x):
    hi = x.astype(CDT)
    r = x - hi.astype(jnp.float32)
    mid = r.astype(CDT)
    lo = (r - mid.astype(jnp.float32)).astype(CDT)
    return hi, mid, lo


def _select_blocks(slc_t, i):
    n_slc = slc_t.shape[0]
    t = i * QB + lax.broadcasted_iota(jnp.int32, (1, QB), 1)
    j_int = lax.broadcasted_iota(jnp.int32, (n_slc, QB), 0)
    j_idx = j_int.astype(jnp.float32)
    cur = t // SLC_BLOCK
    forced = (j_int == 0) | ((cur - j_int >= 0) & (cur - j_int < N_LOCAL_FORCED))
    score = jnp.where(forced, 1e9, jnp.where(j_int > cur, -1e9, slc_t))
    picked = jnp.zeros(score.shape, jnp.bool_)
    for _ in range(min(N_SELECT, n_slc)):
        m = jnp.max(score, axis=0, keepdims=True)
        first = jnp.min(jnp.where(score == m, j_idx, float(n_slc)), axis=0, keepdims=True)
        hit = j_idx == first
        picked = picked | hit
        score = jnp.where(hit, -3e38, score)
    return picked


def _attn_body(q_ref, gate_ref, kc_ref, vc_ref, ks_ref, vs_ref, kw_ref, vw_ref, blk_ref, tab_ref, wov_ref,
               o_ref, kaug_ref, s_ref, p_ref):
    i = pl.program_id(1)
    H = HEADS_PER_GROUP
    n_cmp = kc_ref.shape[0]
    n_slc = wov_ref.shape[0]

    @pl.when(i == 0)
    def _():
        kaug_ref[:, :HEAD_DIM] = ks_ref[...]
        kaug_ref[:, HEAD_DIM:] = blk_ref[...]

    qb = q_ref[...]
    qs = jnp.concatenate([qb[:, h * HEAD_DIM:(h + 1) * HEAD_DIM] for h in range(H)], axis=0)
    near_tab = lambda h: tab_ref[h, :, :NEAR_KEYS]
    win_tab = lambda h: tab_ref[h, :, NEAR_KEYS:NEAR_KEYS + WIN_KEYS]
    band_tab = lambda h: tab_ref[h, :, NEAR_KEYS + WIN_KEYS:]
    rows = lambda x, h: x[h * QB:(h + 1) * QB]

    s_c = _dot_nt(qs, kc_ref[...])
    n_idx = lax.broadcasted_iota(jnp.int32, (1, n_cmp), 1)
    band_lo = 8 * i - BAND_LEAD
    in_band = (n_idx >= band_lo) & (n_idx < band_lo + BAND)
    base = jnp.where(n_idx >= band_lo + BAND, NEG, 0.0)
    shift = jnp.bitwise_and(band_lo, LANES - 1)
    imp = jnp.zeros((QB, n_cmp), jnp.float32)
    p_c = []
    for h in range(H):
        rolled = pltpu.roll(band_tab(h), shift, axis=1)
        bias = jnp.where(in_band, jnp.concatenate([rolled] * (n_cmp // LANES), axis=1), base)
        p = _softmax_rows(rows(s_c, h) + bias)
        imp = imp + p
        p_c.append(p.astype(CDT))
    o_cmp = _dot(jnp.concatenate(p_c, axis=0), vc_ref[...])

    wov_t = wov_ref[...]
    slc_t = sum(_dot_nt(wov_t, part) for part in _split3(imp))
    drop = jnp.where(_select_blocks(slc_t, i), 0.0, 1.0).T
    j_lane = lax.broadcasted_iota(jnp.int32, (QB, n_slc), 1)
    drop_far = jnp.where(j_lane < (i - 1) * (QB // SLC_BLOCK), drop, 1.0)

    def augment(d):
        return jnp.concatenate([qs, jnp.concatenate([d.astype(CDT)] * H, axis=0)], axis=1)

    win0 = pl.multiple_of(i * QB, QB)
    s_w = _dot_nt(qs, kw_ref[pl.ds(win0, WIN_KEYS), :])
    r_idx = lax.broadcasted_iota(jnp.int32, (1, WIN_KEYS), 1)
    before_start = jnp.where(win0 - WINDOW + r_idx >= 0, 0.0, NEG)
    p_w = jnp.concatenate([_softmax_rows(rows(s_w, h) + win_tab(h) + before_start) for h in range(H)], axis=0)
    o_win = _dot(p_w.astype(CDT), vw_ref[pl.ds(win0, WIN_KEYS), :])

    near0 = pl.multiple_of(PAD_ROWS + (i - 1) * QB, QB)
    s_n = _dot_nt(augment(drop), kaug_ref[pl.ds(near0, NEAR_KEYS), :])
    s_n = jnp.concatenate([rows(s_n, h) + near_tab(h) for h in range(H)], axis=0)
    m0 = jnp.max(s_n, axis=-1, keepdims=True)
    p = jnp.exp(s_n - m0)
    l0 = jnp.sum(p, axis=-1, keepdims=True)
    acc0 = _dot(p.astype(CDT), vs_ref[pl.ds(near0, NEAR_KEYS), :])
    q_far = augment(drop_far)

    n_chunks = (kaug_ref.shape[0] - PAD_ROWS) // FAR_CHUNK
    n_far = (jnp.maximum(i - 1, 0) * QB + FAR_CHUNK - 1) // FAR_CHUNK

    def far_step(it, carry):
        a_prev, m, l, acc = carry
        v0 = pl.multiple_of(PAD_ROWS + jnp.maximum(it - 2, 0) * FAR_CHUNK, FAR_CHUNK)
        acc = a_prev * acc + _dot(p_ref[...], vs_ref[pl.ds(v0, FAR_CHUNK), :])
        s_prev = s_ref[...]
        m_new = jnp.maximum(m, jnp.max(s_prev, axis=-1, keepdims=True))
        a = jnp.exp(m - m_new)
        p = jnp.exp(s_prev - m_new)
        l = a * l + jnp.sum(p, axis=-1, keepdims=True)
        p_ref[...] = p.astype(CDT)
        k0 = pl.multiple_of(PAD_ROWS + jnp.minimum(it, n_chunks - 1) * FAR_CHUNK, FAR_CHUNK)
        s_ref[...] = _dot_nt(q_far, kaug_ref[pl.ds(k0, FAR_CHUNK), :]) + jnp.where(it < n_far, 0.0, NEG)
        return a, m_new, l, acc

    s_ref[...] = jnp.full(s_ref.shape, NEG, jnp.float32)
    p_ref[...] = jnp.zeros(p_ref.shape, CDT)
    trips = jnp.where(n_far > 0, n_far + 2, 0)
    _, _, l_s, acc_s = lax.fori_loop(0, trips, far_step, (jnp.exp(m0 - m0), m0, l0, acc0))
    o_slc = acc_s * (1.0 / l_s)

    gates = gate_ref[...]
    for h in range(H):
        g = lambda b: gates[:, h * N_BRANCH + b:h * N_BRANCH + b + 1]
        o = g(0) * rows(o_cmp, h) + g(1) * rows(o_slc, h) + g(2) * rows(o_win, h)
        o_ref[:, h * HEAD_DIM:(h + 1) * HEAD_DIM] = o.astype(o_ref.dtype)


def _block_membership(rows, n_slc):
    key = np.arange(rows)[:, None] - PAD_ROWS
    j = np.arange(n_slc)[None, :]
    member = np.where(key >= 0, key // SLC_BLOCK == j, True)
    return jnp.asarray(np.where(member, NEG, 0.0), dtype=CDT)


def _overlap_weights(n_cmp, n_slc):
    r = SLC_BLOCK // CMP_STRIDE
    lead = -(-CMP_BLOCK // CMP_STRIDE) - 1
    w = np.zeros((n_cmp, n_slc), np.float32)
    for o in range(-lead, r):
        s0 = o * CMP_STRIDE
        ov = max(0, min(s0 + CMP_BLOCK, SLC_BLOCK) - max(s0, 0))
        for j in range(n_slc):
            n = r * j + o
            if ov > 0 and 0 <= n < n_cmp - 1:
                w[n, j] = ov / CMP_STRIDE
    return jnp.asarray(w.T, dtype=CDT)


def _attention(qkv, gates, kvc, tabs):
    rows = qkv.shape[0]
    S = rows - PAD_ROWS
    n_cmp = kvc.shape[2]
    n_slc = S // SLC_BLOCK
    first = Q_WIDTH // LANES
    G = KV_GROUPS
    kv_spec = lambda which: pl.BlockSpec((rows, LANES), lambda g, i: (0, first + which * G + g))
    width = tabs.shape[2]
    return pl.pallas_call(
        _attn_body,
        grid=(G, S // QB),
        in_specs=[pl.BlockSpec((QB, HEADS_PER_GROUP * HEAD_DIM), lambda g, i: (i + PAD_ROWS // QB, g)),
                  pl.BlockSpec((QB, LANES), lambda g, i: (i, g)),
                  pl.BlockSpec((None, None, n_cmp, HEAD_DIM), lambda g, i: (0, g, 0, 0)),
                  pl.BlockSpec((None, None, n_cmp, HEAD_DIM), lambda g, i: (1, g, 0, 0)),
                  kv_spec(2), kv_spec(3), kv_spec(4), kv_spec(5),
                  pl.BlockSpec((rows, n_slc), lambda g, i: (0, 0)),
                  pl.BlockSpec((HEADS_PER_GROUP, QB, width), lambda g, i: (g, 0, 0)),
                  pl.BlockSpec((n_slc, n_cmp), lambda g, i: (0, 0))],
        out_specs=pl.BlockSpec((QB, HEADS_PER_GROUP * HEAD_DIM), lambda g, i: (i, g)),
        out_shape=jax.ShapeDtypeStruct((S, Q_WIDTH), CDT),
        scratch_shapes=[pltpu.VMEM((rows, HEAD_DIM + n_slc), CDT),
                        pltpu.VMEM((HEADS_PER_GROUP * QB, FAR_CHUNK), jnp.float32),
                        pltpu.VMEM((HEADS_PER_GROUP * QB, FAR_CHUNK), CDT)],
        compiler_params=_cparams("parallel", "arbitrary"),
        name="nsa",
    )(qkv, gates, kvc, kvc, qkv, qkv, qkv, qkv, _block_membership(rows, n_slc), tabs,
      _overlap_weights(n_cmp, n_slc))


def _outproj_body(a1_ref, a2_ref, w1_ref, w2_ref, x_ref, o_ref):
    o_ref[...] = x_ref[...] + _dot(a1_ref[...], w1_ref[...]) + _dot(a2_ref[...], w2_ref[...])


def _outproj(y_nsa, y_pool, w_out, x, bm=512, bn=1024):
    S, D = x.shape
    bn = min(bn, D)
    k1, k2 = y_nsa.shape[1], y_pool.shape[1]
    return pl.pallas_call(
        _outproj_body,
        grid=(S // bm, D // bn),
        in_specs=[pl.BlockSpec((bm, k1), lambda i, j: (i, 0)),
                  pl.BlockSpec((bm, k2), lambda i, j: (i, 0)),
                  pl.BlockSpec((k1, bn), lambda i, j: (0, j)),
                  pl.BlockSpec((k2, bn), lambda i, j: (k1 // k2, j)),
                  pl.BlockSpec((bm, bn), lambda i, j: (i, j))],
        out_specs=pl.BlockSpec((bm, bn), lambda i, j: (i, j)),
        out_shape=jax.ShapeDtypeStruct((S, D), jnp.float32),
        compiler_params=_cparams("parallel", "arbitrary"),
        name="outproj",
    )(y_nsa, y_pool, w_out, w_out, x)


CONV_HALO = 16


def _ffn_up_body(a_ref, halo_ref, wa_ref, wb_ref, cwa_ref, cwb_ref, cba_ref, cbb_ref, o_ref, lhs_ref):
    i = pl.program_id(0)

    @pl.when(pl.program_id(1) == 0)
    def _():
        lhs_ref[:CONV_HALO] = jnp.where(i > 0, halo_ref[...], jnp.zeros_like(halo_ref))
        lhs_ref[CONV_HALO:] = a_ref[...]

    lhs = lhs_ref[...]

    def conv(w_ref, cw_ref, cb_ref):
        u = _dot(lhs, w_ref[...])
        c = cb_ref[...] + cw_ref[CONV_WIDTH - 1:CONV_WIDTH, :] * u[CONV_HALO:]
        for k in range(1, CONV_WIDTH):
            c = c + cw_ref[CONV_WIDTH - 1 - k:CONV_WIDTH - k, :] * pltpu.roll(u, k, axis=0)[CONV_HALO:]
        return c

    ca = conv(wa_ref, cwa_ref, cba_ref)
    cb = conv(wb_ref, cwb_ref, cbb_ref)
    o_ref[...] = (ca * jax.nn.sigmoid(ca) * cb).astype(o_ref.dtype)


def _ffn_up(hn, w_up, conv_w, conv_b, bm=1024, bn=256):
    S, D = hn.shape
    F = w_up.shape[1] // 2
    nb = F // bn
    return pl.pallas_call(
        _ffn_up_body,
        grid=(S // bm, nb),
        in_specs=[pl.BlockSpec((bm, D), lambda i, j: (i, 0)),
                  pl.BlockSpec((CONV_HALO, D), lambda i, j: (jnp.maximum(i * (bm // CONV_HALO) - 1, 0), 0)),
                  pl.BlockSpec((D, bn), lambda i, j: (0, j)),
                  pl.BlockSpec((D, bn), lambda i, j: (0, nb + j)),
                  pl.BlockSpec((CONV_WIDTH, bn), lambda i, j: (0, j)),
                  pl.BlockSpec((CONV_WIDTH, bn), lambda i, j: (0, nb + j)),
                  pl.BlockSpec((1, bn), lambda i, j: (0, j)),
                  pl.BlockSpec((1, bn), lambda i, j: (0, nb + j))],
        out_specs=pl.BlockSpec((bm, bn), lambda i, j: (i, j)),
        out_shape=jax.ShapeDtypeStruct((S, F), CDT),
        scratch_shapes=[pltpu.VMEM((CONV_HALO + bm, D), CDT)],
        compiler_params=_cparams("parallel", "arbitrary"),
        name="ffn_up",
    )(hn, hn, w_up, w_up, conv_w, conv_w, conv_b, conv_b)


def _ffn_down_body(a_ref, w_ref, h_ref, g_ref, o_ref):
    k = pl.program_id(1)

    @pl.when(k == 0)
    def _():
        o_ref[...] = h_ref[...]

    o_ref[...] += _dot(a_ref[...], w_ref[...])

    @pl.when(k == pl.num_programs(1) - 1)
    def _():
        x = o_ref[...]
        y = x * lax.rsqrt(jnp.mean(x * x, axis=-1, keepdims=True) + RMS_EPS)
        o_ref[...] = y * g_ref[...]


def _ffn_down(act, w_down, h, g, bm=512, bk=256):
    S, F = act.shape
    D = h.shape[1]
    return pl.pallas_call(
        _ffn_down_body,
        grid=(S // bm, F // bk),
        in_specs=[pl.BlockSpec((bm, bk), lambda i, k: (i, k)),
                  pl.BlockSpec((bk, D), lambda i, k: (k, 0)),
                  pl.BlockSpec((bm, D), lambda i, k: (i, 0)),
                  pl.BlockSpec((1, D), lambda i, k: (0, 0))],
        out_specs=pl.BlockSpec((bm, D), lambda i, k: (i, 0)),
        out_shape=jax.ShapeDtypeStruct((S, D), jnp.float32),
        compiler_params=_cparams("parallel", "arbitrary"),
        name="ffn_down",
    )(act, w_down, h, g.reshape(1, D))


def _gate_weight(w_gate):
    D = w_gate.shape[0]
    per = HEADS_PER_GROUP * N_BRANCH
    w = w_gate.reshape(D, KV_GROUPS, per)
    w = jnp.pad(w, ((0, 0), (0, 0), (0, LANES - per)))
    return w.reshape(D, KV_GROUPS * LANES)


def _nsa_branch(hn, w_in, cmp_w1_k, cmp_pe_k, cmp_w2_k, cmp_w1_v, cmp_pe_v, cmp_w2_v, rel_bias):
    qkv_cols = Q_WIDTH + 6 * KV_WIDTH
    w_qkv = w_in[:, POOL_WIDTH:POOL_WIDTH + qkv_cols].astype(CDT)
    w_gate = _gate_weight(w_in[:, POOL_WIDTH + qkv_cols:]).astype(CDT)
    bm = 512
    qkv = _proj(hn, w_qkv, CDT, pad_tiles=PAD_ROWS // bm, scaled_cols=Q_WIDTH, scale=HEAD_DIM ** -0.5,
                bm=bm, name="proj_qkv")
    gates = _proj(hn, w_gate, jnp.float32, sigmoid=True, bm=bm, name="proj_gate")
    half = lambda w: w.reshape(2, CMP_STRIDE, *w.shape[1:])
    w1 = jnp.stack([half(cmp_w1_k), half(cmp_w1_v)]).astype(CDT)
    pe = jnp.stack([half(cmp_pe_k), half(cmp_pe_v)]).astype(CDT)
    w1 = w1.reshape(2 * 2, CMP_STRIDE, HEAD_DIM, HEAD_DIM)
    pe = pe.reshape(2 * 2, CMP_STRIDE, 1, HEAD_DIM)
    w2 = jnp.stack([cmp_w2_k, cmp_w2_v]).astype(CDT)
    kvc = _compress(qkv, w1, pe, w2)
    tabs = _bias_tiles(rel_bias)
    return _attention(qkv, gates, kvc, tabs)


def kernel(x, norm_mix_g, w_in, pool_w, pool_scale, cmp_pe_k, cmp_w1_k, cmp_w2_k, cmp_pe_v, cmp_w1_v, cmp_w2_v,
           rel_bias, w_out, norm_ffn_g, w_up, conv_w, conv_b, w_down, norm_final_g):
    B, S, D = x.shape
    assert B == 1 and w_in.shape[0] == 1, "single sequence, single layer"
    h = x[0]
    hn = _rmsnorm(h, norm_mix_g[0], CDT)
    u_pool = _proj(hn, w_in[0][:, :POOL_WIDTH].astype(CDT), jnp.float32, name="proj_pool")
    y_pool = _pool(u_pool, pool_w[0], pool_scale[0])
    y_nsa = _nsa_branch(hn, w_in[0], cmp_w1_k[0], cmp_pe_k[0], cmp_w2_k[0],
                        cmp_w1_v[0], cmp_pe_v[0], cmp_w2_v[0], rel_bias)
    h = _outproj(y_nsa, y_pool, w_out[0].astype(CDT), h)
    hn = _rmsnorm(h, norm_ffn_g[0], CDT)
    act = _ffn_up(hn, w_up[0].astype(CDT), conv_w[0], conv_b[0].reshape(1, -1))
    out = _ffn_down(act, w_down[0].astype(CDT), h, norm_final_g)
    return out[None]
```

```python
import functools
import math

import numpy as np
import jax
import jax.numpy as jnp
from jax import lax
from jax.experimental import pallas as pl
from jax.experimental.pallas import tpu as pltpu

POOL_WINDOWS = (2, 4, 8, 16)
POOL_GROUP = 256
POOL_WIDTH = POOL_GROUP * len(POOL_WINDOWS)
HEAD_DIM = 128
KV_GROUPS = 4
HEADS_PER_GROUP = 6
NSA_HEADS = KV_GROUPS * HEADS_PER_GROUP
Q_WIDTH = NSA_HEADS * HEAD_DIM
KV_WIDTH = KV_GROUPS * HEAD_DIM
CMP_BLOCK = 32
CMP_STRIDE = 16
SLC_BLOCK = 64
N_SELECT = 16
N_LOCAL_FORCED = 2
WINDOW = 512
N_BRANCH = 3
REL_BUCKETS = 32
REL_MAX_DIST = 128
CONV_WIDTH = 3
RMS_EPS = 1e-6
NEG = -1e30
LOG2E = math.log2(math.e)

LANES = 128
SUBLANES = 8
VMEM_LIMIT_BYTES = 56 * 1024 * 1024

CDT = jnp.bfloat16
QB = 128
PAD_ROWS = WINDOW
FAR_CHUNK = 512
NEAR_KEYS = 2 * QB
WIN_KEYS = WINDOW + QB
CMP_PER_TILE = QB // CMP_STRIDE
BAND_BACK = 16
BAND_ROWS = 24
BAND_TABLE = BAND_ROWS + BAND_BACK


def _cparams(*sem, flags=None):
    return pltpu.CompilerParams(dimension_semantics=sem, vmem_limit_bytes=VMEM_LIMIT_BYTES, flags=flags)


def _dot(a, b):
    return jnp.dot(a, b, preferred_element_type=jnp.float32)


def _dot_nt(a, b):
    return lax.dot_general(a, b, (((1,), (1,)), ((), ())), preferred_element_type=jnp.float32)


def _rmsnorm_body(x_ref, g_ref, o_ref):
    x = x_ref[...]
    y = x * lax.rsqrt(jnp.mean(x * x, axis=-1, keepdims=True) + RMS_EPS)
    o_ref[...] = (y * g_ref[...]).astype(o_ref.dtype)


def _rmsnorm(x, g, out_dtype, bm=256):
    S, D = x.shape
    return pl.pallas_call(
        _rmsnorm_body,
        grid=(S // bm,),
        in_specs=[pl.BlockSpec((bm, D), lambda i: (i, 0)), pl.BlockSpec((1, D), lambda i: (0, 0))],
        out_specs=pl.BlockSpec((bm, D), lambda i: (i, 0)),
        out_shape=jax.ShapeDtypeStruct((S, D), out_dtype),
        compiler_params=_cparams("parallel"),
        name="rmsnorm",
    )(x, g.reshape(1, D))


def _proj_body(a_ref, w_ref, o_ref, *, pad_tiles, n_scaled, scale, sigmoid):
    i = pl.program_id(0)
    j = pl.program_id(1)

    @pl.when(i < pad_tiles)
    def _():
        o_ref[...] = jnp.zeros_like(o_ref)

    @pl.when(i >= pad_tiles)
    def _():
        r = _dot(a_ref[...], w_ref[...])
        if n_scaled:
            r = r * jnp.where(j < n_scaled, jnp.float32(scale), jnp.float32(1.0))
        if sigmoid:
            r = jax.nn.sigmoid(r)
        o_ref[...] = r.astype(o_ref.dtype)


def _proj(a, w, out_dtype, *, pad_tiles=0, scaled_cols=0, scale=1.0, sigmoid=False, bm=512, bn=1024, name="proj"):
    S, D = a.shape
    N = w.shape[1]
    bn = min(bn, N)
    assert scaled_cols % bn == 0
    n_scaled = scaled_cols // bn
    body = functools.partial(_proj_body, pad_tiles=pad_tiles, n_scaled=n_scaled, scale=scale, sigmoid=sigmoid)
    return pl.pallas_call(
        body,
        grid=(S // bm + pad_tiles, N // bn),
        in_specs=[pl.BlockSpec((bm, D), lambda i, j: (jnp.maximum(i - pad_tiles, 0), 0)),
                  pl.BlockSpec((D, bn), lambda i, j: (0, j))],
        out_specs=pl.BlockSpec((bm, bn), lambda i, j: (i, j)),
        out_shape=jax.ShapeDtypeStruct((S + pad_tiles * bm, N), out_dtype),
        compiler_params=_cparams("parallel", "arbitrary"),
        name=name,
    )(a, w)


POOL_HALO = 16


def _pool_body(u_ref, halo_ref, w_ref, s_ref, o_ref):
    i = pl.program_id(0)
    bm = u_ref.shape[0]
    u = u_ref[...]
    halo = jnp.where(i > 0, halo_ref[...], 0.0)
    ext = jnp.concatenate([halo, u], axis=0)
    t = i * bm + lax.broadcasted_iota(jnp.int32, (bm, 1), 0)
    acc = ext
    sums = {}
    shift = 1
    while shift < POOL_WINDOWS[-1]:
        acc = acc + pltpu.roll(acc, shift, axis=0)
        shift *= 2
        sums[shift] = acc
    for gi, w in enumerate(POOL_WINDOWS):
        cols = slice(gi * POOL_GROUP, (gi + 1) * POOL_GROUP)
        cnt = jnp.minimum(t + 1, w).astype(jnp.float32)
        d = sums[w][POOL_HALO:, cols] / cnt - u[:, cols]
        y = _dot(d.astype(CDT), w_ref[gi])
        o_ref[:, cols] = (y * s_ref[:, cols]).astype(o_ref.dtype)


def _pool(u, pool_w, pool_scale, bm=512):
    S = u.shape[0]
    return pl.pallas_call(
        _pool_body,
        grid=(S // bm,),
        in_specs=[pl.BlockSpec((bm, POOL_WIDTH), lambda i: (i, 0)),
                  pl.BlockSpec((POOL_HALO, POOL_WIDTH), lambda i: (jnp.maximum(i * (bm // POOL_HALO) - 1, 0), 0)),
                  pl.BlockSpec((len(POOL_WINDOWS), POOL_GROUP, POOL_GROUP), lambda i: (0, 0, 0)),
                  pl.BlockSpec((1, POOL_WIDTH), lambda i: (0, 0))],
        out_specs=pl.BlockSpec((bm, POOL_WIDTH), lambda i: (i, 0)),
        out_shape=jax.ShapeDtypeStruct((S, POOL_WIDTH), CDT),
        compiler_params=_cparams("parallel"),
        name="pool",
    )(u, u, pool_w.astype(CDT), pool_scale.reshape(1, POOL_WIDTH))


def _compress_body(x_ref, w1_ref, pe_ref, w2_ref, o_ref, acc_ref, pe_acc_ref, *, pad_chunks):
    l = pl.program_id(2)

    @pl.when(l == 0)
    def _():
        acc_ref[...] = jnp.zeros_like(acc_ref)
        pe_acc_ref[...] = jnp.zeros_like(pe_acc_ref)

    w_lo = w1_ref[0, 0]
    w_hi = w1_ref[1, 0]
    x = x_ref[...]
    acc_ref[:, :HEAD_DIM] += _dot(x, w_lo)
    acc_ref[:, HEAD_DIM:] += _dot(x, w_hi)
    pe_lo = jnp.broadcast_to(pe_ref[0, 0], (SUBLANES, HEAD_DIM))
    pe_hi = jnp.broadcast_to(pe_ref[1, 0], (SUBLANES, HEAD_DIM))
    pe_acc_ref[...] += _dot(pe_lo, w_lo) + _dot(pe_hi, w_hi)

    @pl.when(l == CMP_STRIDE - 1)
    def _():
        n = o_ref.shape[0]
        a = acc_ref[pad_chunks:pad_chunks + n, :HEAD_DIM]
        b = acc_ref[:, HEAD_DIM:]
        b_next = pltpu.roll(b, b.shape[0] - 1, axis=0)[pad_chunks:pad_chunks + n]
        pre = a + b_next + pe_acc_ref[0:1, :]
        h = pre * jax.nn.sigmoid(pre)
        o_ref[...] = _dot(h.astype(CDT), w2_ref[0]).astype(o_ref.dtype)


def _compress(qkv, w1, pe, w2):
    rows = qkv.shape[0]
    chunks = rows // CMP_STRIDE
    pad_chunks = PAD_ROWS // CMP_STRIDE
    n_out = chunks - pad_chunks
    width = 2 * KV_GROUPS
    x = qkv[:, Q_WIDTH:Q_WIDTH + 2 * KV_WIDTH].reshape(chunks, CMP_STRIDE * 2 * KV_WIDTH)
    body = functools.partial(_compress_body, pad_chunks=pad_chunks)
    return pl.pallas_call(
        body,
        grid=(2, KV_GROUPS, CMP_STRIDE),
        in_specs=[pl.BlockSpec((chunks, LANES), lambda kv, g, l: (0, l * width + kv * KV_GROUPS + g)),
                  pl.BlockSpec((2, 1, HEAD_DIM, HEAD_DIM), lambda kv, g, l: (kv, l, 0, 0)),
                  pl.BlockSpec((2, 1, 1, HEAD_DIM), lambda kv, g, l: (kv, l, 0, 0)),
                  pl.BlockSpec((1, HEAD_DIM, HEAD_DIM), lambda kv, g, l: (kv, 0, 0))],
        out_specs=pl.BlockSpec((None, None, n_out, HEAD_DIM), lambda kv, g, l: (kv, g, 0, 0)),
        out_shape=jax.ShapeDtypeStruct((2, KV_GROUPS, n_out, HEAD_DIM), CDT),
        scratch_shapes=[pltpu.VMEM((chunks, 2 * HEAD_DIM), jnp.float32),
                        pltpu.VMEM((SUBLANES, HEAD_DIM), jnp.float32)],
        compiler_params=_cparams("parallel", "parallel", "arbitrary"),
        name="compress",
    )(x, w1, pe, w2)


def _rel_bucket_np(dist):
    n = np.maximum(dist, 0)
    max_exact = REL_BUCKETS // 2
    nf = np.maximum(n, 1).astype(np.float32)
    large = max_exact + (np.log(nf / max_exact) / math.log(REL_MAX_DIST / max_exact)
                         * (REL_BUCKETS - max_exact)).astype(np.int32)
    large = np.minimum(large, REL_BUCKETS - 1)
    return np.where(n < max_exact, n, large).astype(np.int32)


def _bias_index_tiles():
    q = np.arange(QB)[:, None]
    dist = QB + q - np.arange(NEAR_KEYS)[None, :]
    near = np.where(dist >= 0, _rel_bucket_np(dist), -1)
    dist = WINDOW + q - np.arange(WIN_KEYS)[None, :]
    win = np.where((dist >= 0) & (dist < WINDOW), _rel_bucket_np(dist), -1)
    w = np.arange(LANES)[None, :]
    dist = q - CMP_STRIDE * (w - BAND_BACK) - (CMP_BLOCK - 1)
    band = np.where(dist >= 0, _rel_bucket_np(dist), -1)
    return np.concatenate([near, win, band], axis=1).astype(np.int32)


def _bias_body(tbl_ref, idx_ref, o_ref):
    h = pl.program_id(0)
    idx = idx_ref[...]
    far = tbl_ref[REL_BUCKETS - 1, h]
    val = jnp.full(idx.shape, NEG, jnp.float32)
    for b in range(REL_BUCKETS):
        val = jnp.where(idx == b, (tbl_ref[b, h] - far) * LOG2E, val)
    o_ref[...] = val


def _bias_tiles(rel_bias):
    idx = jnp.asarray(_bias_index_tiles())
    width = idx.shape[1]
    return pl.pallas_call(
        _bias_body,
        grid=(NSA_HEADS,),
        in_specs=[pl.BlockSpec(memory_space=pltpu.SMEM),
                  pl.BlockSpec((QB, width), lambda h: (0, 0))],
        out_specs=pl.BlockSpec((None, QB, width), lambda h: (h, 0, 0)),
        out_shape=jax.ShapeDtypeStruct((NSA_HEADS, QB, width), jnp.float32),
        compiler_params=_cparams("arbitrary"),
        name="bias_tiles",
    )(rel_bias, idx)


def _softmax_cols(s):
    m = jnp.max(s, axis=0, keepdims=True)
    p = jnp.exp2(s - m)
    l = jnp.sum(p, axis=0, keepdims=True)
    return p, jnp.where(m > 0.5 * NEG, 1.0 / l, 0.0)


def _split3(x):
    hi = x.astype(CDT)
    r = x - hi.astype(jnp.float32)
    mid = r.astype(CDT)
    lo = (r - mid.astype(jnp.float32)).astype(CDT)
    return hi, mid, lo


def _select_blocks(slc_t, i):
    n_slc = slc_t.shape[0]
    t = i * QB + lax.broadcasted_iota(jnp.int32, (1, QB), 1)
    j_int = lax.broadcasted_iota(jnp.int32, (n_slc, QB), 0)
    j_idx = j_int.astype(jnp.float32)
    cur = t // SLC_BLOCK
    forced = (j_int == 0) | ((cur - j_int >= 0) & (cur - j_int < N_LOCAL_FORCED))
    score = jnp.where(forced, 1e9, jnp.where(j_int > cur, -1e9, slc_t))
    picked = jnp.zeros(score.shape, jnp.bool_)
    for _ in range(min(N_SELECT, n_slc)):
        m = jnp.max(score, axis=0, keepdims=True)
        first = jnp.min(jnp.where(score == m, j_idx, float(n_slc)), axis=0, keepdims=True)
        hit = j_idx == first
        picked = picked | hit
        score = jnp.where(hit, -3e38, score)
    return picked


def _attn_body(q_ref, gate_ref, kc_ref, vct_ref, ks_ref, vst_ref, kw_ref, vwt_ref, blk_ref, tab_ref, wov_ref,
               o_ref, kaug_ref, kwaug_ref, sc_ref, pc_ref, sw_ref, pw_ref, s_ref, p_ref, acc_ref):
    i = pl.program_id(1)
    H = HEADS_PER_GROUP
    n_cmp = kc_ref.shape[0]
    n_slc = wov_ref.shape[0]
    lanes = H * QB
    head = lambda x, h: x[:, h * QB:(h + 1) * QB]

    @pl.when(i == 0)
    def _():
        kaug_ref[:, :HEAD_DIM] = ks_ref[...]
        kaug_ref[:, HEAD_DIM:] = blk_ref[...]
        kwaug_ref[:, :HEAD_DIM] = kw_ref[...]
        row = lax.broadcasted_iota(jnp.int32, (kwaug_ref.shape[0], LANES), 0)
        kwaug_ref[:, HEAD_DIM:] = jnp.where(row < PAD_ROWS, 1.0, 0.0).astype(CDT)

    qb = q_ref[...]
    q_t = jnp.concatenate([qb[:, h * HEAD_DIM:(h + 1) * HEAD_DIM].astype(jnp.float32).T.astype(CDT)
                           for h in range(H)], axis=1)

    def augment(extra):
        return jnp.concatenate([q_t, extra.astype(CDT)], axis=0)

    def values_t(ref, chunk0, n):
        return jnp.concatenate([ref[chunk0 + c] for c in range(n)], axis=1)

    grp = lax.broadcasted_iota(jnp.int32, (LANES, lanes), 0)
    sc_ref[...] = _dot(kc_ref[...], augment(jnp.where(grp > i, NEG, 0.0)))
    band0 = jnp.maximum(CMP_PER_TILE * i - BAND_BACK, 0)
    tab0 = band0 - (CMP_PER_TILE * i - BAND_BACK)
    band_rows = pl.ds(pl.multiple_of(band0, SUBLANES), BAND_ROWS)
    sc_ref[band_rows, :] += tab_ref[pl.ds(pl.multiple_of(NEAR_KEYS + WIN_KEYS + tab0, SUBLANES), BAND_ROWS), :]
    p_c, scale_c = _softmax_cols(sc_ref[...])
    p_c = p_c * scale_c
    imp_t = functools.reduce(lambda a, b: a + b, [head(p_c, h) for h in range(H)])
    pc_ref[...] = p_c.astype(CDT)
    o_cmp_t = _dot(vct_ref[...], pc_ref[...])

    wov_t = wov_ref[...]
    slc_t = sum(_dot(wov_t, part) for part in _split3(imp_t))
    drop = jnp.where(_select_blocks(slc_t, i), 0.0, 1.0)
    j_row = lax.broadcasted_iota(jnp.int32, (n_slc, QB), 0)
    drop_far = jnp.where(j_row < (i - 1) * (QB // SLC_BLOCK), drop, 1.0)
    tile_heads = lambda d: jnp.concatenate([d] * H, axis=1)

    win0 = pl.multiple_of(i * QB, QB)
    first_row = lax.broadcasted_iota(jnp.int32, (LANES, lanes), 0) == 0
    sw_ref[...] = (_dot(kwaug_ref[pl.ds(win0, WIN_KEYS), :], augment(jnp.where(first_row, NEG, 0.0)))
                   + tab_ref[NEAR_KEYS:NEAR_KEYS + WIN_KEYS])
    p_w, scale_w = _softmax_cols(sw_ref[...])
    pw_ref[...] = p_w.astype(CDT)
    o_win_t = _dot(values_t(vwt_ref, win0 // LANES, WIN_KEYS // LANES), pw_ref[...]) * scale_w

    near0 = pl.multiple_of(PAD_ROWS + (i - 1) * QB, QB)
    near = slice(0, NEAR_KEYS)
    s_ref[near] = _dot(kaug_ref[pl.ds(near0, NEAR_KEYS), :], augment(tile_heads(drop))) + tab_ref[:NEAR_KEYS]
    s_n = s_ref[near]
    m0 = jnp.max(s_n, axis=0, keepdims=True)
    p = jnp.exp2(s_n - m0)
    l0 = jnp.sum(p, axis=0, keepdims=True)
    p_ref[near] = p.astype(CDT)
    acc_ref[...] = _dot(values_t(vst_ref, near0 // LANES, NEAR_KEYS // LANES), p_ref[near])
    q_far = augment(tile_heads(drop_far))

    n_far = (jnp.maximum(i - 1, 0) * QB + FAR_CHUNK - 1) // FAR_CHUNK
    far_lanes = FAR_CHUNK // LANES

    def far_step(it, carry):
        a_prev, m, l = carry
        v0 = PAD_ROWS // LANES + jnp.maximum(it - 2, 0) * far_lanes
        acc_ref[...] = a_prev * acc_ref[...] + _dot(values_t(vst_ref, v0, far_lanes), p_ref[...])
        s_prev = s_ref[...]
        m_new = jnp.maximum(m, jnp.max(s_prev, axis=0, keepdims=True))
        a = jnp.exp2(m - m_new)
        p = jnp.exp2(s_prev - m_new)
        l = a * l + jnp.sum(p, axis=0, keepdims=True)
        p_ref[...] = p.astype(CDT)
        k0 = pl.multiple_of(jnp.where(it < n_far, PAD_ROWS + it * FAR_CHUNK, 0), FAR_CHUNK)
        s_ref[...] = _dot(kaug_ref[pl.ds(k0, FAR_CHUNK), :], q_far)
        return a, m_new, l

    s_ref[...] = jnp.full(s_ref.shape, NEG, jnp.float32)
    p_ref[...] = jnp.zeros(p_ref.shape, CDT)
    trips = jnp.where(n_far > 0, n_far + 2, 0)
    _, _, l_s = lax.fori_loop(0, trips, far_step, (jnp.exp2(m0 - m0), m0, l0))
    o_slc_t = acc_ref[...] * (1.0 / l_s)

    gates_t = gate_ref[...].T
    for h in range(H):
        g = lambda b: gates_t[h * N_BRANCH + b:h * N_BRANCH + b + 1, :]
        o_t = g(0) * head(o_cmp_t, h) + g(1) * head(o_slc_t, h) + g(2) * head(o_win_t, h)
        o_ref[:, h * HEAD_DIM:(h + 1) * HEAD_DIM] = o_t.T.astype(o_ref.dtype)


def _block_membership(rows, n_slc):
    key = np.arange(rows)[:, None] - PAD_ROWS
    j = np.arange(n_slc)[None, :]
    member = np.where(key >= 0, key // SLC_BLOCK == j, True)
    return jnp.asarray(np.where(member, NEG, 0.0), dtype=CDT)


def _overlap_weights(n_cmp, n_slc):
    r = SLC_BLOCK // CMP_STRIDE
    lead = -(-CMP_BLOCK // CMP_STRIDE) - 1
    w = np.zeros((n_cmp, n_slc), np.float32)
    for o in range(-lead, r):
        s0 = o * CMP_STRIDE
        ov = max(0, min(s0 + CMP_BLOCK, SLC_BLOCK) - max(s0, 0))
        for j in range(n_slc):
            n = r * j + o
            if ov > 0 and 0 <= n < n_cmp - 1:
                w[n, j] = ov / CMP_STRIDE
    return jnp.asarray(w.T, dtype=CDT)


def _tile_group_columns(n_cmp):
    n = np.arange(n_cmp)[:, None]
    return jnp.asarray(n // CMP_PER_TILE == np.arange(LANES)[None, :], dtype=CDT)


def _attention(qkv, gates, kvc, tabs):
    rows = qkv.shape[0]
    S = rows - PAD_ROWS
    n_cmp = kvc.shape[2]
    n_slc = S // SLC_BLOCK
    assert n_cmp // CMP_PER_TILE <= LANES
    first = Q_WIDTH // LANES
    G = KV_GROUPS
    lanes = HEADS_PER_GROUP * QB
    kv_spec = lambda which: pl.BlockSpec((rows, LANES), lambda g, i: (0, first + which * G + g))
    vt_spec = lambda which: pl.BlockSpec((None, None, rows // LANES, HEAD_DIM, LANES), lambda g, i: (which, g, 0, 0, 0))
    width = tabs.shape[2]
    tabs_t = tabs.reshape(G, HEADS_PER_GROUP, QB, width).transpose(0, 3, 1, 2).reshape(G, width, lanes)
    kc_aug = jnp.concatenate([kvc[0], jnp.broadcast_to(_tile_group_columns(n_cmp), (G, n_cmp, LANES))], axis=-1)
    vc_t = kvc[1].transpose(0, 2, 1)
    v_cols = jnp.stack([qkv[:, (first + w * G) * LANES:(first + (w + 1) * G) * LANES] for w in (3, 5)])
    v_t = v_cols.reshape(2, rows // LANES, LANES, G, HEAD_DIM).transpose(0, 3, 1, 4, 2)
    return pl.pallas_call(
        _attn_body,
        grid=(G, S // QB),
        in_specs=[pl.BlockSpec((QB, HEADS_PER_GROUP * HEAD_DIM), lambda g, i: (i + PAD_ROWS // QB, g)),
                  pl.BlockSpec((QB, LANES), lambda g, i: (i, g)),
                  pl.BlockSpec((None, n_cmp, HEAD_DIM + LANES), lambda g, i: (g, 0, 0)),
                  pl.BlockSpec((None, HEAD_DIM, n_cmp), lambda g, i: (g, 0, 0)),
                  kv_spec(2), vt_spec(0), kv_spec(4), vt_spec(1),
                  pl.BlockSpec((rows, n_slc), lambda g, i: (0, 0)),
                  pl.BlockSpec((None, width, lanes), lambda g, i: (g, 0, 0)),
                  pl.BlockSpec((n_slc, n_cmp), lambda g, i: (0, 0))],
        out_specs=pl.BlockSpec((QB, HEADS_PER_GROUP * HEAD_DIM), lambda g, i: (i, g)),
        out_shape=jax.ShapeDtypeStruct((S, Q_WIDTH), CDT),
        scratch_shapes=[pltpu.VMEM((rows, HEAD_DIM + n_slc), CDT),
                        pltpu.VMEM((rows, HEAD_DIM + LANES), CDT),
                        pltpu.VMEM((n_cmp, lanes), jnp.float32),
                        pltpu.VMEM((n_cmp, lanes), CDT),
                        pltpu.VMEM((WIN_KEYS, lanes), jnp.float32),
                        pltpu.VMEM((WIN_KEYS, lanes), CDT),
                        pltpu.VMEM((FAR_CHUNK, lanes), jnp.float32),
                        pltpu.VMEM((FAR_CHUNK, lanes), CDT),
                        pltpu.VMEM((HEAD_DIM, lanes), jnp.float32)],
        compiler_params=_cparams("parallel", "arbitrary"),
        name="nsa",
    )(qkv, gates, kc_aug, vc_t, qkv, v_t, qkv, v_t, _block_membership(rows, n_slc), tabs_t,
      _overlap_weights(n_cmp, n_slc))


def _outproj_body(a1_ref, a2_ref, w1_ref, w2_ref, x_ref, o_ref):
    o_ref[...] = x_ref[...] + _dot(a1_ref[...], w1_ref[...]) + _dot(a2_ref[...], w2_ref[...])


def _outproj(y_nsa, y_pool, w_out, x, bm=512, bn=1024):
    S, D = x.shape
    bn = min(bn, D)
    k1, k2 = y_nsa.shape[1], y_pool.shape[1]
    return pl.pallas_call(
        _outproj_body,
        grid=(S // bm, D // bn),
        in_specs=[pl.BlockSpec((bm, k1), lambda i, j: (i, 0)),
                  pl.BlockSpec((bm, k2), lambda i, j: (i, 0)),
                  pl.BlockSpec((k1, bn), lambda i, j: (0, j)),
                  pl.BlockSpec((k2, bn), lambda i, j: (k1 // k2, j)),
                  pl.BlockSpec((bm, bn), lambda i, j: (i, j))],
        out_specs=pl.BlockSpec((bm, bn), lambda i, j: (i, j)),
        out_shape=jax.ShapeDtypeStruct((S, D), jnp.float32),
        compiler_params=_cparams("parallel", "arbitrary"),
        name="outproj",
    )(y_nsa, y_pool, w_out, w_out, x)


CONV_HALO = 16


def _ffn_up_body(a_ref, halo_ref, wa_ref, wb_ref, cwa_ref, cwb_ref, cba_ref, cbb_ref, o_ref, lhs_ref):
    i = pl.program_id(0)

    @pl.when(pl.program_id(1) == 0)
    def _():
        lhs_ref[:CONV_HALO] = jnp.where(i > 0, halo_ref[...], jnp.zeros_like(halo_ref))
        lhs_ref[CONV_HALO:] = a_ref[...]

    lhs = lhs_ref[...]

    def conv(w_ref, cw_ref, cb_ref):
        u = _dot(lhs, w_ref[...])
        c = cb_ref[...] + cw_ref[CONV_WIDTH - 1:CONV_WIDTH, :] * u[CONV_HALO:]
        for k in range(1, CONV_WIDTH):
            c = c + cw_ref[CONV_WIDTH - 1 - k:CONV_WIDTH - k, :] * pltpu.roll(u, k, axis=0)[CONV_HALO:]
        return c

    ca = conv(wa_ref, cwa_ref, cba_ref)
    cb = conv(wb_ref, cwb_ref, cbb_ref)
    o_ref[...] = (ca * jax.nn.sigmoid(ca) * cb).astype(o_ref.dtype)


def _ffn_up(hn, w_up, conv_w, conv_b, bm=1024, bn=256):
    S, D = hn.shape
    F = w_up.shape[1] // 2
    nb = F // bn
    return pl.pallas_call(
        _ffn_up_body,
        grid=(S // bm, nb),
        in_specs=[pl.BlockSpec((bm, D), lambda i, j: (i, 0)),
                  pl.BlockSpec((CONV_HALO, D), lambda i, j: (jnp.maximum(i * (bm // CONV_HALO) - 1, 0), 0)),
                  pl.BlockSpec((D, bn), lambda i, j: (0, j)),
                  pl.BlockSpec((D, bn), lambda i, j: (0, nb + j)),
                  pl.BlockSpec((CONV_WIDTH, bn), lambda i, j: (0, j)),
                  pl.BlockSpec((CONV_WIDTH, bn), lambda i, j: (0, nb + j)),
                  pl.BlockSpec((1, bn), lambda i, j: (0, j)),
                  pl.BlockSpec((1, bn), lambda i, j: (0, nb + j))],
        out_specs=pl.BlockSpec((bm, bn), lambda i, j: (i, j)),
        out_shape=jax.ShapeDtypeStruct((S, F), CDT),
        scratch_shapes=[pltpu.VMEM((CONV_HALO + bm, D), CDT)],
        compiler_params=_cparams("parallel", "arbitrary"),
        name="ffn_up",
    )(hn, hn, w_up, w_up, conv_w, conv_w, conv_b, conv_b)


def _ffn_down_body(a_ref, w_ref, h_ref, g_ref, o_ref):
    k = pl.program_id(1)

    @pl.when(k == 0)
    def _():
        o_ref[...] = h_ref[...]

    o_ref[...] += _dot(a_ref[...], w_ref[...])

    @pl.when(k == pl.num_programs(1) - 1)
    def _():
        x = o_ref[...]
        y = x * lax.rsqrt(jnp.mean(x * x, axis=-1, keepdims=True) + RMS_EPS)
        o_ref[...] = y * g_ref[...]


def _ffn_down(act, w_down, h, g, bm=512, bk=256):
    S, F = act.shape
    D = h.shape[1]
    return pl.pallas_call(
        _ffn_down_body,
        grid=(S // bm, F // bk),
        in_specs=[pl.BlockSpec((bm, bk), lambda i, k: (i, k)),
                  pl.BlockSpec((bk, D), lambda i, k: (k, 0)),
                  pl.BlockSpec((bm, D), lambda i, k: (i, 0)),
                  pl.BlockSpec((1, D), lambda i, k: (0, 0))],
        out_specs=pl.BlockSpec((bm, D), lambda i, k: (i, 0)),
        out_shape=jax.ShapeDtypeStruct((S, D), jnp.float32),
        compiler_params=_cparams("parallel", "arbitrary"),
        name="ffn_down",
    )(act, w_down, h, g.reshape(1, D))


def _gate_weight(w_gate):
    D = w_gate.shape[0]
    per = HEADS_PER_GROUP * N_BRANCH
    w = w_gate.reshape(D, KV_GROUPS, per)
    w = jnp.pad(w, ((0, 0), (0, 0), (0, LANES - per)))
    return w.reshape(D, KV_GROUPS * LANES)


def _nsa_branch(hn, w_in, cmp_w1_k, cmp_pe_k, cmp_w2_k, cmp_w1_v, cmp_pe_v, cmp_w2_v, rel_bias):
    qkv_cols = Q_WIDTH + 6 * KV_WIDTH
    w_qkv = w_in[:, POOL_WIDTH:POOL_WIDTH + qkv_cols].astype(CDT)
    w_gate = _gate_weight(w_in[:, POOL_WIDTH + qkv_cols:]).astype(CDT)
    bm = 512
    qkv = _proj(hn, w_qkv, CDT, pad_tiles=PAD_ROWS // bm, scaled_cols=Q_WIDTH, scale=HEAD_DIM ** -0.5 * LOG2E,
                bm=bm, name="proj_qkv")
    gates = _proj(hn, w_gate, jnp.float32, sigmoid=True, bm=bm, name="proj_gate")
    half = lambda w: w.reshape(2, CMP_STRIDE, *w.shape[1:])
    w1 = jnp.stack([half(cmp_w1_k), half(cmp_w1_v)]).astype(CDT)
    pe = jnp.stack([half(cmp_pe_k), half(cmp_pe_v)]).astype(CDT)
    w1 = w1.reshape(2 * 2, CMP_STRIDE, HEAD_DIM, HEAD_DIM)
    pe = pe.reshape(2 * 2, CMP_STRIDE, 1, HEAD_DIM)
    w2 = jnp.stack([cmp_w2_k, cmp_w2_v]).astype(CDT)
    kvc = _compress(qkv, w1, pe, w2)
    tabs = _bias_tiles(rel_bias)
    return _attention(qkv, gates, kvc, tabs)


def kernel(x, norm_mix_g, w_in, pool_w, pool_scale, cmp_pe_k, cmp_w1_k, cmp_w2_k, cmp_pe_v, cmp_w1_v, cmp_w2_v,
           rel_bias, w_out, norm_ffn_g, w_up, conv_w, conv_b, w_down, norm_final_g):
    B, S, D = x.shape
    assert B == 1 and w_in.shape[0] == 1, "single sequence, single layer"
    h = x[0]
    hn = _rmsnorm(h, norm_mix_g[0], CDT)
    u_pool = _proj(hn, w_in[0][:, :POOL_WIDTH].astype(CDT), jnp.float32, name="proj_pool")
    y_pool = _pool(u_pool, pool_w[0], pool_scale[0])
    y_nsa = _nsa_branch(hn, w_in[0], cmp_w1_k[0], cmp_pe_k[0], cmp_w2_k[0],
                        cmp_w1_v[0], cmp_pe_v[0], cmp_w2_v[0], rel_bias)
    h = _outproj(y_nsa, y_pool, w_out[0].astype(CDT), h)
    hn = _rmsnorm(h, norm_ffn_g[0], CDT)
    act = _ffn_up(hn, w_up[0].astype(CDT), conv_w[0], conv_b[0].reshape(1, -1))
    out = _ffn_down(act, w_down[0].astype(CDT), h, norm_final_g)
    return out[None]
```

```python
import functools
import math

import numpy as np
import jax
import jax.numpy as jnp
from jax import lax
from jax.experimental import pallas as pl
from jax.experimental.pallas import tpu as pltpu

POOL_WINDOWS = (2, 4, 8, 16)
POOL_GROUP = 256
POOL_WIDTH = POOL_GROUP * len(POOL_WINDOWS)
HEAD_DIM = 128
KV_GROUPS = 4
HEADS_PER_GROUP = 6
NSA_HEADS = KV_GROUPS * HEADS_PER_GROUP
Q_WIDTH = NSA_HEADS * HEAD_DIM
KV_WIDTH = KV_GROUPS * HEAD_DIM
CMP_BLOCK = 32
CMP_STRIDE = 16
SLC_BLOCK = 64
N_SELECT = 16
N_LOCAL_FORCED = 2
WINDOW = 512
N_BRANCH = 3
REL_BUCKETS = 32
REL_MAX_DIST = 128
CONV_WIDTH = 3
RMS_EPS = 1e-6
NEG = -1e30
LOG2E = math.log2(math.e)

LANES = 128
SUBLANES = 8
VMEM_LIMIT_BYTES = 56 * 1024 * 1024

CDT = jnp.bfloat16
QB = 128
PAD_ROWS = WINDOW
FAR_CHUNK = 512
NEAR_KEYS = 2 * QB
WIN_KEYS = WINDOW + QB
CMP_PER_TILE = QB // CMP_STRIDE
BAND_BACK = 16
BAND_ROWS = 24
BAND_TABLE = BAND_ROWS + BAND_BACK


def _cparams(*sem, flags=None):
    return pltpu.CompilerParams(dimension_semantics=sem, vmem_limit_bytes=VMEM_LIMIT_BYTES, flags=flags)


def _dot(a, b):
    return jnp.dot(a, b, preferred_element_type=jnp.float32)


def _dot_nt(a, b):
    return lax.dot_general(a, b, (((1,), (1,)), ((), ())), preferred_element_type=jnp.float32)


def _rmsnorm_body(x_ref, g_ref, o_ref):
    x = x_ref[...]
    y = x * lax.rsqrt(jnp.mean(x * x, axis=-1, keepdims=True) + RMS_EPS)
    o_ref[...] = (y * g_ref[...]).astype(o_ref.dtype)


def _rmsnorm(x, g, out_dtype, bm=256):
    S, D = x.shape
    return pl.pallas_call(
        _rmsnorm_body,
        grid=(S // bm,),
        in_specs=[pl.BlockSpec((bm, D), lambda i: (i, 0)), pl.BlockSpec((1, D), lambda i: (0, 0))],
        out_specs=pl.BlockSpec((bm, D), lambda i: (i, 0)),
        out_shape=jax.ShapeDtypeStruct((S, D), out_dtype),
        compiler_params=_cparams("parallel"),
        name="rmsnorm",
    )(x, g.reshape(1, D))


def _proj_body(a_ref, w_ref, o_ref, *, pad_tiles, n_scaled, scale, sigmoid):
    i = pl.program_id(0)
    j = pl.program_id(1)

    @pl.when(i < pad_tiles)
    def _():
        o_ref[...] = jnp.zeros_like(o_ref)

    @pl.when(i >= pad_tiles)
    def _():
        r = _dot(a_ref[...], w_ref[...])
        if n_scaled:
            r = r * jnp.where(j < n_scaled, jnp.float32(scale), jnp.float32(1.0))
        if sigmoid:
            r = jax.nn.sigmoid(r)
        o_ref[...] = r.astype(o_ref.dtype)


def _proj(a, w, out_dtype, *, col0=0, n_cols=None, pad_tiles=0, scaled_cols=0, scale=1.0, sigmoid=False,
          bm=512, bn=1024, name="proj"):
    S, D = a.shape
    N = w.shape[1] if n_cols is None else n_cols
    bn = min(bn, N)
    assert scaled_cols % bn == 0 and col0 % bn == 0 and N % bn == 0
    n_scaled = scaled_cols // bn
    first = col0 // bn
    body = functools.partial(_proj_body, pad_tiles=pad_tiles, n_scaled=n_scaled, scale=scale, sigmoid=sigmoid)
    return pl.pallas_call(
        body,
        grid=(S // bm + pad_tiles, N // bn),
        in_specs=[pl.BlockSpec((bm, D), lambda i, j: (jnp.maximum(i - pad_tiles, 0), 0)),
                  pl.BlockSpec((D, bn), lambda i, j: (0, first + j))],
        out_specs=pl.BlockSpec((bm, bn), lambda i, j: (i, j)),
        out_shape=jax.ShapeDtypeStruct((S + pad_tiles * bm, N), out_dtype),
        compiler_params=_cparams("parallel", "arbitrary"),
        name=name,
    )(a, w)


POOL_HALO = 16


def _pool_body(u_ref, halo_ref, w_ref, s_ref, o_ref):
    i = pl.program_id(0)
    bm = u_ref.shape[0]
    u = u_ref[...]
    halo = jnp.where(i > 0, halo_ref[...], 0.0)
    ext = jnp.concatenate([halo, u], axis=0)
    t = i * bm + lax.broadcasted_iota(jnp.int32, (bm, 1), 0)
    acc = ext
    sums = {}
    shift = 1
    while shift < POOL_WINDOWS[-1]:
        acc = acc + pltpu.roll(acc, shift, axis=0)
        shift *= 2
        sums[shift] = acc
    for gi, w in enumerate(POOL_WINDOWS):
        cols = slice(gi * POOL_GROUP, (gi + 1) * POOL_GROUP)
        cnt = jnp.minimum(t + 1, w).astype(jnp.float32)
        d = sums[w][POOL_HALO:, cols] / cnt - u[:, cols]
        y = _dot(d.astype(CDT), w_ref[gi])
        o_ref[:, cols] = (y * s_ref[:, cols]).astype(o_ref.dtype)


def _pool(u, pool_w, pool_scale, bm=512):
    S = u.shape[0]
    return pl.pallas_call(
        _pool_body,
        grid=(S // bm,),
        in_specs=[pl.BlockSpec((bm, POOL_WIDTH), lambda i: (i, 0)),
                  pl.BlockSpec((POOL_HALO, POOL_WIDTH), lambda i: (jnp.maximum(i * (bm // POOL_HALO) - 1, 0), 0)),
                  pl.BlockSpec((len(POOL_WINDOWS), POOL_GROUP, POOL_GROUP), lambda i: (0, 0, 0)),
                  pl.BlockSpec((1, POOL_WIDTH), lambda i: (0, 0))],
        out_specs=pl.BlockSpec((bm, POOL_WIDTH), lambda i: (i, 0)),
        out_shape=jax.ShapeDtypeStruct((S, POOL_WIDTH), CDT),
        compiler_params=_cparams("parallel"),
        name="pool",
    )(u, u, pool_w.astype(CDT), pool_scale.reshape(1, POOL_WIDTH))


def _compress_body(x_ref, w1_ref, pe_ref, w2_ref, o_ref, acc_ref, pe_acc_ref, *, pad_chunks):
    l = pl.program_id(2)

    @pl.when(l == 0)
    def _():
        acc_ref[...] = jnp.zeros_like(acc_ref)
        pe_acc_ref[...] = jnp.zeros_like(pe_acc_ref)

    w_lo = w1_ref[0, 0]
    w_hi = w1_ref[1, 0]
    x = x_ref[...]
    acc_ref[:, :HEAD_DIM] += _dot(x, w_lo)
    acc_ref[:, HEAD_DIM:] += _dot(x, w_hi)
    pe_lo = jnp.broadcast_to(pe_ref[0, 0], (SUBLANES, HEAD_DIM))
    pe_hi = jnp.broadcast_to(pe_ref[1, 0], (SUBLANES, HEAD_DIM))
    pe_acc_ref[...] += _dot(pe_lo, w_lo) + _dot(pe_hi, w_hi)

    @pl.when(l == CMP_STRIDE - 1)
    def _():
        n = o_ref.shape[0]
        a = acc_ref[pad_chunks:pad_chunks + n, :HEAD_DIM]
        b = acc_ref[:, HEAD_DIM:]
        b_next = pltpu.roll(b, b.shape[0] - 1, axis=0)[pad_chunks:pad_chunks + n]
        pre = a + b_next + pe_acc_ref[0:1, :]
        h = pre * jax.nn.sigmoid(pre)
        o_ref[...] = _dot(h.astype(CDT), w2_ref[0]).astype(o_ref.dtype)


def _compress(qkv, w1, pe, w2):
    rows = qkv.shape[0]
    chunks = rows // CMP_STRIDE
    pad_chunks = PAD_ROWS // CMP_STRIDE
    n_out = chunks - pad_chunks
    width = 2 * KV_GROUPS
    x = qkv[:, Q_WIDTH:Q_WIDTH + 2 * KV_WIDTH].reshape(chunks, CMP_STRIDE * 2 * KV_WIDTH)
    body = functools.partial(_compress_body, pad_chunks=pad_chunks)
    return pl.pallas_call(
        body,
        grid=(2, KV_GROUPS, CMP_STRIDE),
        in_specs=[pl.BlockSpec((chunks, LANES), lambda kv, g, l: (0, l * width + kv * KV_GROUPS + g)),
                  pl.BlockSpec((2, 1, HEAD_DIM, HEAD_DIM), lambda kv, g, l: (kv, l, 0, 0)),
                  pl.BlockSpec((2, 1, 1, HEAD_DIM), lambda kv, g, l: (kv, l, 0, 0)),
                  pl.BlockSpec((1, HEAD_DIM, HEAD_DIM), lambda kv, g, l: (kv, 0, 0))],
        out_specs=pl.BlockSpec((None, None, n_out, HEAD_DIM), lambda kv, g, l: (kv, g, 0, 0)),
        out_shape=jax.ShapeDtypeStruct((2, KV_GROUPS, n_out, HEAD_DIM), CDT),
        scratch_shapes=[pltpu.VMEM((chunks, 2 * HEAD_DIM), jnp.float32),
                        pltpu.VMEM((SUBLANES, HEAD_DIM), jnp.float32)],
        compiler_params=_cparams("parallel", "parallel", "arbitrary"),
        name="compress",
    )(x, w1, pe, w2)


def _rel_bucket_np(dist):
    n = np.maximum(dist, 0)
    max_exact = REL_BUCKETS // 2
    nf = np.maximum(n, 1).astype(np.float32)
    large = max_exact + (np.log(nf / max_exact) / math.log(REL_MAX_DIST / max_exact)
                         * (REL_BUCKETS - max_exact)).astype(np.int32)
    large = np.minimum(large, REL_BUCKETS - 1)
    return np.where(n < max_exact, n, large).astype(np.int32)


def _bias_index_tiles():
    q = np.arange(QB)[:, None]
    dist = QB + q - np.arange(NEAR_KEYS)[None, :]
    near = np.where(dist >= 0, _rel_bucket_np(dist), -1)
    dist = WINDOW + q - np.arange(WIN_KEYS)[None, :]
    win = np.where((dist >= 0) & (dist < WINDOW), _rel_bucket_np(dist), -1)
    w = np.arange(LANES)[None, :]
    dist = q - CMP_STRIDE * (w - BAND_BACK) - (CMP_BLOCK - 1)
    band = np.where(dist >= 0, _rel_bucket_np(dist), -1)
    return np.concatenate([near, win, band], axis=1).astype(np.int32)


def _bias_body(tbl_ref, idx_ref, o_ref):
    h = pl.program_id(0)
    idx = idx_ref[...]
    far = tbl_ref[REL_BUCKETS - 1, h]
    val = jnp.full(idx.shape, NEG, jnp.float32)
    for b in range(REL_BUCKETS):
        val = jnp.where(idx == b, (tbl_ref[b, h] - far) * LOG2E, val)
    o_ref[...] = val


def _bias_tiles(rel_bias):
    idx = jnp.asarray(_bias_index_tiles())
    width = idx.shape[1]
    return pl.pallas_call(
        _bias_body,
        grid=(NSA_HEADS,),
        in_specs=[pl.BlockSpec(memory_space=pltpu.SMEM),
                  pl.BlockSpec((QB, width), lambda h: (0, 0))],
        out_specs=pl.BlockSpec((None, QB, width), lambda h: (h, 0, 0)),
        out_shape=jax.ShapeDtypeStruct((NSA_HEADS, QB, width), jnp.float32),
        compiler_params=_cparams("arbitrary"),
        name="bias_tiles",
    )(rel_bias, idx)


def _softmax_cols(s):
    m = jnp.max(s, axis=0, keepdims=True)
    p = jnp.exp2(s - m)
    l = jnp.sum(p, axis=0, keepdims=True)
    return p, jnp.where(m > 0.5 * NEG, 1.0 / l, 0.0)


def _split3(x):
    hi = x.astype(CDT)
    r = x - hi.astype(jnp.float32)
    mid = r.astype(CDT)
    lo = (r - mid.astype(jnp.float32)).astype(CDT)
    return hi, mid, lo


def _select_blocks(slc_t, i):
    n_slc = slc_t.shape[0]
    t = i * QB + lax.broadcasted_iota(jnp.int32, (1, QB), 1)
    j_int = lax.broadcasted_iota(jnp.int32, (n_slc, QB), 0)
    j_idx = j_int.astype(jnp.float32)
    cur = t // SLC_BLOCK
    forced = (j_int == 0) | ((cur - j_int >= 0) & (cur - j_int < N_LOCAL_FORCED))
    score = jnp.where(forced, 1e9, jnp.where(j_int > cur, -1e9, slc_t))
    picked = jnp.zeros(score.shape, jnp.bool_)
    for _ in range(min(N_SELECT, n_slc)):
        m = jnp.max(score, axis=0, keepdims=True)
        first = jnp.min(jnp.where(score == m, j_idx, float(n_slc)), axis=0, keepdims=True)
        hit = j_idx == first
        picked = picked | hit
        score = jnp.where(hit, -3e38, score)
    return picked


def _attn_body(q_ref, gate_ref, kc_ref, vct_ref, ks_ref, vst_ref, kw_ref, vwt_ref, blk_ref, tab_ref, wov_ref,
               o_ref, kaug_ref, kwaug_ref, sc_ref, pc_ref, sw_ref, pw_ref, s_ref, p_ref, acc_ref):
    i = pl.program_id(1)
    H = HEADS_PER_GROUP
    n_cmp = kc_ref.shape[0]
    n_slc = wov_ref.shape[0]
    lanes = H * QB
    head = lambda x, h: x[:, h * QB:(h + 1) * QB]

    @pl.when(i == 0)
    def _():
        kaug_ref[:, :HEAD_DIM] = ks_ref[...]
        kaug_ref[:, HEAD_DIM:] = blk_ref[...]
        kwaug_ref[:, :HEAD_DIM] = kw_ref[...]
        row = lax.broadcasted_iota(jnp.int32, (kwaug_ref.shape[0], LANES), 0)
        kwaug_ref[:, HEAD_DIM:] = jnp.where(row < PAD_ROWS, 1.0, 0.0).astype(CDT)

    qb = q_ref[...]
    q_t = jnp.concatenate([qb[:, h * HEAD_DIM:(h + 1) * HEAD_DIM].astype(jnp.float32).T.astype(CDT)
                           for h in range(H)], axis=1)

    def augment(extra):
        return jnp.concatenate([q_t, extra.astype(CDT)], axis=0)

    def values_t(ref, chunk0, n):
        return jnp.concatenate([ref[chunk0 + c] for c in range(n)], axis=1)

    grp = lax.broadcasted_iota(jnp.int32, (LANES, lanes), 0)
    sc_ref[...] = _dot(kc_ref[...], augment(jnp.where(grp > i, NEG, 0.0)))
    band0 = jnp.maximum(CMP_PER_TILE * i - BAND_BACK, 0)
    tab0 = band0 - (CMP_PER_TILE * i - BAND_BACK)
    band_rows = pl.ds(pl.multiple_of(band0, SUBLANES), BAND_ROWS)
    sc_ref[band_rows, :] += tab_ref[pl.ds(pl.multiple_of(NEAR_KEYS + WIN_KEYS + tab0, SUBLANES), BAND_ROWS), :]
    p_c, scale_c = _softmax_cols(sc_ref[...])
    p_c = p_c * scale_c
    imp_t = functools.reduce(lambda a, b: a + b, [head(p_c, h) for h in range(H)])
    pc_ref[...] = p_c.astype(CDT)
    o_cmp_t = _dot(vct_ref[...], pc_ref[...])

    wov_t = wov_ref[...]
    slc_t = sum(_dot(wov_t, part) for part in _split3(imp_t))
    drop = jnp.where(_select_blocks(slc_t, i), 0.0, 1.0)
    j_row = lax.broadcasted_iota(jnp.int32, (n_slc, QB), 0)
    drop_far = jnp.where(j_row < (i - 1) * (QB // SLC_BLOCK), drop, 1.0)
    tile_heads = lambda d: jnp.concatenate([d] * H, axis=1)

    win0 = pl.multiple_of(i * QB, QB)
    first_row = lax.broadcasted_iota(jnp.int32, (LANES, lanes), 0) == 0
    sw_ref[...] = (_dot(kwaug_ref[pl.ds(win0, WIN_KEYS), :], augment(jnp.where(first_row, NEG, 0.0)))
                   + tab_ref[NEAR_KEYS:NEAR_KEYS + WIN_KEYS])
    p_w, scale_w = _softmax_cols(sw_ref[...])
    pw_ref[...] = p_w.astype(CDT)
    o_win_t = _dot(values_t(vwt_ref, win0 // LANES, WIN_KEYS // LANES), pw_ref[...]) * scale_w

    near0 = pl.multiple_of(PAD_ROWS + (i - 1) * QB, QB)
    blank = FAR_CHUNK - NEAR_KEYS
    s_ref[:blank] = jnp.full((blank, lanes), NEG, jnp.float32)
    s_ref[blank:] = _dot(kaug_ref[pl.ds(near0, NEAR_KEYS), :], augment(tile_heads(drop))) + tab_ref[:NEAR_KEYS]
    p_ref[...] = jnp.zeros(p_ref.shape, CDT)
    acc_ref[...] = jnp.zeros(acc_ref.shape, jnp.float32)
    q_far = augment(tile_heads(drop_far))

    n_far = (jnp.maximum(i - 1, 0) * QB + FAR_CHUNK - 1) // FAR_CHUNK
    far_lanes = FAR_CHUNK // LANES

    def far_step(it, carry):
        a_prev, m, l = carry
        v0 = jnp.where(it == 1, (near0 - blank) // LANES, PAD_ROWS // LANES + jnp.maximum(it - 2, 0) * far_lanes)
        acc_ref[...] = a_prev * acc_ref[...] + _dot(values_t(vst_ref, v0, far_lanes), p_ref[...])
        s_prev = s_ref[...]
        m_new = jnp.maximum(m, jnp.max(s_prev, axis=0, keepdims=True))
        a = jnp.exp2(m - m_new)
        p = jnp.exp2(s_prev - m_new)
        l = a * l + jnp.sum(p, axis=0, keepdims=True)
        p_ref[...] = p.astype(CDT)
        k0 = pl.multiple_of(jnp.where(it < n_far, PAD_ROWS + it * FAR_CHUNK, 0), FAR_CHUNK)
        s_ref[...] = _dot(kaug_ref[pl.ds(k0, FAR_CHUNK), :], q_far)
        return a, m_new, l

    zero = scale_w - scale_w
    _, _, l_s = lax.fori_loop(0, n_far + 2, far_step, (zero + 1.0, zero + 0.1 * NEG, zero))
    o_slc_t = acc_ref[...] * (1.0 / l_s)

    gates_t = gate_ref[...].T
    for h in range(H):
        g = lambda b: gates_t[h * N_BRANCH + b:h * N_BRANCH + b + 1, :]
        o_t = g(0) * head(o_cmp_t, h) + g(1) * head(o_slc_t, h) + g(2) * head(o_win_t, h)
        o_ref[:, h * HEAD_DIM:(h + 1) * HEAD_DIM] = o_t.T.astype(o_ref.dtype)


def _block_membership(rows, n_slc):
    key = np.arange(rows)[:, None] - PAD_ROWS
    j = np.arange(n_slc)[None, :]
    member = np.where(key >= 0, key // SLC_BLOCK == j, True)
    return jnp.asarray(np.where(member, NEG, 0.0), dtype=CDT)


def _overlap_weights(n_cmp, n_slc):
    r = SLC_BLOCK // CMP_STRIDE
    lead = -(-CMP_BLOCK // CMP_STRIDE) - 1
    w = np.zeros((n_cmp, n_slc), np.float32)
    for o in range(-lead, r):
        s0 = o * CMP_STRIDE
        ov = max(0, min(s0 + CMP_BLOCK, SLC_BLOCK) - max(s0, 0))
        for j in range(n_slc):
            n = r * j + o
            if ov > 0 and 0 <= n < n_cmp - 1:
                w[n, j] = ov / CMP_STRIDE
    return jnp.asarray(w.T, dtype=CDT)


def _tile_group_columns(n_cmp):
    n = np.arange(n_cmp)[:, None]
    return jnp.asarray(n // CMP_PER_TILE == np.arange(LANES)[None, :], dtype=CDT)


def _attention(qkv, gates, kvc, tabs):
    rows = qkv.shape[0]
    S = rows - PAD_ROWS
    n_cmp = kvc.shape[2]
    n_slc = S // SLC_BLOCK
    assert n_cmp // CMP_PER_TILE <= LANES
    first = Q_WIDTH // LANES
    G = KV_GROUPS
    lanes = HEADS_PER_GROUP * QB
    kv_spec = lambda which: pl.BlockSpec((rows, LANES), lambda g, i: (0, first + which * G + g))
    vt_spec = lambda which: pl.BlockSpec((None, None, rows // LANES, HEAD_DIM, LANES), lambda g, i: (which, g, 0, 0, 0))
    width = tabs.shape[2]
    tabs_t = tabs.reshape(G, HEADS_PER_GROUP, QB, width).transpose(0, 3, 1, 2).reshape(G, width, lanes)
    kc_aug = jnp.concatenate([kvc[0], jnp.broadcast_to(_tile_group_columns(n_cmp), (G, n_cmp, LANES))], axis=-1)
    vc_t = kvc[1].transpose(0, 2, 1)
    v_cols = jnp.stack([qkv[:, (first + w * G) * LANES:(first + (w + 1) * G) * LANES] for w in (3, 5)])
    v_t = v_cols.reshape(2, rows // LANES, LANES, G, HEAD_DIM).transpose(0, 3, 1, 4, 2)
    return pl.pallas_call(
        _attn_body,
        grid=(G, S // QB),
        in_specs=[pl.BlockSpec((QB, HEADS_PER_GROUP * HEAD_DIM), lambda g, i: (i + PAD_ROWS // QB, g)),
                  pl.BlockSpec((QB, LANES), lambda g, i: (i, g)),
                  pl.BlockSpec((None, n_cmp, HEAD_DIM + LANES), lambda g, i: (g, 0, 0)),
                  pl.BlockSpec((None, HEAD_DIM, n_cmp), lambda g, i: (g, 0, 0)),
                  kv_spec(2), vt_spec(0), kv_spec(4), vt_spec(1),
                  pl.BlockSpec((rows, n_slc), lambda g, i: (0, 0)),
                  pl.BlockSpec((None, width, lanes), lambda g, i: (g, 0, 0)),
                  pl.BlockSpec((n_slc, n_cmp), lambda g, i: (0, 0))],
        out_specs=pl.BlockSpec((QB, HEADS_PER_GROUP * HEAD_DIM), lambda g, i: (i, g)),
        out_shape=jax.ShapeDtypeStruct((S, Q_WIDTH), CDT),
        scratch_shapes=[pltpu.VMEM((rows, HEAD_DIM + n_slc), CDT),
                        pltpu.VMEM((rows, HEAD_DIM + LANES), CDT),
                        pltpu.VMEM((n_cmp, lanes), jnp.float32),
                        pltpu.VMEM((n_cmp, lanes), CDT),
                        pltpu.VMEM((WIN_KEYS, lanes), jnp.float32),
                        pltpu.VMEM((WIN_KEYS, lanes), CDT),
                        pltpu.VMEM((FAR_CHUNK, lanes), jnp.float32),
                        pltpu.VMEM((FAR_CHUNK, lanes), CDT),
                        pltpu.VMEM((HEAD_DIM, lanes), jnp.float32)],
        compiler_params=_cparams("parallel", "arbitrary"),
        name="nsa",
    )(qkv, gates, kc_aug, vc_t, qkv, v_t, qkv, v_t, _block_membership(rows, n_slc), tabs_t,
      _overlap_weights(n_cmp, n_slc))


def _outproj_body(a1_ref, a2_ref, w1_ref, w2_ref, x_ref, o_ref):
    o_ref[...] = x_ref[...] + _dot(a1_ref[...], w1_ref[...]) + _dot(a2_ref[...], w2_ref[...])


def _outproj(y_nsa, y_pool, w_out, x, bm=512, bn=1024):
    S, D = x.shape
    bn = min(bn, D)
    k1, k2 = y_nsa.shape[1], y_pool.shape[1]
    return pl.pallas_call(
        _outproj_body,
        grid=(S // bm, D // bn),
        in_specs=[pl.BlockSpec((bm, k1), lambda i, j: (i, 0)),
                  pl.BlockSpec((bm, k2), lambda i, j: (i, 0)),
                  pl.BlockSpec((k1, bn), lambda i, j: (0, j)),
                  pl.BlockSpec((k2, bn), lambda i, j: (k1 // k2, j)),
                  pl.BlockSpec((bm, bn), lambda i, j: (i, j))],
        out_specs=pl.BlockSpec((bm, bn), lambda i, j: (i, j)),
        out_shape=jax.ShapeDtypeStruct((S, D), jnp.float32),
        compiler_params=_cparams("parallel", "arbitrary"),
        name="outproj",
    )(y_nsa, y_pool, w_out, w_out, x)


CONV_HALO = 16
FFN_SUBTILES = 2


def _ffn_up_body(a_ref, halo_ref, wa_ref, wb_ref, cwa_ref, cwb_ref, cba_ref, cbb_ref, o_ref, lhs_ref):
    i = pl.program_id(0)

    @pl.when(pl.program_id(1) == 0)
    def _():
        lhs_ref[:CONV_HALO] = jnp.where(i > 0, halo_ref[...], jnp.zeros_like(halo_ref))
        lhs_ref[CONV_HALO:] = a_ref[...]

    def conv(lhs, w, cw_ref, cb_ref):
        u = _dot(lhs, w)
        c = cb_ref[...] + cw_ref[CONV_WIDTH - 1:CONV_WIDTH, :] * u[CONV_HALO:]
        for k in range(1, CONV_WIDTH):
            c = c + cw_ref[CONV_WIDTH - 1 - k:CONV_WIDTH - k, :] * pltpu.roll(u, k, axis=0)[CONV_HALO:]
        return c

    wa = wa_ref[...].astype(CDT)
    wb = wb_ref[...].astype(CDT)
    sub = o_ref.shape[0] // FFN_SUBTILES
    for s in range(FFN_SUBTILES):
        lhs = lhs_ref[s * sub:s * sub + CONV_HALO + sub]
        ca = conv(lhs, wa, cwa_ref, cba_ref)
        cb = conv(lhs, wb, cwb_ref, cbb_ref)
        o_ref[s * sub:(s + 1) * sub] = (ca * jax.nn.sigmoid(ca) * cb).astype(o_ref.dtype)


def _ffn_up(hn, w_up, conv_w, conv_b, bm=1024, bn=256):
    S, D = hn.shape
    F = w_up.shape[1] // 2
    nb = F // bn
    return pl.pallas_call(
        _ffn_up_body,
        grid=(S // bm, nb),
        in_specs=[pl.BlockSpec((bm, D), lambda i, j: (i, 0)),
                  pl.BlockSpec((CONV_HALO, D), lambda i, j: (jnp.maximum(i * (bm // CONV_HALO) - 1, 0), 0)),
                  pl.BlockSpec((D, bn), lambda i, j: (0, j)),
                  pl.BlockSpec((D, bn), lambda i, j: (0, nb + j)),
                  pl.BlockSpec((CONV_WIDTH, bn), lambda i, j: (0, j)),
                  pl.BlockSpec((CONV_WIDTH, bn), lambda i, j: (0, nb + j)),
                  pl.BlockSpec((1, bn), lambda i, j: (0, j)),
                  pl.BlockSpec((1, bn), lambda i, j: (0, nb + j))],
        out_specs=pl.BlockSpec((bm, bn), lambda i, j: (i, j)),
        out_shape=jax.ShapeDtypeStruct((S, F), CDT),
        scratch_shapes=[pltpu.VMEM((CONV_HALO + bm, D), CDT)],
        compiler_params=_cparams("parallel", "arbitrary"),
        name="ffn_up",
    )(hn, hn, w_up, w_up, conv_w, conv_w, conv_b, conv_b)


def _ffn_down_body(a_ref, w_ref, h_ref, g_ref, o_ref):
    k = pl.program_id(1)

    @pl.when(k == 0)
    def _():
        o_ref[...] = h_ref[...]

    o_ref[...] += _dot(a_ref[...], w_ref[...].astype(CDT))

    @pl.when(k == pl.num_programs(1) - 1)
    def _():
        x = o_ref[...]
        y = x * lax.rsqrt(jnp.mean(x * x, axis=-1, keepdims=True) + RMS_EPS)
        o_ref[...] = y * g_ref[...]


def _ffn_down(act, w_down, h, g, bm=512, bk=256):
    S, F = act.shape
    D = h.shape[1]
    return pl.pallas_call(
        _ffn_down_body,
        grid=(S // bm, F // bk),
        in_specs=[pl.BlockSpec((bm, bk), lambda i, k: (i, k)),
                  pl.BlockSpec((bk, D), lambda i, k: (k, 0)),
                  pl.BlockSpec((bm, D), lambda i, k: (i, 0)),
                  pl.BlockSpec((1, D), lambda i, k: (0, 0))],
        out_specs=pl.BlockSpec((bm, D), lambda i, k: (i, 0)),
        out_shape=jax.ShapeDtypeStruct((S, D), jnp.float32),
        compiler_params=_cparams("parallel", "arbitrary"),
        name="ffn_down",
    )(act, w_down, h, g.reshape(1, D))


def _gate_weight(w_gate):
    D = w_gate.shape[0]
    per = HEADS_PER_GROUP * N_BRANCH
    w = w_gate.reshape(D, KV_GROUPS, per)
    w = jnp.pad(w, ((0, 0), (0, 0), (0, LANES - per)))
    return w.reshape(D, KV_GROUPS * LANES)


def _nsa_branch(hn, w_in, cmp_w1_k, cmp_pe_k, cmp_w2_k, cmp_w1_v, cmp_pe_v, cmp_w2_v, rel_bias):
    qkv_cols = Q_WIDTH + 6 * KV_WIDTH
    w_gate = _gate_weight(w_in[:, POOL_WIDTH + qkv_cols:])
    bm = 512
    qkv = _proj(hn, w_in, CDT, col0=POOL_WIDTH, n_cols=qkv_cols, pad_tiles=PAD_ROWS // bm, scaled_cols=Q_WIDTH,
                scale=HEAD_DIM ** -0.5 * LOG2E, bm=bm, name="proj_qkv")
    gates = _proj(hn, w_gate, jnp.float32, sigmoid=True, bm=bm, name="proj_gate")
    half = lambda w: w.reshape(2, CMP_STRIDE, *w.shape[1:])
    w1 = jnp.stack([half(cmp_w1_k), half(cmp_w1_v)]).astype(CDT)
    pe = jnp.stack([half(cmp_pe_k), half(cmp_pe_v)]).astype(CDT)
    w1 = w1.reshape(2 * 2, CMP_STRIDE, HEAD_DIM, HEAD_DIM)
    pe = pe.reshape(2 * 2, CMP_STRIDE, 1, HEAD_DIM)
    w2 = jnp.stack([cmp_w2_k, cmp_w2_v]).astype(CDT)
    kvc = _compress(qkv, w1, pe, w2)
    tabs = _bias_tiles(rel_bias)
    return _attention(qkv, gates, kvc, tabs)


def kernel(x, norm_mix_g, w_in, pool_w, pool_scale, cmp_pe_k, cmp_w1_k, cmp_w2_k, cmp_pe_v, cmp_w1_v, cmp_w2_v,
           rel_bias, w_out, norm_ffn_g, w_up, conv_w, conv_b, w_down, norm_final_g):
    B, S, D = x.shape
    assert B == 1 and w_in.shape[0] == 1, "single sequence, single layer"
    h = x.reshape(S, D)
    hn = _rmsnorm(h, norm_mix_g[0], CDT)
    w_all = w_in[0].astype(CDT)
    u_pool = _proj(hn, w_all, jnp.float32, n_cols=POOL_WIDTH, name="proj_pool")
    y_pool = _pool(u_pool, pool_w[0], pool_scale[0])
    y_nsa = _nsa_branch(hn, w_all, cmp_w1_k[0], cmp_pe_k[0], cmp_w2_k[0],
                        cmp_w1_v[0], cmp_pe_v[0], cmp_w2_v[0], rel_bias)
    h = _outproj(y_nsa, y_pool, w_out[0].astype(CDT), h)
    hn = _rmsnorm(h, norm_ffn_g[0], CDT)
    act = _ffn_up(hn, w_up[0], conv_w[0], conv_b[0].reshape(1, -1))
    out = _ffn_down(act, w_down[0], h, norm_final_g)
    return out.reshape(B, S, D)
```

```python
import functools
import math

import numpy as np
import jax
import jax.numpy as jnp
from jax import lax
from jax.experimental import pallas as pl
from jax.experimental.pallas import tpu as pltpu

POOL_WINDOWS = (2, 4, 8, 16)
POOL_GROUP = 256
POOL_WIDTH = POOL_GROUP * len(POOL_WINDOWS)
HEAD_DIM = 128
KV_GROUPS = 4
HEADS_PER_GROUP = 6
NSA_HEADS = KV_GROUPS * HEADS_PER_GROUP
Q_WIDTH = NSA_HEADS * HEAD_DIM
KV_WIDTH = KV_GROUPS * HEAD_DIM
CMP_BLOCK = 32
CMP_STRIDE = 16
SLC_BLOCK = 64
N_SELECT = 16
N_LOCAL_FORCED = 2
WINDOW = 512
N_BRANCH = 3
REL_BUCKETS = 32
REL_MAX_DIST = 128
CONV_WIDTH = 3
RMS_EPS = 1e-6
NEG = -1e30
LOG2E = math.log2(math.e)

LANES = 128
SUBLANES = 8
VMEM_LIMIT_BYTES = 56 * 1024 * 1024

CDT = jnp.bfloat16
QB = 128
PAD_ROWS = WINDOW
FAR_CHUNK = 512
NEAR_KEYS = FAR_CHUNK
FAR_UNROLL = 2
WIN_KEYS = WINDOW + QB
CMP_PER_TILE = QB // CMP_STRIDE
BAND_BACK = 16
BAND_ROWS = 24
BAND_TABLE = BAND_ROWS + BAND_BACK


def _cparams(*sem, flags=None):
    return pltpu.CompilerParams(dimension_semantics=sem, vmem_limit_bytes=VMEM_LIMIT_BYTES, flags=flags)


def _dot(a, b):
    return jnp.dot(a, b, preferred_element_type=jnp.float32)


def _dot_nt(a, b):
    return lax.dot_general(a, b, (((1,), (1,)), ((), ())), preferred_element_type=jnp.float32)


def _rmsnorm_body(x_ref, g_ref, o_ref):
    x = x_ref[...]
    y = x * lax.rsqrt(jnp.mean(x * x, axis=-1, keepdims=True) + RMS_EPS)
    o_ref[...] = (y * g_ref[...]).astype(o_ref.dtype)


def _rmsnorm(x, g, out_dtype, bm=256):
    S, D = x.shape
    return pl.pallas_call(
        _rmsnorm_body,
        grid=(S // bm,),
        in_specs=[pl.BlockSpec((bm, D), lambda i: (i, 0)), pl.BlockSpec((1, D), lambda i: (0, 0))],
        out_specs=pl.BlockSpec((bm, D), lambda i: (i, 0)),
        out_shape=jax.ShapeDtypeStruct((S, D), out_dtype),
        compiler_params=_cparams("parallel"),
        name="rmsnorm",
    )(x, g.reshape(1, D))


def _proj_body(a_ref, w_ref, o_ref, *, pad_tiles, n_scaled, scale, sigmoid):
    i = pl.program_id(0)
    j = pl.program_id(1)

    @pl.when(i < pad_tiles)
    def _():
        o_ref[...] = jnp.zeros_like(o_ref)

    @pl.when(i >= pad_tiles)
    def _():
        r = _dot(a_ref[...], w_ref[...])
        if n_scaled:
            r = r * jnp.where(j < n_scaled, jnp.float32(scale), jnp.float32(1.0))
        if sigmoid:
            r = jax.nn.sigmoid(r)
        o_ref[...] = r.astype(o_ref.dtype)


def _proj(a, w, out_dtype, *, col0=0, n_cols=None, pad_tiles=0, scaled_cols=0, scale=1.0, sigmoid=False,
          bm=512, bn=1024, name="proj"):
    S, D = a.shape
    N = w.shape[1] if n_cols is None else n_cols
    bn = min(bn, N)
    assert scaled_cols % bn == 0 and col0 % bn == 0 and N % bn == 0
    n_scaled = scaled_cols // bn
    first = col0 // bn
    body = functools.partial(_proj_body, pad_tiles=pad_tiles, n_scaled=n_scaled, scale=scale, sigmoid=sigmoid)
    return pl.pallas_call(
        body,
        grid=(S // bm + pad_tiles, N // bn),
        in_specs=[pl.BlockSpec((bm, D), lambda i, j: (jnp.maximum(i - pad_tiles, 0), 0)),
                  pl.BlockSpec((D, bn), lambda i, j: (0, first + j))],
        out_specs=pl.BlockSpec((bm, bn), lambda i, j: (i, j)),
        out_shape=jax.ShapeDtypeStruct((S + pad_tiles * bm, N), out_dtype),
        compiler_params=_cparams("parallel", "arbitrary"),
        name=name,
    )(a, w)


POOL_HALO = 16


def _pool_body(u_ref, halo_ref, w_ref, s_ref, o_ref):
    i = pl.program_id(0)
    bm = u_ref.shape[0]
    u = u_ref[...]
    halo = jnp.where(i > 0, halo_ref[...], 0.0)
    ext = jnp.concatenate([halo, u], axis=0)
    t = i * bm + lax.broadcasted_iota(jnp.int32, (bm, 1), 0)
    acc = ext
    sums = {}
    shift = 1
    while shift < POOL_WINDOWS[-1]:
        acc = acc + pltpu.roll(acc, shift, axis=0)
        shift *= 2
        sums[shift] = acc
    for gi, w in enumerate(POOL_WINDOWS):
        cols = slice(gi * POOL_GROUP, (gi + 1) * POOL_GROUP)
        cnt = jnp.minimum(t + 1, w).astype(jnp.float32)
        d = sums[w][POOL_HALO:, cols] / cnt - u[:, cols]
        y = _dot(d.astype(CDT), w_ref[gi])
        o_ref[:, cols] = (y * s_ref[:, cols]).astype(o_ref.dtype)


def _pool(u, pool_w, pool_scale, bm=512):
    S = u.shape[0]
    return pl.pallas_call(
        _pool_body,
        grid=(S // bm,),
        in_specs=[pl.BlockSpec((bm, POOL_WIDTH), lambda i: (i, 0)),
                  pl.BlockSpec((POOL_HALO, POOL_WIDTH), lambda i: (jnp.maximum(i * (bm // POOL_HALO) - 1, 0), 0)),
                  pl.BlockSpec((len(POOL_WINDOWS), POOL_GROUP, POOL_GROUP), lambda i: (0, 0, 0)),
                  pl.BlockSpec((1, POOL_WIDTH), lambda i: (0, 0))],
        out_specs=pl.BlockSpec((bm, POOL_WIDTH), lambda i: (i, 0)),
        out_shape=jax.ShapeDtypeStruct((S, POOL_WIDTH), CDT),
        compiler_params=_cparams("parallel"),
        name="pool",
    )(u, u, pool_w.astype(CDT), pool_scale.reshape(1, POOL_WIDTH))


def _compress_body(x_ref, w1_ref, pe_ref, w2_ref, o_ref, acc_ref, pe_acc_ref, *, pad_chunks):
    l = pl.program_id(2)

    @pl.when(l == 0)
    def _():
        acc_ref[...] = jnp.zeros_like(acc_ref)
        pe_acc_ref[...] = jnp.zeros_like(pe_acc_ref)

    w_lo = w1_ref[0, 0]
    w_hi = w1_ref[1, 0]
    x = x_ref[...]
    acc_ref[:, :HEAD_DIM] += _dot(x, w_lo)
    acc_ref[:, HEAD_DIM:] += _dot(x, w_hi)
    pe_lo = jnp.broadcast_to(pe_ref[0, 0], (SUBLANES, HEAD_DIM))
    pe_hi = jnp.broadcast_to(pe_ref[1, 0], (SUBLANES, HEAD_DIM))
    pe_acc_ref[...] += _dot(pe_lo, w_lo) + _dot(pe_hi, w_hi)

    @pl.when(l == CMP_STRIDE - 1)
    def _():
        n = o_ref.shape[0]
        a = acc_ref[pad_chunks:pad_chunks + n, :HEAD_DIM]
        b = acc_ref[:, HEAD_DIM:]
        b_next = pltpu.roll(b, b.shape[0] - 1, axis=0)[pad_chunks:pad_chunks + n]
        pre = a + b_next + pe_acc_ref[0:1, :]
        h = pre * jax.nn.sigmoid(pre)
        o_ref[...] = _dot(h.astype(CDT), w2_ref[0]).astype(o_ref.dtype)


def _compress(qkv, w1, pe, w2):
    rows = qkv.shape[0]
    chunks = rows // CMP_STRIDE
    pad_chunks = PAD_ROWS // CMP_STRIDE
    n_out = chunks - pad_chunks
    width = 2 * KV_GROUPS
    x = qkv[:, Q_WIDTH:Q_WIDTH + 2 * KV_WIDTH].reshape(chunks, CMP_STRIDE * 2 * KV_WIDTH)
    body = functools.partial(_compress_body, pad_chunks=pad_chunks)
    return pl.pallas_call(
        body,
        grid=(2, KV_GROUPS, CMP_STRIDE),
        in_specs=[pl.BlockSpec((chunks, LANES), lambda kv, g, l: (0, l * width + kv * KV_GROUPS + g)),
                  pl.BlockSpec((2, 1, HEAD_DIM, HEAD_DIM), lambda kv, g, l: (kv, l, 0, 0)),
                  pl.BlockSpec((2, 1, 1, HEAD_DIM), lambda kv, g, l: (kv, l, 0, 0)),
                  pl.BlockSpec((1, HEAD_DIM, HEAD_DIM), lambda kv, g, l: (kv, 0, 0))],
        out_specs=pl.BlockSpec((None, None, n_out, HEAD_DIM), lambda kv, g, l: (kv, g, 0, 0)),
        out_shape=jax.ShapeDtypeStruct((2, KV_GROUPS, n_out, HEAD_DIM), CDT),
        scratch_shapes=[pltpu.VMEM((chunks, 2 * HEAD_DIM), jnp.float32),
                        pltpu.VMEM((SUBLANES, HEAD_DIM), jnp.float32)],
        compiler_params=_cparams("parallel", "parallel", "arbitrary"),
        name="compress",
    )(x, w1, pe, w2)


def _rel_bucket_np(dist):
    n = np.maximum(dist, 0)
    max_exact = REL_BUCKETS // 2
    nf = np.maximum(n, 1).astype(np.float32)
    large = max_exact + (np.log(nf / max_exact) / math.log(REL_MAX_DIST / max_exact)
                         * (REL_BUCKETS - max_exact)).astype(np.int32)
    large = np.minimum(large, REL_BUCKETS - 1)
    return np.where(n < max_exact, n, large).astype(np.int32)


def _bias_index_tiles():
    q = np.arange(QB)[:, None]
    dist = NEAR_KEYS - QB + q - np.arange(NEAR_KEYS)[None, :]
    near = np.where(dist >= 0, _rel_bucket_np(dist), -1)
    dist = WINDOW + q - np.arange(WIN_KEYS)[None, :]
    win = np.where((dist >= 0) & (dist < WINDOW), _rel_bucket_np(dist), -1)
    w = np.arange(LANES)[None, :]
    dist = q - CMP_STRIDE * (w - BAND_BACK) - (CMP_BLOCK - 1)
    band = np.where(dist >= 0, _rel_bucket_np(dist), -1)
    return np.concatenate([near, win, band], axis=1).astype(np.int32)


def _bias_body(tbl_ref, idx_ref, o_ref):
    h = pl.program_id(0)
    idx = idx_ref[...]
    far = tbl_ref[REL_BUCKETS - 1, h]
    val = jnp.full(idx.shape, NEG, jnp.float32)
    for b in range(REL_BUCKETS):
        val = jnp.where(idx == b, (tbl_ref[b, h] - far) * LOG2E, val)
    o_ref[...] = val


def _bias_tiles(rel_bias):
    idx = jnp.asarray(_bias_index_tiles())
    width = idx.shape[1]
    return pl.pallas_call(
        _bias_body,
        grid=(NSA_HEADS,),
        in_specs=[pl.BlockSpec(memory_space=pltpu.SMEM),
                  pl.BlockSpec((QB, width), lambda h: (0, 0))],
        out_specs=pl.BlockSpec((None, QB, width), lambda h: (h, 0, 0)),
        out_shape=jax.ShapeDtypeStruct((NSA_HEADS, QB, width), jnp.float32),
        compiler_params=_cparams("arbitrary"),
        name="bias_tiles",
    )(rel_bias, idx)


def _softmax_cols(s):
    m = jnp.max(s, axis=0, keepdims=True)
    p = jnp.exp2(s - m)
    l = jnp.sum(p, axis=0, keepdims=True)
    return p, jnp.where(m > 0.5 * NEG, 1.0 / l, 0.0)


def _split3(x):
    hi = x.astype(CDT)
    r = x - hi.astype(jnp.float32)
    mid = r.astype(CDT)
    lo = (r - mid.astype(jnp.float32)).astype(CDT)
    return hi, mid, lo


def _select_blocks(slc_t, i):
    n_slc = slc_t.shape[0]
    t = i * QB + lax.broadcasted_iota(jnp.int32, (1, QB), 1)
    j_int = lax.broadcasted_iota(jnp.int32, (n_slc, QB), 0)
    j_idx = j_int.astype(jnp.float32)
    cur = t // SLC_BLOCK
    forced = (j_int == 0) | ((cur - j_int >= 0) & (cur - j_int < N_LOCAL_FORCED))
    score = jnp.where(forced, 1e9, jnp.where(j_int > cur, -1e9, slc_t))
    picked = jnp.zeros(score.shape, jnp.bool_)
    for _ in range(min(N_SELECT, n_slc)):
        m = jnp.max(score, axis=0, keepdims=True)
        first = jnp.min(jnp.where(score == m, j_idx, float(n_slc)), axis=0, keepdims=True)
        hit = j_idx == first
        picked = picked | hit
        score = jnp.where(hit, -3e38, score)
    return picked


def _attn_body(q_ref, gate_ref, kc_ref, vct_ref, ks_ref, vst_ref, kw_ref, vwt_ref, blk_ref, tab_ref, wov_ref,
               o_ref, kaug_ref, kwaug_ref, sc_ref, pc_ref, sw_ref, pw_ref, s_ref, p_ref, acc_ref):
    i = pl.program_id(1)
    H = HEADS_PER_GROUP
    n_cmp = kc_ref.shape[0]
    n_slc = wov_ref.shape[0]
    lanes = H * QB
    head = lambda x, h: x[:, h * QB:(h + 1) * QB]

    @pl.when(i == 0)
    def _():
        kaug_ref[:, :HEAD_DIM] = ks_ref[...]
        kaug_ref[:, HEAD_DIM:] = blk_ref[...]
        kwaug_ref[:, :HEAD_DIM] = kw_ref[...]
        row = lax.broadcasted_iota(jnp.int32, (kwaug_ref.shape[0], LANES), 0)
        kwaug_ref[:, HEAD_DIM:] = jnp.where(row < PAD_ROWS, 1.0, 0.0).astype(CDT)

    qb = q_ref[...]
    q_t = jnp.concatenate([qb[:, h * HEAD_DIM:(h + 1) * HEAD_DIM].astype(jnp.float32).T.astype(CDT)
                           for h in range(H)], axis=1)

    def augment(extra):
        return jnp.concatenate([q_t, extra.astype(CDT)], axis=0)

    def values_t(ref, chunk0, n):
        return jnp.concatenate([ref[chunk0 + c] for c in range(n)], axis=1)

    grp = lax.broadcasted_iota(jnp.int32, (LANES, lanes), 0)
    sc_ref[...] = _dot(kc_ref[...], augment(jnp.where(grp > i, NEG, 0.0)))
    band0 = jnp.maximum(CMP_PER_TILE * i - BAND_BACK, 0)
    tab0 = band0 - (CMP_PER_TILE * i - BAND_BACK)
    band_rows = pl.ds(pl.multiple_of(band0, SUBLANES), BAND_ROWS)
    sc_ref[band_rows, :] += tab_ref[pl.ds(pl.multiple_of(NEAR_KEYS + WIN_KEYS + tab0, SUBLANES), BAND_ROWS), :]
    s_c = sc_ref[...]
    m_c = jnp.max(s_c, axis=0, keepdims=True)
    p_c = jnp.exp2(s_c - m_c)
    l_c = jnp.sum(p_c, axis=0, keepdims=True)
    pc_ref[...] = p_c.astype(CDT)
    seen = m_c > 0.5 * NEG
    o_cmp_t = _dot(vct_ref[...], pc_ref[...]) * jnp.where(seen, 1.0 / l_c, 0.0)
    shift_c = jnp.where(seen, m_c + jnp.log2(l_c), -NEG)
    imp_t = functools.reduce(lambda a, b: a + b,
                             [jnp.exp2(sc_ref[:, h * QB:(h + 1) * QB] - head(shift_c, h)) for h in range(H)])

    wov_t = wov_ref[...]
    slc_t = sum(_dot(wov_t, part) for part in _split3(imp_t))
    drop = jnp.where(_select_blocks(slc_t, i), 0.0, 1.0)
    j_row = lax.broadcasted_iota(jnp.int32, (n_slc, QB), 0)
    near_tile = i + 1 - NEAR_KEYS // QB
    drop_far = jnp.where(j_row < near_tile * (QB // SLC_BLOCK), drop, 1.0)
    tile_heads = lambda d: jnp.concatenate([d] * H, axis=1)

    win0 = pl.multiple_of(i * QB, QB)
    first_row = lax.broadcasted_iota(jnp.int32, (LANES, lanes), 0) == 0
    sw_ref[...] = (_dot(kwaug_ref[pl.ds(win0, WIN_KEYS), :], augment(jnp.where(first_row, NEG, 0.0)))
                   + tab_ref[NEAR_KEYS:NEAR_KEYS + WIN_KEYS])
    p_w, scale_w = _softmax_cols(sw_ref[...])
    pw_ref[...] = p_w.astype(CDT)
    o_win_t = _dot(values_t(vwt_ref, win0 // LANES, WIN_KEYS // LANES), pw_ref[...]) * scale_w

    near0 = pl.multiple_of(PAD_ROWS + near_tile * QB, QB)
    s_ref[...] = _dot(kaug_ref[pl.ds(near0, NEAR_KEYS), :], augment(tile_heads(drop))) + tab_ref[:NEAR_KEYS]
    p_ref[...] = jnp.zeros(p_ref.shape, CDT)
    acc_ref[...] = jnp.zeros(acc_ref.shape, jnp.float32)
    q_far = augment(tile_heads(drop_far))

    n_chunks = (kaug_ref.shape[0] - PAD_ROWS) // FAR_CHUNK
    n_far = (jnp.maximum(near_tile, 0) * QB + FAR_CHUNK - 1) // FAR_CHUNK
    far_lanes = FAR_CHUNK // LANES

    def far_step(it, carry):
        a_prev, m, l = carry
        v0 = jnp.where(it == 1, near0 // LANES, PAD_ROWS // LANES + jnp.clip(it - 2, 0, n_chunks - 1) * far_lanes)
        acc_ref[...] = a_prev * acc_ref[...] + _dot(values_t(vst_ref, v0, far_lanes), p_ref[...])
        s_prev = s_ref[...]
        m_new = jnp.maximum(m, jnp.max(s_prev, axis=0, keepdims=True))
        a = jnp.exp2(m - m_new)
        p = jnp.exp2(s_prev - m_new)
        l = a * l + jnp.sum(p, axis=0, keepdims=True)
        p_ref[...] = p.astype(CDT)
        k0 = pl.multiple_of(jnp.where(it < n_far, PAD_ROWS + it * FAR_CHUNK, 0), FAR_CHUNK)
        s_ref[...] = _dot(kaug_ref[pl.ds(k0, FAR_CHUNK), :], q_far)
        return a, m_new, l

    def far_trip(k, carry):
        for u in range(FAR_UNROLL):
            carry = far_step(FAR_UNROLL * k + u, carry)
        return carry

    zero = scale_w - scale_w
    trips = (n_far + 2 + FAR_UNROLL - 1) // FAR_UNROLL
    _, _, l_s = lax.fori_loop(0, trips, far_trip, (zero + 1.0, zero + 0.1 * NEG, zero))
    o_slc_t = acc_ref[...] * (1.0 / l_s)

    gates_t = gate_ref[...].T
    for h in range(H):
        g = lambda b: gates_t[h * N_BRANCH + b:h * N_BRANCH + b + 1, :]
        o_t = g(0) * head(o_cmp_t, h) + g(1) * head(o_slc_t, h) + g(2) * head(o_win_t, h)
        o_ref[:, h * HEAD_DIM:(h + 1) * HEAD_DIM] = o_t.T.astype(o_ref.dtype)


def _block_membership(rows, n_slc):
    key = np.arange(rows)[:, None] - PAD_ROWS
    j = np.arange(n_slc)[None, :]
    member = np.where(key >= 0, key // SLC_BLOCK == j, True)
    return jnp.asarray(np.where(member, NEG, 0.0), dtype=CDT)


def _overlap_weights(n_cmp, n_slc):
    r = SLC_BLOCK // CMP_STRIDE
    lead = -(-CMP_BLOCK // CMP_STRIDE) - 1
    w = np.zeros((n_cmp, n_slc), np.float32)
    for o in range(-lead, r):
        s0 = o * CMP_STRIDE
        ov = max(0, min(s0 + CMP_BLOCK, SLC_BLOCK) - max(s0, 0))
        for j in range(n_slc):
            n = r * j + o
            if ov > 0 and 0 <= n < n_cmp - 1:
                w[n, j] = ov / CMP_STRIDE
    return jnp.asarray(w.T, dtype=CDT)


def _tile_group_columns(n_cmp):
    n = np.arange(n_cmp)[:, None]
    return jnp.asarray(n // CMP_PER_TILE == np.arange(LANES)[None, :], dtype=CDT)


def _attention(qkv, gates, kvc, tabs):
    rows = qkv.shape[0]
    S = rows - PAD_ROWS
    n_cmp = kvc.shape[2]
    n_slc = S // SLC_BLOCK
    assert n_cmp // CMP_PER_TILE <= LANES
    first = Q_WIDTH // LANES
    G = KV_GROUPS
    lanes = HEADS_PER_GROUP * QB
    kv_spec = lambda which: pl.BlockSpec((rows, LANES), lambda g, i: (0, first + which * G + g))
    vt_spec = lambda which: pl.BlockSpec((None, None, rows // LANES, HEAD_DIM, LANES), lambda g, i: (which, g, 0, 0, 0))
    width = tabs.shape[2]
    tabs_t = tabs.reshape(G, HEADS_PER_GROUP, QB, width).transpose(0, 3, 1, 2).reshape(G, width, lanes)
    kc_aug = jnp.concatenate([kvc[0], jnp.broadcast_to(_tile_group_columns(n_cmp), (G, n_cmp, LANES))], axis=-1)
    vc_t = kvc[1].transpose(0, 2, 1)
    v_cols = jnp.stack([qkv[:, (first + w * G) * LANES:(first + (w + 1) * G) * LANES] for w in (3, 5)])
    v_t = v_cols.reshape(2, rows // LANES, LANES, G, HEAD_DIM).transpose(0, 3, 1, 4, 2)
    return pl.pallas_call(
        _attn_body,
        grid=(G, S // QB),
        in_specs=[pl.BlockSpec((QB, HEADS_PER_GROUP * HEAD_DIM), lambda g, i: (i + PAD_ROWS // QB, g)),
                  pl.BlockSpec((QB, LANES), lambda g, i: (i, g)),
                  pl.BlockSpec((None, n_cmp, HEAD_DIM + LANES), lambda g, i: (g, 0, 0)),
                  pl.BlockSpec((None, HEAD_DIM, n_cmp), lambda g, i: (g, 0, 0)),
                  kv_spec(2), vt_spec(0), kv_spec(4), vt_spec(1),
                  pl.BlockSpec((rows, n_slc), lambda g, i: (0, 0)),
                  pl.BlockSpec((None, width, lanes), lambda g, i: (g, 0, 0)),
                  pl.BlockSpec((n_slc, n_cmp), lambda g, i: (0, 0))],
        out_specs=pl.BlockSpec((QB, HEADS_PER_GROUP * HEAD_DIM), lambda g, i: (i, g)),
        out_shape=jax.ShapeDtypeStruct((S, Q_WIDTH), CDT),
        scratch_shapes=[pltpu.VMEM((rows, HEAD_DIM + n_slc), CDT),
                        pltpu.VMEM((rows, HEAD_DIM + LANES), CDT),
                        pltpu.VMEM((n_cmp, lanes), jnp.float32),
                        pltpu.VMEM((n_cmp, lanes), CDT),
                        pltpu.VMEM((WIN_KEYS, lanes), jnp.float32),
                        pltpu.VMEM((WIN_KEYS, lanes), CDT),
                        pltpu.VMEM((FAR_CHUNK, lanes), jnp.float32),
                        pltpu.VMEM((FAR_CHUNK, lanes), CDT),
                        pltpu.VMEM((HEAD_DIM, lanes), jnp.float32)],
        compiler_params=_cparams("parallel", "arbitrary"),
        name="nsa",
    )(qkv, gates, kc_aug, vc_t, qkv, v_t, qkv, v_t, _block_membership(rows, n_slc), tabs_t,
      _overlap_weights(n_cmp, n_slc))


def _outproj_body(a1_ref, a2_ref, w1_ref, w2_ref, x_ref, o_ref):
    o_ref[...] = x_ref[...] + _dot(a1_ref[...], w1_ref[...]) + _dot(a2_ref[...], w2_ref[...])


def _outproj(y_nsa, y_pool, w_out, x, bm=512, bn=1024):
    S, D = x.shape
    bn = min(bn, D)
    k1, k2 = y_nsa.shape[1], y_pool.shape[1]
    return pl.pallas_call(
        _outproj_body,
        grid=(S // bm, D // bn),
        in_specs=[pl.BlockSpec((bm, k1), lambda i, j: (i, 0)),
                  pl.BlockSpec((bm, k2), lambda i, j: (i, 0)),
                  pl.BlockSpec((k1, bn), lambda i, j: (0, j)),
                  pl.BlockSpec((k2, bn), lambda i, j: (k1 // k2, j)),
                  pl.BlockSpec((bm, bn), lambda i, j: (i, j))],
        out_specs=pl.BlockSpec((bm, bn), lambda i, j: (i, j)),
        out_shape=jax.ShapeDtypeStruct((S, D), jnp.float32),
        compiler_params=_cparams("parallel", "arbitrary"),
        name="outproj",
    )(y_nsa, y_pool, w_out, w_out, x)


CONV_HALO = 16
FFN_SUBTILES = 2


def _ffn_up_body(a_ref, halo_ref, wa_ref, wb_ref, cwa_ref, cwb_ref, cba_ref, cbb_ref, o_ref, lhs_ref):
    i = pl.program_id(0)

    @pl.when(pl.program_id(1) == 0)
    def _():
        lhs_ref[:CONV_HALO] = jnp.where(i > 0, halo_ref[...], jnp.zeros_like(halo_ref))
        lhs_ref[CONV_HALO:] = a_ref[...]

    def conv(lhs, w, cw_ref, cb_ref):
        u = _dot(lhs, w)
        c = cb_ref[...] + cw_ref[CONV_WIDTH - 1:CONV_WIDTH, :] * u[CONV_HALO:]
        for k in range(1, CONV_WIDTH):
            c = c + cw_ref[CONV_WIDTH - 1 - k:CONV_WIDTH - k, :] * pltpu.roll(u, k, axis=0)[CONV_HALO:]
        return c

    wa = wa_ref[...].astype(CDT)
    wb = wb_ref[...].astype(CDT)
    sub = o_ref.shape[0] // FFN_SUBTILES
    for s in range(FFN_SUBTILES):
        lhs = lhs_ref[s * sub:s * sub + CONV_HALO + sub]
        ca = conv(lhs, wa, cwa_ref, cba_ref)
        cb = conv(lhs, wb, cwb_ref, cbb_ref)
        o_ref[s * sub:(s + 1) * sub] = (ca * jax.nn.sigmoid(ca) * cb).astype(o_ref.dtype)


def _ffn_up(hn, w_up, conv_w, conv_b, bm=1024, bn=256):
    S, D = hn.shape
    F = w_up.shape[1] // 2
    nb = F // bn
    return pl.pallas_call(
        _ffn_up_body,
        grid=(S // bm, nb),
        in_specs=[pl.BlockSpec((bm, D), lambda i, j: (i, 0)),
                  pl.BlockSpec((CONV_HALO, D), lambda i, j: (jnp.maximum(i * (bm // CONV_HALO) - 1, 0), 0)),
                  pl.BlockSpec((D, bn), lambda i, j: (0, j)),
                  pl.BlockSpec((D, bn), lambda i, j: (0, nb + j)),
                  pl.BlockSpec((CONV_WIDTH, bn), lambda i, j: (0, j)),
                  pl.BlockSpec((CONV_WIDTH, bn), lambda i, j: (0, nb + j)),
                  pl.BlockSpec((1, bn), lambda i, j: (0, j)),
                  pl.BlockSpec((1, bn), lambda i, j: (0, nb + j))],
        out_specs=pl.BlockSpec((bm, bn), lambda i, j: (i, j)),
        out_shape=jax.ShapeDtypeStruct((S, F), CDT),
        scratch_shapes=[pltpu.VMEM((CONV_HALO + bm, D), CDT)],
        compiler_params=_cparams("parallel", "arbitrary"),
        name="ffn_up",
    )(hn, hn, w_up, w_up, conv_w, conv_w, conv_b, conv_b)


def _ffn_down_body(a_ref, w_ref, h_ref, g_ref, o_ref):
    k = pl.program_id(1)

    @pl.when(k == 0)
    def _():
        o_ref[...] = h_ref[...]

    o_ref[...] += _dot(a_ref[...], w_ref[...])

    @pl.when(k == pl.num_programs(1) - 1)
    def _():
        x = o_ref[...]
        y = x * lax.rsqrt(jnp.mean(x * x, axis=-1, keepdims=True) + RMS_EPS)
        o_ref[...] = y * g_ref[...]


def _ffn_down(act, w_down, h, g, bm=512, bk=256):
    S, F = act.shape
    D = h.shape[1]
    return pl.pallas_call(
        _ffn_down_body,
        grid=(S // bm, F // bk),
        in_specs=[pl.BlockSpec((bm, bk), lambda i, k: (i, k)),
                  pl.BlockSpec((bk, D), lambda i, k: (k, 0)),
                  pl.BlockSpec((bm, D), lambda i, k: (i, 0)),
                  pl.BlockSpec((1, D), lambda i, k: (0, 0))],
        out_specs=pl.BlockSpec((bm, D), lambda i, k: (i, 0)),
        out_shape=jax.ShapeDtypeStruct((S, D), jnp.float32),
        compiler_params=_cparams("parallel", "arbitrary"),
        name="ffn_down",
    )(act, w_down, h, g.reshape(1, D))


def _gate_weight(w_gate):
    D = w_gate.shape[0]
    per = HEADS_PER_GROUP * N_BRANCH
    w = w_gate.reshape(D, KV_GROUPS, per)
    w = jnp.pad(w, ((0, 0), (0, 0), (0, LANES - per)))
    return w.reshape(D, KV_GROUPS * LANES)


def _nsa_branch(hn, w_in, w_gate, cmp_w1_k, cmp_pe_k, cmp_w2_k, cmp_w1_v, cmp_pe_v, cmp_w2_v, rel_bias):
    qkv_cols = Q_WIDTH + 6 * KV_WIDTH
    w_gate = _gate_weight(w_gate).astype(CDT)
    bm = 512
    qkv = _proj(hn, w_in, CDT, col0=POOL_WIDTH, n_cols=qkv_cols, pad_tiles=PAD_ROWS // bm, scaled_cols=Q_WIDTH,
                scale=HEAD_DIM ** -0.5 * LOG2E, bm=bm, name="proj_qkv")
    gates = _proj(hn, w_gate, jnp.float32, sigmoid=True, bm=bm, name="proj_gate")
    half = lambda w: w.reshape(2, CMP_STRIDE, *w.shape[1:])
    w1 = jnp.stack([half(cmp_w1_k), half(cmp_w1_v)]).astype(CDT)
    pe = jnp.stack([half(cmp_pe_k), half(cmp_pe_v)]).astype(CDT)
    w1 = w1.reshape(2 * 2, CMP_STRIDE, HEAD_DIM, HEAD_DIM)
    pe = pe.reshape(2 * 2, CMP_STRIDE, 1, HEAD_DIM)
    w2 = jnp.stack([cmp_w2_k, cmp_w2_v]).astype(CDT)
    kvc = _compress(qkv, w1, pe, w2)
    tabs = _bias_tiles(rel_bias)
    return _attention(qkv, gates, kvc, tabs)


def kernel(x, norm_mix_g, w_in, pool_w, pool_scale, cmp_pe_k, cmp_w1_k, cmp_w2_k, cmp_pe_v, cmp_w1_v, cmp_w2_v,
           rel_bias, w_out, norm_ffn_g, w_up, conv_w, conv_b, w_down, norm_final_g):
    B, S, D = x.shape
    assert B == 1 and w_in.shape[0] == 1, "single sequence, single layer"
    h = x.reshape(S, D)
    hn = _rmsnorm(h, norm_mix_g[0], CDT)
    n_main = POOL_WIDTH + Q_WIDTH + 6 * KV_WIDTH
    w_all = w_in[0][:, :n_main].astype(CDT)
    u_pool = _proj(hn, w_all, jnp.float32, n_cols=POOL_WIDTH, name="proj_pool")
    y_pool = _pool(u_pool, pool_w[0], pool_scale[0])
    y_nsa = _nsa_branch(hn, w_all, w_in[0][:, n_main:], cmp_w1_k[0], cmp_pe_k[0], cmp_w2_k[0],
                        cmp_w1_v[0], cmp_pe_v[0], cmp_w2_v[0], rel_bias)
    h = _outproj(y_nsa, y_pool, w_out[0].astype(CDT), h)
    hn = _rmsnorm(h, norm_ffn_g[0], CDT)
    act = _ffn_up(hn, w_up[0], conv_w[0], conv_b[0].reshape(1, -1))
    out = _ffn_down(act, w_down[0].astype(CDT), h, norm_final_g)
    return out.reshape(B, S, D)
```

```python
import functools
import math

import numpy as np
import jax
import jax.numpy as jnp
from jax import lax
from jax.experimental import pallas as pl
from jax.experimental.pallas import tpu as pltpu

POOL_WINDOWS = (2, 4, 8, 16)
POOL_GROUP = 256
POOL_WIDTH = POOL_GROUP * len(POOL_WINDOWS)
HEAD_DIM = 128
KV_GROUPS = 4
HEADS_PER_GROUP = 6
NSA_HEADS = KV_GROUPS * HEADS_PER_GROUP
Q_WIDTH = NSA_HEADS * HEAD_DIM
KV_WIDTH = KV_GROUPS * HEAD_DIM
CMP_BLOCK = 32
CMP_STRIDE = 16
SLC_BLOCK = 64
N_SELECT = 16
N_LOCAL_FORCED = 2
WINDOW = 512
N_BRANCH = 3
REL_BUCKETS = 32
REL_MAX_DIST = 128
CONV_WIDTH = 3
RMS_EPS = 1e-6
NEG = -1e30
LOG2E = math.log2(math.e)

LANES = 128
SUBLANES = 8
VMEM_LIMIT_BYTES = 56 * 1024 * 1024

CDT = jnp.bfloat16
QB = 128
PAD_ROWS = WINDOW
FAR_CHUNK = 512
NEAR_KEYS = FAR_CHUNK
FAR_UNROLL = 2
WIN_KEYS = WINDOW + QB
CMP_PER_TILE = QB // CMP_STRIDE
BAND_BACK = 16
BAND_ROWS = 24
BAND_TABLE = BAND_ROWS + BAND_BACK


def _cparams(*sem, flags=None):
    return pltpu.CompilerParams(dimension_semantics=sem, vmem_limit_bytes=VMEM_LIMIT_BYTES, flags=flags)


def _dot(a, b):
    return jnp.dot(a, b, preferred_element_type=jnp.float32)


def _dot_nt(a, b):
    return lax.dot_general(a, b, (((1,), (1,)), ((), ())), preferred_element_type=jnp.float32)


def _cast_body(x_ref, o_ref):
    o_ref[...] = x_ref[...].astype(o_ref.dtype)


def _cast_cols(w, n_cols, bm=512, bn=1024):
    rows = w.shape[0]
    bm, bn = min(bm, rows), min(bn, n_cols)
    assert rows % bm == 0 and n_cols % bn == 0
    return pl.pallas_call(
        _cast_body,
        grid=(rows // bm, n_cols // bn),
        in_specs=[pl.BlockSpec((bm, bn), lambda i, j: (i, j))],
        out_specs=pl.BlockSpec((bm, bn), lambda i, j: (i, j)),
        out_shape=jax.ShapeDtypeStruct((rows, n_cols), CDT),
        compiler_params=_cparams("parallel", "parallel"),
        name="cast_cols",
    )(w)


def _rmsnorm_body(x_ref, g_ref, o_ref):
    x = x_ref[...]
    y = x * lax.rsqrt(jnp.mean(x * x, axis=-1, keepdims=True) + RMS_EPS)
    o_ref[...] = (y * g_ref[...]).astype(o_ref.dtype)


def _rmsnorm(x, g, out_dtype, bm=256):
    S, D = x.shape
    return pl.pallas_call(
        _rmsnorm_body,
        grid=(S // bm,),
        in_specs=[pl.BlockSpec((bm, D), lambda i: (i, 0)), pl.BlockSpec((1, D), lambda i: (0, 0))],
        out_specs=pl.BlockSpec((bm, D), lambda i: (i, 0)),
        out_shape=jax.ShapeDtypeStruct((S, D), out_dtype),
        compiler_params=_cparams("parallel"),
        name="rmsnorm",
    )(x, g.reshape(1, D))


def _proj_body(a_ref, w_ref, o_ref, *, pad_tiles, n_scaled, scale, sigmoid):
    i = pl.program_id(0)
    j = pl.program_id(1)

    @pl.when(i < pad_tiles)
    def _():
        o_ref[...] = jnp.zeros_like(o_ref)

    @pl.when(i >= pad_tiles)
    def _():
        r = _dot(a_ref[...], w_ref[...])
        if n_scaled:
            r = r * jnp.where(j < n_scaled, jnp.float32(scale), jnp.float32(1.0))
        if sigmoid:
            r = jax.nn.sigmoid(r)
        o_ref[...] = r.astype(o_ref.dtype)


def _proj(a, w, out_dtype, *, col0=0, n_cols=None, pad_tiles=0, scaled_cols=0, scale=1.0, sigmoid=False,
          bm=512, bn=1024, name="proj"):
    S, D = a.shape
    N = w.shape[1] if n_cols is None else n_cols
    bn = min(bn, N)
    assert scaled_cols % bn == 0 and col0 % bn == 0 and N % bn == 0
    n_scaled = scaled_cols // bn
    first = col0 // bn
    body = functools.partial(_proj_body, pad_tiles=pad_tiles, n_scaled=n_scaled, scale=scale, sigmoid=sigmoid)
    return pl.pallas_call(
        body,
        grid=(S // bm + pad_tiles, N // bn),
        in_specs=[pl.BlockSpec((bm, D), lambda i, j: (jnp.maximum(i - pad_tiles, 0), 0)),
                  pl.BlockSpec((D, bn), lambda i, j: (0, first + j))],
        out_specs=pl.BlockSpec((bm, bn), lambda i, j: (i, j)),
        out_shape=jax.ShapeDtypeStruct((S + pad_tiles * bm, N), out_dtype),
        compiler_params=_cparams("parallel", "arbitrary"),
        name=name,
    )(a, w)


POOL_HALO = 16


def _pool_body(u_ref, halo_ref, w_ref, s_ref, o_ref):
    i = pl.program_id(0)
    bm = u_ref.shape[0]
    u = u_ref[...]
    halo = jnp.where(i > 0, halo_ref[...], 0.0)
    ext = jnp.concatenate([halo, u], axis=0)
    t = i * bm + lax.broadcasted_iota(jnp.int32, (bm, 1), 0)
    acc = ext
    sums = {}
    shift = 1
    while shift < POOL_WINDOWS[-1]:
        acc = acc + pltpu.roll(acc, shift, axis=0)
        shift *= 2
        sums[shift] = acc
    for gi, w in enumerate(POOL_WINDOWS):
        cols = slice(gi * POOL_GROUP, (gi + 1) * POOL_GROUP)
        cnt = jnp.minimum(t + 1, w).astype(jnp.float32)
        d = sums[w][POOL_HALO:, cols] / cnt - u[:, cols]
        y = _dot(d.astype(CDT), w_ref[gi])
        o_ref[:, cols] = (y * s_ref[:, cols]).astype(o_ref.dtype)


def _pool(u, pool_w, pool_scale, bm=512):
    S = u.shape[0]
    return pl.pallas_call(
        _pool_body,
        grid=(S // bm,),
        in_specs=[pl.BlockSpec((bm, POOL_WIDTH), lambda i: (i, 0)),
                  pl.BlockSpec((POOL_HALO, POOL_WIDTH), lambda i: (jnp.maximum(i * (bm // POOL_HALO) - 1, 0), 0)),
                  pl.BlockSpec((len(POOL_WINDOWS), POOL_GROUP, POOL_GROUP), lambda i: (0, 0, 0)),
                  pl.BlockSpec((1, POOL_WIDTH), lambda i: (0, 0))],
        out_specs=pl.BlockSpec((bm, POOL_WIDTH), lambda i: (i, 0)),
        out_shape=jax.ShapeDtypeStruct((S, POOL_WIDTH), CDT),
        compiler_params=_cparams("parallel"),
        name="pool",
    )(u, u, pool_w.astype(CDT), pool_scale.reshape(1, POOL_WIDTH))


def _compress_body(x_ref, w1_ref, pe_ref, w2_ref, o_ref, acc_ref, pe_acc_ref, *, pad_chunks):
    l = pl.program_id(2)

    @pl.when(l == 0)
    def _():
        acc_ref[...] = jnp.zeros_like(acc_ref)
        pe_acc_ref[...] = jnp.zeros_like(pe_acc_ref)

    w_lo = w1_ref[0, 0]
    w_hi = w1_ref[1, 0]
    x = x_ref[...]
    acc_ref[:, :HEAD_DIM] += _dot(x, w_lo)
    acc_ref[:, HEAD_DIM:] += _dot(x, w_hi)
    pe_lo = jnp.broadcast_to(pe_ref[0, 0], (SUBLANES, HEAD_DIM))
    pe_hi = jnp.broadcast_to(pe_ref[1, 0], (SUBLANES, HEAD_DIM))
    pe_acc_ref[...] += _dot(pe_lo, w_lo) + _dot(pe_hi, w_hi)

    @pl.when(l == CMP_STRIDE - 1)
    def _():
        n = o_ref.shape[0]
        a = acc_ref[pad_chunks:pad_chunks + n, :HEAD_DIM]
        b = acc_ref[:, HEAD_DIM:]
        b_next = pltpu.roll(b, b.shape[0] - 1, axis=0)[pad_chunks:pad_chunks + n]
        pre = a + b_next + pe_acc_ref[0:1, :]
        h = pre * jax.nn.sigmoid(pre)
        o_ref[...] = _dot(h.astype(CDT), w2_ref[0]).astype(o_ref.dtype)


def _compress(qkv, w1, pe, w2):
    rows = qkv.shape[0]
    chunks = rows // CMP_STRIDE
    pad_chunks = PAD_ROWS // CMP_STRIDE
    n_out = chunks - pad_chunks
    width = 2 * KV_GROUPS
    x = qkv[:, Q_WIDTH:Q_WIDTH + 2 * KV_WIDTH].reshape(chunks, CMP_STRIDE * 2 * KV_WIDTH)
    body = functools.partial(_compress_body, pad_chunks=pad_chunks)
    return pl.pallas_call(
        body,
        grid=(2, KV_GROUPS, CMP_STRIDE),
        in_specs=[pl.BlockSpec((chunks, LANES), lambda kv, g, l: (0, l * width + kv * KV_GROUPS + g)),
                  pl.BlockSpec((2, 1, HEAD_DIM, HEAD_DIM), lambda kv, g, l: (kv, l, 0, 0)),
                  pl.BlockSpec((2, 1, 1, HEAD_DIM), lambda kv, g, l: (kv, l, 0, 0)),
                  pl.BlockSpec((1, HEAD_DIM, HEAD_DIM), lambda kv, g, l: (kv, 0, 0))],
        out_specs=pl.BlockSpec((None, None, n_out, HEAD_DIM), lambda kv, g, l: (kv, g, 0, 0)),
        out_shape=jax.ShapeDtypeStruct((2, KV_GROUPS, n_out, HEAD_DIM), CDT),
        scratch_shapes=[pltpu.VMEM((chunks, 2 * HEAD_DIM), jnp.float32),
                        pltpu.VMEM((SUBLANES, HEAD_DIM), jnp.float32)],
        compiler_params=_cparams("parallel", "parallel", "arbitrary"),
        name="compress",
    )(x, w1, pe, w2)


def _rel_bucket_np(dist):
    n = np.maximum(dist, 0)
    max_exact = REL_BUCKETS // 2
    nf = np.maximum(n, 1).astype(np.float32)
    large = max_exact + (np.log(nf / max_exact) / math.log(REL_MAX_DIST / max_exact)
                         * (REL_BUCKETS - max_exact)).astype(np.int32)
    large = np.minimum(large, REL_BUCKETS - 1)
    return np.where(n < max_exact, n, large).astype(np.int32)


def _bias_index_tiles():
    q = np.arange(QB)[:, None]
    dist = NEAR_KEYS - QB + q - np.arange(NEAR_KEYS)[None, :]
    near = np.where(dist >= 0, _rel_bucket_np(dist), -1)
    dist = WINDOW + q - np.arange(WIN_KEYS)[None, :]
    win = np.where((dist >= 0) & (dist < WINDOW), _rel_bucket_np(dist), -1)
    w = np.arange(LANES)[None, :]
    dist = q - CMP_STRIDE * (w - BAND_BACK) - (CMP_BLOCK - 1)
    band = np.where(dist >= 0, _rel_bucket_np(dist), -1)
    return np.concatenate([near, win, band], axis=1).astype(np.int32)


def _bias_body(tbl_ref, idx_ref, o_ref):
    h = pl.program_id(0)
    idx = idx_ref[...]
    far = tbl_ref[REL_BUCKETS - 1, h]
    val = jnp.full(idx.shape, NEG, jnp.float32)
    for b in range(REL_BUCKETS):
        val = jnp.where(idx == b, (tbl_ref[b, h] - far) * LOG2E, val)
    o_ref[...] = val


def _bias_tiles(rel_bias):
    idx = jnp.asarray(_bias_index_tiles())
    width = idx.shape[1]
    return pl.pallas_call(
        _bias_body,
        grid=(NSA_HEADS,),
        in_specs=[pl.BlockSpec(memory_space=pltpu.SMEM),
                  pl.BlockSpec((QB, width), lambda h: (0, 0))],
        out_specs=pl.BlockSpec((None, QB, width), lambda h: (h, 0, 0)),
        out_shape=jax.ShapeDtypeStruct((NSA_HEADS, QB, width), jnp.float32),
        compiler_params=_cparams("arbitrary"),
        name="bias_tiles",
    )(rel_bias, idx)


def _softmax_cols(s):
    m = jnp.max(s, axis=0, keepdims=True)
    p = jnp.exp2(s - m)
    l = jnp.sum(p, axis=0, keepdims=True)
    return p, jnp.where(m > 0.5 * NEG, 1.0 / l, 0.0)


def _split3(x):
    hi = x.astype(CDT)
    r = x - hi.astype(jnp.float32)
    mid = r.astype(CDT)
    lo = (r - mid.astype(jnp.float32)).astype(CDT)
    return hi, mid, lo


def _select_blocks(slc_t, i):
    n_slc = slc_t.shape[0]
    t = i * QB + lax.broadcasted_iota(jnp.int32, (1, QB), 1)
    j_int = lax.broadcasted_iota(jnp.int32, (n_slc, QB), 0)
    j_idx = j_int.astype(jnp.float32)
    cur = t // SLC_BLOCK
    forced = (j_int == 0) | ((cur - j_int >= 0) & (cur - j_int < N_LOCAL_FORCED))
    score = jnp.where(forced, 1e9, jnp.where(j_int > cur, -1e9, slc_t))
    picked = jnp.zeros(score.shape, jnp.bool_)
    for _ in range(min(N_SELECT, n_slc)):
        m = jnp.max(score, axis=0, keepdims=True)
        first = jnp.min(jnp.where(score == m, j_idx, float(n_slc)), axis=0, keepdims=True)
        hit = j_idx == first
        picked = picked | hit
        score = jnp.where(hit, -3e38, score)
    return picked


def _attn_body(q_ref, gate_ref, kc_ref, vct_ref, ks_ref, vst_ref, kw_ref, vwt_ref, blk_ref, tab_ref, wov_ref,
               o_ref, kaug_ref, kwaug_ref, sc_ref, pc_ref, sw_ref, pw_ref, s_ref, p_ref, acc_ref):
    i = pl.program_id(1)
    H = HEADS_PER_GROUP
    n_cmp = kc_ref.shape[0]
    n_slc = wov_ref.shape[0]
    lanes = H * QB
    head = lambda x, h: x[:, h * QB:(h + 1) * QB]

    @pl.when(i == 0)
    def _():
        kaug_ref[:, :HEAD_DIM] = ks_ref[...]
        kaug_ref[:, HEAD_DIM:] = blk_ref[...]
        kwaug_ref[:, :HEAD_DIM] = kw_ref[...]
        row = lax.broadcasted_iota(jnp.int32, (kwaug_ref.shape[0], LANES), 0)
        kwaug_ref[:, HEAD_DIM:] = jnp.where(row < PAD_ROWS, 1.0, 0.0).astype(CDT)

    qb = q_ref[...]
    q_t = jnp.concatenate([qb[:, h * HEAD_DIM:(h + 1) * HEAD_DIM].astype(jnp.float32).T.astype(CDT)
                           for h in range(H)], axis=1)

    def augment(extra):
        return jnp.concatenate([q_t, extra.astype(CDT)], axis=0)

    def values_t(ref, chunk0, n):
        return jnp.concatenate([ref[chunk0 + c] for c in range(n)], axis=1)

    grp = lax.broadcasted_iota(jnp.int32, (LANES, lanes), 0)
    sc_ref[...] = _dot(kc_ref[...], augment(jnp.where(grp > i, NEG, 0.0)))
    band0 = jnp.maximum(CMP_PER_TILE * i - BAND_BACK, 0)
    tab0 = band0 - (CMP_PER_TILE * i - BAND_BACK)
    band_rows = pl.ds(pl.multiple_of(band0, SUBLANES), BAND_ROWS)
    sc_ref[band_rows, :] += tab_ref[pl.ds(pl.multiple_of(NEAR_KEYS + WIN_KEYS + tab0, SUBLANES), BAND_ROWS), :]
    s_c = sc_ref[...]
    m_c = jnp.max(s_c, axis=0, keepdims=True)
    p_c = jnp.exp2(s_c - m_c)
    l_c = jnp.sum(p_c, axis=0, keepdims=True)
    pc_ref[...] = p_c.astype(CDT)
    seen = m_c > 0.5 * NEG
    o_cmp_t = _dot(vct_ref[...], pc_ref[...]) * jnp.where(seen, 1.0 / l_c, 0.0)
    shift_c = jnp.where(seen, m_c + jnp.log2(l_c), -NEG)
    imp_t = functools.reduce(lambda a, b: a + b,
                             [jnp.exp2(sc_ref[:, h * QB:(h + 1) * QB] - head(shift_c, h)) for h in range(H)])

    wov_t = wov_ref[...]
    slc_t = sum(_dot(wov_t, part) for part in _split3(imp_t))
    drop = jnp.where(_select_blocks(slc_t, i), 0.0, 1.0)
    j_row = lax.broadcasted_iota(jnp.int32, (n_slc, QB), 0)
    near_tile = i + 1 - NEAR_KEYS // QB
    drop_far = jnp.where(j_row < near_tile * (QB // SLC_BLOCK), drop, 1.0)
    tile_heads = lambda d: jnp.concatenate([d] * H, axis=1)

    win0 = pl.multiple_of(i * QB, QB)
    first_row = lax.broadcasted_iota(jnp.int32, (LANES, lanes), 0) == 0
    sw_ref[...] = (_dot(kwaug_ref[pl.ds(win0, WIN_KEYS), :], augment(jnp.where(first_row, NEG, 0.0)))
                   + tab_ref[NEAR_KEYS:NEAR_KEYS + WIN_KEYS])
    p_w, scale_w = _softmax_cols(sw_ref[...])
    pw_ref[...] = p_w.astype(CDT)
    o_win_t = _dot(values_t(vwt_ref, win0 // LANES, WIN_KEYS // LANES), pw_ref[...]) * scale_w

    near0 = pl.multiple_of(PAD_ROWS + near_tile * QB, QB)
    s_ref[...] = _dot(kaug_ref[pl.ds(near0, NEAR_KEYS), :], augment(tile_heads(drop))) + tab_ref[:NEAR_KEYS]
    p_ref[...] = jnp.zeros(p_ref.shape, CDT)
    acc_ref[...] = jnp.zeros(acc_ref.shape, jnp.float32)
    q_far = augment(tile_heads(drop_far))

    n_chunks = (kaug_ref.shape[0] - PAD_ROWS) // FAR_CHUNK
    n_far = (jnp.maximum(near_tile, 0) * QB + FAR_CHUNK - 1) // FAR_CHUNK
    far_lanes = FAR_CHUNK // LANES

    def far_step(it, carry):
        a_prev, m, l = carry
        v0 = jnp.where(it == 1, near0 // LANES, PAD_ROWS // LANES + jnp.clip(it - 2, 0, n_chunks - 1) * far_lanes)
        acc_ref[...] = a_prev * acc_ref[...] + _dot(values_t(vst_ref, v0, far_lanes), p_ref[...])
        s_prev = s_ref[...]
        m_new = jnp.maximum(m, jnp.max(s_prev, axis=0, keepdims=True))
        a = jnp.exp2(m - m_new)
        p = jnp.exp2(s_prev - m_new)
        l = a * l + jnp.sum(p, axis=0, keepdims=True)
        p_ref[...] = p.astype(CDT)
        k0 = pl.multiple_of(jnp.where(it < n_far, PAD_ROWS + it * FAR_CHUNK, 0), FAR_CHUNK)
        s_ref[...] = _dot(kaug_ref[pl.ds(k0, FAR_CHUNK), :], q_far)
        return a, m_new, l

    def far_trip(k, carry):
        for u in range(FAR_UNROLL):
            carry = far_step(FAR_UNROLL * k + u, carry)
        return carry

    zero = scale_w - scale_w
    trips = (n_far + 2 + FAR_UNROLL - 1) // FAR_UNROLL
    _, _, l_s = lax.fori_loop(0, trips, far_trip, (zero + 1.0, zero + 0.1 * NEG, zero))
    o_slc_t = acc_ref[...] * (1.0 / l_s)

    gates_t = gate_ref[...].T
    for h in range(H):
        g = lambda b: gates_t[h * N_BRANCH + b:h * N_BRANCH + b + 1, :]
        o_t = g(0) * head(o_cmp_t, h) + g(1) * head(o_slc_t, h) + g(2) * head(o_win_t, h)
        o_ref[:, h * HEAD_DIM:(h + 1) * HEAD_DIM] = o_t.T.astype(o_ref.dtype)


def _block_membership(rows, n_slc):
    key = np.arange(rows)[:, None] - PAD_ROWS
    j = np.arange(n_slc)[None, :]
    member = np.where(key >= 0, key // SLC_BLOCK == j, True)
    return jnp.asarray(np.where(member, NEG, 0.0), dtype=CDT)


def _overlap_weights(n_cmp, n_slc):
    r = SLC_BLOCK // CMP_STRIDE
    lead = -(-CMP_BLOCK // CMP_STRIDE) - 1
    w = np.zeros((n_cmp, n_slc), np.float32)
    for o in range(-lead, r):
        s0 = o * CMP_STRIDE
        ov = max(0, min(s0 + CMP_BLOCK, SLC_BLOCK) - max(s0, 0))
        for j in range(n_slc):
            n = r * j + o
            if ov > 0 and 0 <= n < n_cmp - 1:
                w[n, j] = ov / CMP_STRIDE
    return jnp.asarray(w.T, dtype=CDT)


def _tile_group_columns(n_cmp):
    n = np.arange(n_cmp)[:, None]
    return jnp.asarray(n // CMP_PER_TILE == np.arange(LANES)[None, :], dtype=CDT)


def _attention(qkv, gates, kvc, tabs):
    rows = qkv.shape[0]
    S = rows - PAD_ROWS
    n_cmp = kvc.shape[2]
    n_slc = S // SLC_BLOCK
    assert n_cmp // CMP_PER_TILE <= LANES
    first = Q_WIDTH // LANES
    G = KV_GROUPS
    lanes = HEADS_PER_GROUP * QB
    kv_spec = lambda which: pl.BlockSpec((rows, LANES), lambda g, i: (0, first + which * G + g))
    vt_spec = lambda which: pl.BlockSpec((None, None, rows // LANES, HEAD_DIM, LANES), lambda g, i: (which, g, 0, 0, 0))
    width = tabs.shape[2]
    tabs_t = tabs.reshape(G, HEADS_PER_GROUP, QB, width).transpose(0, 3, 1, 2).reshape(G, width, lanes)
    kc_aug = jnp.concatenate([kvc[0], jnp.broadcast_to(_tile_group_columns(n_cmp), (G, n_cmp, LANES))], axis=-1)
    vc_t = kvc[1].transpose(0, 2, 1)
    v_cols = jnp.stack([qkv[:, (first + w * G) * LANES:(first + (w + 1) * G) * LANES] for w in (3, 5)])
    v_t = v_cols.reshape(2, rows // LANES, LANES, G, HEAD_DIM).transpose(0, 3, 1, 4, 2)
    return pl.pallas_call(
        _attn_body,
        grid=(G, S // QB),
        in_specs=[pl.BlockSpec((QB, HEADS_PER_GROUP * HEAD_DIM), lambda g, i: (i + PAD_ROWS // QB, g)),
                  pl.BlockSpec((QB, LANES), lambda g, i: (i, g)),
                  pl.BlockSpec((None, n_cmp, HEAD_DIM + LANES), lambda g, i: (g, 0, 0)),
                  pl.BlockSpec((None, HEAD_DIM, n_cmp), lambda g, i: (g, 0, 0)),
                  kv_spec(2), vt_spec(0), kv_spec(4), vt_spec(1),
                  pl.BlockSpec((rows, n_slc), lambda g, i: (0, 0)),
                  pl.BlockSpec((None, width, lanes), lambda g, i: (g, 0, 0)),
                  pl.BlockSpec((n_slc, n_cmp), lambda g, i: (0, 0))],
        out_specs=pl.BlockSpec((QB, HEADS_PER_GROUP * HEAD_DIM), lambda g, i: (i, g)),
        out_shape=jax.ShapeDtypeStruct((S, Q_WIDTH), CDT),
        scratch_shapes=[pltpu.VMEM((rows, HEAD_DIM + n_slc), CDT),
                        pltpu.VMEM((rows, HEAD_DIM + LANES), CDT),
                        pltpu.VMEM((n_cmp, lanes), jnp.float32),
                        pltpu.VMEM((n_cmp, lanes), CDT),
                        pltpu.VMEM((WIN_KEYS, lanes), jnp.float32),
                        pltpu.VMEM((WIN_KEYS, lanes), CDT),
                        pltpu.VMEM((FAR_CHUNK, lanes), jnp.float32),
                        pltpu.VMEM((FAR_CHUNK, lanes), CDT),
                        pltpu.VMEM((HEAD_DIM, lanes), jnp.float32)],
        compiler_params=_cparams("parallel", "arbitrary"),
        name="nsa",
    )(qkv, gates, kc_aug, vc_t, qkv, v_t, qkv, v_t, _block_membership(rows, n_slc), tabs_t,
      _overlap_weights(n_cmp, n_slc))


def _outproj_body(a1_ref, a2_ref, w1_ref, w2_ref, x_ref, o_ref):
    o_ref[...] = x_ref[...] + _dot(a1_ref[...], w1_ref[...]) + _dot(a2_ref[...], w2_ref[...])


def _outproj(y_nsa, y_pool, w_out, x, bm=512, bn=1024):
    S, D = x.shape
    bn = min(bn, D)
    k1, k2 = y_nsa.shape[1], y_pool.shape[1]
    return pl.pallas_call(
        _outproj_body,
        grid=(S // bm, D // bn),
        in_specs=[pl.BlockSpec((bm, k1), lambda i, j: (i, 0)),
                  pl.BlockSpec((bm, k2), lambda i, j: (i, 0)),
                  pl.BlockSpec((k1, bn), lambda i, j: (0, j)),
                  pl.BlockSpec((k2, bn), lambda i, j: (k1 // k2, j)),
                  pl.BlockSpec((bm, bn), lambda i, j: (i, j))],
        out_specs=pl.BlockSpec((bm, bn), lambda i, j: (i, j)),
        out_shape=jax.ShapeDtypeStruct((S, D), jnp.float32),
        compiler_params=_cparams("parallel", "arbitrary"),
        name="outproj",
    )(y_nsa, y_pool, w_out, w_out, x)


CONV_HALO = 16
FFN_SUBTILES = 2
FFN_K_TILE = 512


def _ffn_up_body(a_ref, halo_ref, wa_ref, wb_ref, cwa_ref, cwb_ref, cba_ref, cbb_ref, o_ref, lhs_ref, *, nb):
    i = pl.program_id(0)
    j = pl.program_id(1)

    @pl.when(j == 0)
    def _():
        lhs_ref[:CONV_HALO] = jnp.where(i > 0, halo_ref[...], jnp.zeros_like(halo_ref))
        lhs_ref[CONV_HALO:] = a_ref[...]

    def conv(lhs, w, cw_ref, cb_ref):
        u = _dot(lhs, w)
        c = cb_ref[...] + cw_ref[CONV_WIDTH - 1:CONV_WIDTH, :] * u[CONV_HALO:]
        for k in range(1, CONV_WIDTH):
            c = c + cw_ref[CONV_WIDTH - 1 - k:CONV_WIDTH - k, :] * pltpu.roll(u, k, axis=0)[CONV_HALO:]
        return c

    @pl.when(j < nb)
    def _():
        wa = wa_ref[...].astype(CDT)
        wb = wb_ref[...].astype(CDT)
        sub = o_ref.shape[0] // FFN_SUBTILES
        for s in range(FFN_SUBTILES):
            lhs = lhs_ref[s * sub:s * sub + CONV_HALO + sub]
            ca = conv(lhs, wa, cwa_ref, cba_ref)
            cb = conv(lhs, wb, cwb_ref, cbb_ref)
            o_ref[s * sub:(s + 1) * sub] = (ca * jax.nn.sigmoid(ca) * cb).astype(o_ref.dtype)

    @pl.when(j >= nb)
    def _():
        o_ref[...] = jnp.zeros_like(o_ref)


def _ffn_up(hn, w_up, conv_w, conv_b, f_pad, bm=1024, bn=256):
    S, D = hn.shape
    F = w_up.shape[1] // 2
    nb = F // bn
    col = lambda j: jnp.minimum(j, nb - 1)
    return pl.pallas_call(
        functools.partial(_ffn_up_body, nb=nb),
        grid=(S // bm, f_pad // bn),
        in_specs=[pl.BlockSpec((bm, D), lambda i, j: (i, 0)),
                  pl.BlockSpec((CONV_HALO, D), lambda i, j: (jnp.maximum(i * (bm // CONV_HALO) - 1, 0), 0)),
                  pl.BlockSpec((D, bn), lambda i, j: (0, col(j))),
                  pl.BlockSpec((D, bn), lambda i, j: (0, nb + col(j))),
                  pl.BlockSpec((CONV_WIDTH, bn), lambda i, j: (0, col(j))),
                  pl.BlockSpec((CONV_WIDTH, bn), lambda i, j: (0, nb + col(j))),
                  pl.BlockSpec((1, bn), lambda i, j: (0, col(j))),
                  pl.BlockSpec((1, bn), lambda i, j: (0, nb + col(j)))],
        out_specs=pl.BlockSpec((bm, bn), lambda i, j: (i, j)),
        out_shape=jax.ShapeDtypeStruct((S, f_pad), CDT),
        scratch_shapes=[pltpu.VMEM((CONV_HALO + bm, D), CDT)],
        compiler_params=_cparams("parallel", "arbitrary"),
        name="ffn_up",
    )(hn, hn, w_up, w_up, conv_w, conv_w, conv_b, conv_b)


def _ffn_down_body(a_ref, w_ref, h_ref, g_ref, o_ref):
    k = pl.program_id(1)

    @pl.when(k == 0)
    def _():
        o_ref[...] = h_ref[...]

    o_ref[...] += _dot(a_ref[...], w_ref[...])

    @pl.when(k == pl.num_programs(1) - 1)
    def _():
        x = o_ref[...]
        y = x * lax.rsqrt(jnp.mean(x * x, axis=-1, keepdims=True) + RMS_EPS)
        o_ref[...] = y * g_ref[...]


def _ffn_down(act, w_down, h, g, bm=512, bk=FFN_K_TILE):
    S, F = act.shape
    D = h.shape[1]
    return pl.pallas_call(
        _ffn_down_body,
        grid=(S // bm, F // bk),
        in_specs=[pl.BlockSpec((bm, bk), lambda i, k: (i, k)),
                  pl.BlockSpec((bk, D), lambda i, k: (k, 0)),
                  pl.BlockSpec((bm, D), lambda i, k: (i, 0)),
                  pl.BlockSpec((1, D), lambda i, k: (0, 0))],
        out_specs=pl.BlockSpec((bm, D), lambda i, k: (i, 0)),
        out_shape=jax.ShapeDtypeStruct((S, D), jnp.float32),
        compiler_params=_cparams("parallel", "arbitrary"),
        name="ffn_down",
    )(act, w_down, h, g.reshape(1, D))


def _gate_weight(w_gate):
    D = w_gate.shape[0]
    per = HEADS_PER_GROUP * N_BRANCH
    w = w_gate.reshape(D, KV_GROUPS, per)
    w = jnp.pad(w, ((0, 0), (0, 0), (0, LANES - per)))
    return w.reshape(D, KV_GROUPS * LANES)


def _nsa_branch(hn, w_in, w_gate, cmp_w1_k, cmp_pe_k, cmp_w2_k, cmp_w1_v, cmp_pe_v, cmp_w2_v, rel_bias):
    qkv_cols = Q_WIDTH + 6 * KV_WIDTH
    w_gate = _gate_weight(w_gate).astype(CDT)
    bm = 512
    qkv = _proj(hn, w_in, CDT, col0=POOL_WIDTH, n_cols=qkv_cols, pad_tiles=PAD_ROWS // bm, scaled_cols=Q_WIDTH,
                scale=HEAD_DIM ** -0.5 * LOG2E, bm=bm, name="proj_qkv")
    gates = _proj(hn, w_gate, jnp.float32, sigmoid=True, bm=bm, name="proj_gate")
    half = lambda w: w.reshape(2, CMP_STRIDE, *w.shape[1:])
    w1 = jnp.stack([half(cmp_w1_k), half(cmp_w1_v)]).astype(CDT)
    pe = jnp.stack([half(cmp_pe_k), half(cmp_pe_v)]).astype(CDT)
    w1 = w1.reshape(2 * 2, CMP_STRIDE, HEAD_DIM, HEAD_DIM)
    pe = pe.reshape(2 * 2, CMP_STRIDE, 1, HEAD_DIM)
    w2 = jnp.stack([cmp_w2_k, cmp_w2_v]).astype(CDT)
    kvc = _compress(qkv, w1, pe, w2)
    tabs = _bias_tiles(rel_bias)
    return _attention(qkv, gates, kvc, tabs)


def kernel(x, norm_mix_g, w_in, pool_w, pool_scale, cmp_pe_k, cmp_w1_k, cmp_w2_k, cmp_pe_v, cmp_w1_v, cmp_w2_v,
           rel_bias, w_out, norm_ffn_g, w_up, conv_w, conv_b, w_down, norm_final_g):
    B, S, D = x.shape
    assert B == 1 and w_in.shape[0] == 1, "single sequence, single layer"
    h = x.reshape(S, D)
    hn = _rmsnorm(h, norm_mix_g[0], CDT)
    n_main = POOL_WIDTH + Q_WIDTH + 6 * KV_WIDTH
    w_all = _cast_cols(w_in[0], n_main)
    u_pool = _proj(hn, w_all, jnp.float32, n_cols=POOL_WIDTH, name="proj_pool")
    y_pool = _pool(u_pool, pool_w[0], pool_scale[0])
    y_nsa = _nsa_branch(hn, w_all, w_in[0][:, n_main:], cmp_w1_k[0], cmp_pe_k[0], cmp_w2_k[0],
                        cmp_w1_v[0], cmp_pe_v[0], cmp_w2_v[0], rel_bias)
    h = _outproj(y_nsa, y_pool, w_out[0].astype(CDT), h)
    hn = _rmsnorm(h, norm_ffn_g[0], CDT)
    F = w_down.shape[1]
    f_pad = -(-F // FFN_K_TILE) * FFN_K_TILE
    act = _ffn_up(hn, w_up[0], conv_w[0], conv_b[0].reshape(1, -1), f_pad)
    out = _ffn_down(act, jnp.pad(w_down[0], ((0, f_pad - F), (0, 0))).astype(CDT), h, norm_final_g)
    return out.reshape(B, S, D)
```

```python
import functools
import math

import numpy as np
import jax
import jax.numpy as jnp
from jax import lax
from jax.experimental import pallas as pl
from jax.experimental.pallas import tpu as pltpu

POOL_WINDOWS = (2, 4, 8, 16)
POOL_GROUP = 256
POOL_WIDTH = POOL_GROUP * len(POOL_WINDOWS)
HEAD_DIM = 128
KV_GROUPS = 4
HEADS_PER_GROUP = 6
NSA_HEADS = KV_GROUPS * HEADS_PER_GROUP
Q_WIDTH = NSA_HEADS * HEAD_DIM
KV_WIDTH = KV_GROUPS * HEAD_DIM
CMP_BLOCK = 32
CMP_STRIDE = 16
SLC_BLOCK = 64
N_SELECT = 16
N_LOCAL_FORCED = 2
WINDOW = 512
N_BRANCH = 3
REL_BUCKETS = 32
REL_MAX_DIST = 128
CONV_WIDTH = 3
RMS_EPS = 1e-6
NEG = -1e30
LOG2E = math.log2(math.e)

LANES = 128
SUBLANES = 8
VMEM_LIMIT_BYTES = 56 * 1024 * 1024

CDT = jnp.bfloat16
QB = 128
PAD_ROWS = WINDOW
FAR_CHUNK = 512
NEAR_KEYS = FAR_CHUNK
FAR_UNROLL = 2
WIN_KEYS = WINDOW + QB
CMP_PER_TILE = QB // CMP_STRIDE
BAND_BACK = 16
BAND_ROWS = 24
BAND_TABLE = BAND_ROWS + BAND_BACK


def _cparams(*sem, flags=None):
    return pltpu.CompilerParams(dimension_semantics=sem, vmem_limit_bytes=VMEM_LIMIT_BYTES, flags=flags)


def _dot(a, b):
    return jnp.dot(a, b, preferred_element_type=jnp.float32)


def _dot_nt(a, b):
    return lax.dot_general(a, b, (((1,), (1,)), ((), ())), preferred_element_type=jnp.float32)


def _cast_body(x_ref, o_ref, *, rows_valid):
    x = x_ref[...]
    row = pl.program_id(0) * x.shape[0] + lax.broadcasted_iota(jnp.int32, x.shape, 0)
    o_ref[...] = jnp.where(row < rows_valid, x, 0.0).astype(o_ref.dtype)


def _cast_weight(w, layer, n_rows, n_cols, bm=256, bn=1024):
    rows = w.shape[1]
    bm, bn = min(bm, rows), min(bn, n_cols)
    assert rows % bm == 0 and n_rows % bm == 0 and n_cols % bn == 0
    last = rows // bm - 1
    return pl.pallas_call(
        functools.partial(_cast_body, rows_valid=rows),
        grid=(n_rows // bm, n_cols // bn),
        in_specs=[pl.BlockSpec((None, bm, bn), lambda i, j: (layer, jnp.minimum(i, last), j))],
        out_specs=pl.BlockSpec((bm, bn), lambda i, j: (i, j)),
        out_shape=jax.ShapeDtypeStruct((n_rows, n_cols), CDT),
        compiler_params=_cparams("parallel", "parallel"),
        name="cast_weight",
    )(w)


def _rmsnorm_body(x_ref, g_ref, o_ref):
    x = x_ref[...]
    y = x * lax.rsqrt(jnp.mean(x * x, axis=-1, keepdims=True) + RMS_EPS)
    o_ref[...] = (y * g_ref[...]).astype(o_ref.dtype)


def _rmsnorm(x, g, out_dtype, bm=256):
    S, D = x.shape
    return pl.pallas_call(
        _rmsnorm_body,
        grid=(S // bm,),
        in_specs=[pl.BlockSpec((bm, D), lambda i: (i, 0)), pl.BlockSpec((1, D), lambda i: (0, 0))],
        out_specs=pl.BlockSpec((bm, D), lambda i: (i, 0)),
        out_shape=jax.ShapeDtypeStruct((S, D), out_dtype),
        compiler_params=_cparams("parallel"),
        name="rmsnorm",
    )(x, g.reshape(1, D))


def _proj_body(a_ref, w_ref, o_ref, *, pad_tiles, n_scaled, scale, sigmoid):
    i = pl.program_id(0)
    j = pl.program_id(1)

    @pl.when(i < pad_tiles)
    def _():
        o_ref[...] = jnp.zeros_like(o_ref)

    @pl.when(i >= pad_tiles)
    def _():
        r = _dot(a_ref[...], w_ref[...].astype(a_ref.dtype))
        if n_scaled:
            r = r * jnp.where(j < n_scaled, jnp.float32(scale), jnp.float32(1.0))
        if sigmoid:
            r = jax.nn.sigmoid(r)
        o_ref[...] = r.astype(o_ref.dtype)


def _proj(a, w, out_dtype, *, col0=0, n_cols=None, pad_tiles=0, scaled_cols=0, scale=1.0, sigmoid=False,
          bm=512, bn=1024, name="proj"):
    S, D = a.shape
    N = w.shape[1] if n_cols is None else n_cols
    bn = min(bn, N)
    assert scaled_cols % bn == 0 and col0 % bn == 0 and N % bn == 0
    n_scaled = scaled_cols // bn
    first = col0 // bn
    body = functools.partial(_proj_body, pad_tiles=pad_tiles, n_scaled=n_scaled, scale=scale, sigmoid=sigmoid)
    return pl.pallas_call(
        body,
        grid=(S // bm + pad_tiles, N // bn),
        in_specs=[pl.BlockSpec((bm, D), lambda i, j: (jnp.maximum(i - pad_tiles, 0), 0)),
                  pl.BlockSpec((D, bn), lambda i, j: (0, first + j))],
        out_specs=pl.BlockSpec((bm, bn), lambda i, j: (i, j)),
        out_shape=jax.ShapeDtypeStruct((S + pad_tiles * bm, N), out_dtype),
        compiler_params=_cparams("parallel", "arbitrary"),
        name=name,
    )(a, w)


POOL_HALO = 16


def _pool_body(u_ref, halo_ref, w_ref, s_ref, o_ref):
    i = pl.program_id(0)
    bm = u_ref.shape[0]
    u = u_ref[...]
    halo = jnp.where(i > 0, halo_ref[...], 0.0)
    ext = jnp.concatenate([halo, u], axis=0)
    t = i * bm + lax.broadcasted_iota(jnp.int32, (bm, 1), 0)
    acc = ext
    sums = {}
    shift = 1
    while shift < POOL_WINDOWS[-1]:
        acc = acc + pltpu.roll(acc, shift, axis=0)
        shift *= 2
        sums[shift] = acc
    for gi, w in enumerate(POOL_WINDOWS):
        cols = slice(gi * POOL_GROUP, (gi + 1) * POOL_GROUP)
        cnt = jnp.minimum(t + 1, w).astype(jnp.float32)
        d = sums[w][POOL_HALO:, cols] / cnt - u[:, cols]
        y = _dot(d.astype(CDT), w_ref[gi])
        o_ref[:, cols] = (y * s_ref[:, cols]).astype(o_ref.dtype)


def _pool(u, pool_w, pool_scale, bm=512):
    S = u.shape[0]
    return pl.pallas_call(
        _pool_body,
        grid=(S // bm,),
        in_specs=[pl.BlockSpec((bm, POOL_WIDTH), lambda i: (i, 0)),
                  pl.BlockSpec((POOL_HALO, POOL_WIDTH), lambda i: (jnp.maximum(i * (bm // POOL_HALO) - 1, 0), 0)),
                  pl.BlockSpec((len(POOL_WINDOWS), POOL_GROUP, POOL_GROUP), lambda i: (0, 0, 0)),
                  pl.BlockSpec((1, POOL_WIDTH), lambda i: (0, 0))],
        out_specs=pl.BlockSpec((bm, POOL_WIDTH), lambda i: (i, 0)),
        out_shape=jax.ShapeDtypeStruct((S, POOL_WIDTH), CDT),
        compiler_params=_cparams("parallel"),
        name="pool",
    )(u, u, pool_w.astype(CDT), pool_scale.reshape(1, POOL_WIDTH))


def _compress_body(x_ref, w1_ref, pe_ref, w2_ref, o_ref, *, pad_chunks):
    x = x_ref[...]
    w_lo = w1_ref[0]
    w_hi = w1_ref[1]
    a = _dot(x, w_lo)
    b = _dot(x, w_hi)
    pe = _dot(jnp.broadcast_to(pe_ref[0:1], (SUBLANES, pe_ref.shape[1])), w_lo) \
        + _dot(jnp.broadcast_to(pe_ref[1:2], (SUBLANES, pe_ref.shape[1])), w_hi)
    n = o_ref.shape[0]
    b_next = pltpu.roll(b, b.shape[0] - 1, axis=0)[pad_chunks:pad_chunks + n]
    pre = a[pad_chunks:pad_chunks + n] + b_next + pe[0:1, :]
    h = pre * jax.nn.sigmoid(pre)
    o_ref[...] = _dot(h.astype(CDT), w2_ref[...]).astype(o_ref.dtype)


def _compress(qkv, w1, pe, w2):
    rows = qkv.shape[0]
    chunks = rows // CMP_STRIDE
    pad_chunks = PAD_ROWS // CMP_STRIDE
    n_out = chunks - pad_chunks
    G = KV_GROUPS
    x = qkv[:, Q_WIDTH:Q_WIDTH + 2 * KV_WIDTH].reshape(chunks, CMP_STRIDE, 2 * G, HEAD_DIM)
    x = x.transpose(2, 0, 1, 3).reshape(2 * G, chunks, CMP_STRIDE * HEAD_DIM)
    width = CMP_STRIDE * HEAD_DIM
    body = functools.partial(_compress_body, pad_chunks=pad_chunks)
    return pl.pallas_call(
        body,
        grid=(2, G),
        in_specs=[pl.BlockSpec((None, chunks, width), lambda kv, g: (kv * G + g, 0, 0)),
                  pl.BlockSpec((None, 2, width, HEAD_DIM), lambda kv, g: (kv, 0, 0, 0)),
                  pl.BlockSpec((None, 2, width), lambda kv, g: (kv, 0, 0)),
                  pl.BlockSpec((None, HEAD_DIM, HEAD_DIM), lambda kv, g: (kv, 0, 0))],
        out_specs=pl.BlockSpec((None, None, n_out, HEAD_DIM), lambda kv, g: (kv, g, 0, 0)),
        out_shape=jax.ShapeDtypeStruct((2, G, n_out, HEAD_DIM), CDT),
        compiler_params=_cparams("parallel", "parallel"),
        name="compress",
    )(x, w1, pe, w2)


def _rel_bucket_np(dist):
    n = np.maximum(dist, 0)
    max_exact = REL_BUCKETS // 2
    nf = np.maximum(n, 1).astype(np.float32)
    large = max_exact + (np.log(nf / max_exact) / math.log(REL_MAX_DIST / max_exact)
                         * (REL_BUCKETS - max_exact)).astype(np.int32)
    large = np.minimum(large, REL_BUCKETS - 1)
    return np.where(n < max_exact, n, large).astype(np.int32)


def _bias_index_tiles():
    q = np.arange(QB)[:, None]
    dist = NEAR_KEYS - QB + q - np.arange(NEAR_KEYS)[None, :]
    near = np.where(dist >= 0, _rel_bucket_np(dist), -1)
    dist = WINDOW + q - np.arange(WIN_KEYS)[None, :]
    win = np.where((dist >= 0) & (dist < WINDOW), _rel_bucket_np(dist), -1)
    w = np.arange(LANES)[None, :]
    dist = q - CMP_STRIDE * (w - BAND_BACK) - (CMP_BLOCK - 1)
    band = np.where(dist >= 0, _rel_bucket_np(dist), -1)
    return np.concatenate([near, win, band], axis=1).astype(np.int32)


def _bias_body(tbl_ref, idx_ref, o_ref):
    h = pl.program_id(0)
    idx = idx_ref[...]
    far = tbl_ref[REL_BUCKETS - 1, h]
    val = jnp.full(idx.shape, NEG, jnp.float32)
    for b in range(REL_BUCKETS):
        val = jnp.where(idx == b, (tbl_ref[b, h] - far) * LOG2E, val)
    o_ref[...] = val


def _bias_tiles(rel_bias):
    idx = jnp.asarray(_bias_index_tiles())
    width = idx.shape[1]
    return pl.pallas_call(
        _bias_body,
        grid=(NSA_HEADS,),
        in_specs=[pl.BlockSpec(memory_space=pltpu.SMEM),
                  pl.BlockSpec((QB, width), lambda h: (0, 0))],
        out_specs=pl.BlockSpec((None, QB, width), lambda h: (h, 0, 0)),
        out_shape=jax.ShapeDtypeStruct((NSA_HEADS, QB, width), jnp.float32),
        compiler_params=_cparams("arbitrary"),
        name="bias_tiles",
    )(rel_bias, idx)


def _softmax_cols(s):
    m = jnp.max(s, axis=0, keepdims=True)
    p = jnp.exp2(s - m)
    l = jnp.sum(p, axis=0, keepdims=True)
    return p, jnp.where(m > 0.5 * NEG, 1.0 / l, 0.0)


def _split3(x):
    hi = x.astype(CDT)
    r = x - hi.astype(jnp.float32)
    mid = r.astype(CDT)
    lo = (r - mid.astype(jnp.float32)).astype(CDT)
    return hi, mid, lo


def _select_blocks(slc_t, i):
    n_slc = slc_t.shape[0]
    t = i * QB + lax.broadcasted_iota(jnp.int32, (1, QB), 1)
    j_int = lax.broadcasted_iota(jnp.int32, (n_slc, QB), 0)
    j_idx = j_int.astype(jnp.float32)
    cur = t // SLC_BLOCK
    forced = (j_int == 0) | ((cur - j_int >= 0) & (cur - j_int < N_LOCAL_FORCED))
    score = jnp.where(forced, 1e9, jnp.where(j_int > cur, -1e9, slc_t))
    picked = jnp.zeros(score.shape, jnp.bool_)
    for _ in range(min(N_SELECT, n_slc)):
        m = jnp.max(score, axis=0, keepdims=True)
        first = jnp.min(jnp.where(score == m, j_idx, float(n_slc)), axis=0, keepdims=True)
        hit = j_idx == first
        picked = picked | hit
        score = jnp.where(hit, -3e38, score)
    return picked


def _attn_body(q_ref, gate_ref, kc_ref, vct_ref, ks_ref, vst_ref, kw_ref, vwt_ref, blk_ref, tab_ref, wov_ref,
               o_ref, kaug_ref, kwaug_ref, sc_ref, pc_ref, sw_ref, pw_ref, s_ref, p_ref, acc_ref):
    i = pl.program_id(1)
    H = HEADS_PER_GROUP
    n_cmp = kc_ref.shape[0]
    n_slc = wov_ref.shape[0]
    lanes = H * QB
    head = lambda x, h: x[:, h * QB:(h + 1) * QB]

    @pl.when(i == 0)
    def _():
        kaug_ref[:, :HEAD_DIM] = ks_ref[...]
        kaug_ref[:, HEAD_DIM:] = blk_ref[...]
        kwaug_ref[:, :HEAD_DIM] = kw_ref[...]
        row = lax.broadcasted_iota(jnp.int32, (kwaug_ref.shape[0], LANES), 0)
        kwaug_ref[:, HEAD_DIM:] = jnp.where(row < PAD_ROWS, 1.0, 0.0).astype(CDT)

    qb = q_ref[...]
    q_t = jnp.concatenate([qb[:, h * HEAD_DIM:(h + 1) * HEAD_DIM].astype(jnp.float32).T.astype(CDT)
                           for h in range(H)], axis=1)

    def augment(extra):
        return jnp.concatenate([q_t, extra.astype(CDT)], axis=0)

    def values_t(ref, chunk0, n):
        return jnp.concatenate([ref[chunk0 + c] for c in range(n)], axis=1)

    grp = lax.broadcasted_iota(jnp.int32, (LANES, lanes), 0)
    sc_ref[...] = _dot(kc_ref[...], augment(jnp.where(grp > i, NEG, 0.0)))
    band0 = jnp.maximum(CMP_PER_TILE * i - BAND_BACK, 0)
    tab0 = band0 - (CMP_PER_TILE * i - BAND_BACK)
    band_rows = pl.ds(pl.multiple_of(band0, SUBLANES), BAND_ROWS)
    sc_ref[band_rows, :] += tab_ref[pl.ds(pl.multiple_of(NEAR_KEYS + WIN_KEYS + tab0, SUBLANES), BAND_ROWS), :]
    s_c = sc_ref[...]
    m_c = jnp.max(s_c, axis=0, keepdims=True)
    p_c = jnp.exp2(s_c - m_c)
    l_c = jnp.sum(p_c, axis=0, keepdims=True)
    pc_ref[...] = p_c.astype(CDT)
    seen = m_c > 0.5 * NEG
    o_cmp_t = _dot(vct_ref[...], pc_ref[...]) * jnp.where(seen, 1.0 / l_c, 0.0)
    shift_c = jnp.where(seen, m_c + jnp.log2(l_c), -NEG)
    imp_t = functools.reduce(lambda a, b: a + b,
                             [jnp.exp2(sc_ref[:, h * QB:(h + 1) * QB] - head(shift_c, h)) for h in range(H)])

    wov_t = wov_ref[...]
    slc_t = sum(_dot(wov_t, part) for part in _split3(imp_t))
    drop = jnp.where(_select_blocks(slc_t, i), 0.0, 1.0)
    j_row = lax.broadcasted_iota(jnp.int32, (n_slc, QB), 0)
    near_tile = i + 1 - NEAR_KEYS // QB
    drop_far = jnp.where(j_row < near_tile * (QB // SLC_BLOCK), drop, 1.0)
    tile_heads = lambda d: jnp.concatenate([d] * H, axis=1)

    win0 = pl.multiple_of(i * QB, QB)
    first_row = lax.broadcasted_iota(jnp.int32, (LANES, lanes), 0) == 0
    sw_ref[...] = (_dot(kwaug_ref[pl.ds(win0, WIN_KEYS), :], augment(jnp.where(first_row, NEG, 0.0)))
                   + tab_ref[NEAR_KEYS:NEAR_KEYS + WIN_KEYS])
    p_w, scale_w = _softmax_cols(sw_ref[...])
    pw_ref[...] = p_w.astype(CDT)
    o_win_t = _dot(values_t(vwt_ref, win0 // LANES, WIN_KEYS // LANES), pw_ref[...]) * scale_w

    near0 = pl.multiple_of(PAD_ROWS + near_tile * QB, QB)
    s_ref[...] = _dot(kaug_ref[pl.ds(near0, NEAR_KEYS), :], augment(tile_heads(drop))) + tab_ref[:NEAR_KEYS]
    p_ref[...] = jnp.zeros(p_ref.shape, CDT)
    acc_ref[...] = jnp.zeros(acc_ref.shape, jnp.float32)
    q_far = augment(tile_heads(drop_far))

    n_chunks = (kaug_ref.shape[0] - PAD_ROWS) // FAR_CHUNK
    n_far = (jnp.maximum(near_tile, 0) * QB + FAR_CHUNK - 1) // FAR_CHUNK
    far_lanes = FAR_CHUNK // LANES

    def far_step(it, carry):
        a_prev, m, l = carry
        v0 = jnp.where(it == 1, near0 // LANES, PAD_ROWS // LANES + jnp.clip(it - 2, 0, n_chunks - 1) * far_lanes)
        acc_ref[...] = a_prev * acc_ref[...] + _dot(values_t(vst_ref, v0, far_lanes), p_ref[...])
        s_prev = s_ref[...]
        m_new = jnp.maximum(m, jnp.max(s_prev, axis=0, keepdims=True))
        a = jnp.exp2(m - m_new)
        p = jnp.exp2(s_prev - m_new)
        l = a * l + jnp.sum(p, axis=0, keepdims=True)
        p_ref[...] = p.astype(CDT)
        k0 = pl.multiple_of(jnp.where(it < n_far, PAD_ROWS + it * FAR_CHUNK, 0), FAR_CHUNK)
        s_ref[...] = _dot(kaug_ref[pl.ds(k0, FAR_CHUNK), :], q_far)
        return a, m_new, l

    def far_trip(k, carry):
        for u in range(FAR_UNROLL):
            carry = far_step(FAR_UNROLL * k + u, carry)
        return carry

    zero = scale_w - scale_w
    trips = (n_far + 2 + FAR_UNROLL - 1) // FAR_UNROLL
    _, _, l_s = lax.fori_loop(0, trips, far_trip, (zero + 1.0, zero + 0.1 * NEG, zero))
    o_slc_t = acc_ref[...] * (1.0 / l_s)

    gates_t = gate_ref[...].T
    for h in range(H):
        g = lambda b: gates_t[h * N_BRANCH + b:h * N_BRANCH + b + 1, :]
        o_t = g(0) * head(o_cmp_t, h) + g(1) * head(o_slc_t, h) + g(2) * head(o_win_t, h)
        o_ref[:, h * HEAD_DIM:(h + 1) * HEAD_DIM] = o_t.T.astype(o_ref.dtype)


def _block_membership(rows, n_slc):
    key = np.arange(rows)[:, None] - PAD_ROWS
    j = np.arange(n_slc)[None, :]
    member = np.where(key >= 0, key // SLC_BLOCK == j, True)
    return jnp.asarray(np.where(member, NEG, 0.0), dtype=CDT)


def _overlap_weights(n_cmp, n_slc):
    r = SLC_BLOCK // CMP_STRIDE
    lead = -(-CMP_BLOCK // CMP_STRIDE) - 1
    w = np.zeros((n_cmp, n_slc), np.float32)
    for o in range(-lead, r):
        s0 = o * CMP_STRIDE
        ov = max(0, min(s0 + CMP_BLOCK, SLC_BLOCK) - max(s0, 0))
        for j in range(n_slc):
            n = r * j + o
            if ov > 0 and 0 <= n < n_cmp - 1:
                w[n, j] = ov / CMP_STRIDE
    return jnp.asarray(w.T, dtype=CDT)


def _tile_group_columns(n_cmp):
    n = np.arange(n_cmp)[:, None]
    return jnp.asarray(n // CMP_PER_TILE == np.arange(LANES)[None, :], dtype=CDT)


def _attention(qkv, gates, kvc, tabs):
    rows = qkv.shape[0]
    S = rows - PAD_ROWS
    n_cmp = kvc.shape[2]
    n_slc = S // SLC_BLOCK
    assert n_cmp // CMP_PER_TILE <= LANES
    first = Q_WIDTH // LANES
    G = KV_GROUPS
    lanes = HEADS_PER_GROUP * QB
    kv_spec = lambda which: pl.BlockSpec((rows, LANES), lambda g, i: (0, first + which * G + g))
    vt_spec = lambda which: pl.BlockSpec((None, None, rows // LANES, HEAD_DIM, LANES), lambda g, i: (which, g, 0, 0, 0))
    width = tabs.shape[2]
    tabs_t = tabs.reshape(G, HEADS_PER_GROUP, QB, width).transpose(0, 3, 1, 2).reshape(G, width, lanes)
    kc_aug = jnp.concatenate([kvc[0], jnp.broadcast_to(_tile_group_columns(n_cmp), (G, n_cmp, LANES))], axis=-1)
    vc_t = kvc[1].transpose(0, 2, 1)
    v_cols = jnp.stack([qkv[:, (first + w * G) * LANES:(first + (w + 1) * G) * LANES] for w in (3, 5)])
    v_t = v_cols.reshape(2, rows // LANES, LANES, G, HEAD_DIM).transpose(0, 3, 1, 4, 2)
    return pl.pallas_call(
        _attn_body,
        grid=(G, S // QB),
        in_specs=[pl.BlockSpec((QB, HEADS_PER_GROUP * HEAD_DIM), lambda g, i: (i + PAD_ROWS // QB, g)),
                  pl.BlockSpec((QB, LANES), lambda g, i: (i, g)),
                  pl.BlockSpec((None, n_cmp, HEAD_DIM + LANES), lambda g, i: (g, 0, 0)),
                  pl.BlockSpec((None, HEAD_DIM, n_cmp), lambda g, i: (g, 0, 0)),
                  kv_spec(2), vt_spec(0), kv_spec(4), vt_spec(1),
                  pl.BlockSpec((rows, n_slc), lambda g, i: (0, 0)),
                  pl.BlockSpec((None, width, lanes), lambda g, i: (g, 0, 0)),
                  pl.BlockSpec((n_slc, n_cmp), lambda g, i: (0, 0))],
        out_specs=pl.BlockSpec((QB, HEADS_PER_GROUP * HEAD_DIM), lambda g, i: (i, g)),
        out_shape=jax.ShapeDtypeStruct((S, Q_WIDTH), CDT),
        scratch_shapes=[pltpu.VMEM((rows, HEAD_DIM + n_slc), CDT),
                        pltpu.VMEM((rows, HEAD_DIM + LANES), CDT),
                        pltpu.VMEM((n_cmp, lanes), jnp.float32),
                        pltpu.VMEM((n_cmp, lanes), CDT),
                        pltpu.VMEM((WIN_KEYS, lanes), jnp.float32),
                        pltpu.VMEM((WIN_KEYS, lanes), CDT),
                        pltpu.VMEM((FAR_CHUNK, lanes), jnp.float32),
                        pltpu.VMEM((FAR_CHUNK, lanes), CDT),
                        pltpu.VMEM((HEAD_DIM, lanes), jnp.float32)],
        compiler_params=_cparams("parallel", "arbitrary"),
        name="nsa",
    )(qkv, gates, kc_aug, vc_t, qkv, v_t, qkv, v_t, _block_membership(rows, n_slc), tabs_t,
      _overlap_weights(n_cmp, n_slc))


def _outproj_body(a1_ref, a2_ref, w1_ref, w2_ref, x_ref, o_ref):
    o_ref[...] = x_ref[...] + _dot(a1_ref[...], w1_ref[...]) + _dot(a2_ref[...], w2_ref[...])


def _outproj(y_nsa, y_pool, w_out, x, bm=512, bn=1024):
    S, D = x.shape
    bn = min(bn, D)
    k1, k2 = y_nsa.shape[1], y_pool.shape[1]
    return pl.pallas_call(
        _outproj_body,
        grid=(S // bm, D // bn),
        in_specs=[pl.BlockSpec((bm, k1), lambda i, j: (i, 0)),
                  pl.BlockSpec((bm, k2), lambda i, j: (i, 0)),
                  pl.BlockSpec((k1, bn), lambda i, j: (0, j)),
                  pl.BlockSpec((k2, bn), lambda i, j: (k1 // k2, j)),
                  pl.BlockSpec((bm, bn), lambda i, j: (i, j))],
        out_specs=pl.BlockSpec((bm, bn), lambda i, j: (i, j)),
        out_shape=jax.ShapeDtypeStruct((S, D), jnp.float32),
        compiler_params=_cparams("parallel", "arbitrary"),
        name="outproj",
    )(y_nsa, y_pool, w_out, w_out, x)


CONV_HALO = 16
FFN_SUBTILES = 2
FFN_K_TILE = 512


def _ffn_up_body(a_ref, halo_ref, wa_ref, wb_ref, cwa_ref, cwb_ref, cba_ref, cbb_ref, o_ref, lhs_ref, *, nb):
    i = pl.program_id(0)
    j = pl.program_id(1)

    @pl.when(j == 0)
    def _():
        lhs_ref[:CONV_HALO] = jnp.where(i > 0, halo_ref[...], jnp.zeros_like(halo_ref))
        lhs_ref[CONV_HALO:] = a_ref[...]

    def conv(lhs, w, cw_ref, cb_ref):
        u = _dot(lhs, w)
        c = cb_ref[...] + cw_ref[CONV_WIDTH - 1:CONV_WIDTH, :] * u[CONV_HALO:]
        for k in range(1, CONV_WIDTH):
            c = c + cw_ref[CONV_WIDTH - 1 - k:CONV_WIDTH - k, :] * pltpu.roll(u, k, axis=0)[CONV_HALO:]
        return c

    @pl.when(j < nb)
    def _():
        wa = wa_ref[...].astype(CDT)
        wb = wb_ref[...].astype(CDT)
        sub = o_ref.shape[0] // FFN_SUBTILES
        for s in range(FFN_SUBTILES):
            lhs = lhs_ref[s * sub:s * sub + CONV_HALO + sub]
            ca = conv(lhs, wa, cwa_ref, cba_ref)
            cb = conv(lhs, wb, cwb_ref, cbb_ref)
            o_ref[s * sub:(s + 1) * sub] = (ca * jax.nn.sigmoid(ca) * cb).astype(o_ref.dtype)

    @pl.when(j >= nb)
    def _():
        o_ref[...] = jnp.zeros_like(o_ref)


def _ffn_up(hn, w_up, conv_w, conv_b, f_pad, bm=1024, bn=256):
    S, D = hn.shape
    F = w_up.shape[1] // 2
    nb = F // bn
    col = lambda j: jnp.minimum(j, nb - 1)
    return pl.pallas_call(
        functools.partial(_ffn_up_body, nb=nb),
        grid=(S // bm, f_pad // bn),
        in_specs=[pl.BlockSpec((bm, D), lambda i, j: (i, 0)),
                  pl.BlockSpec((CONV_HALO, D), lambda i, j: (jnp.maximum(i * (bm // CONV_HALO) - 1, 0), 0)),
                  pl.BlockSpec((D, bn), lambda i, j: (0, col(j))),
                  pl.BlockSpec((D, bn), lambda i, j: (0, nb + col(j))),
                  pl.BlockSpec((CONV_WIDTH, bn), lambda i, j: (0, col(j))),
                  pl.BlockSpec((CONV_WIDTH, bn), lambda i, j: (0, nb + col(j))),
                  pl.BlockSpec((1, bn), lambda i, j: (0, col(j))),
                  pl.BlockSpec((1, bn), lambda i, j: (0, nb + col(j)))],
        out_specs=pl.BlockSpec((bm, bn), lambda i, j: (i, j)),
        out_shape=jax.ShapeDtypeStruct((S, f_pad), CDT),
        scratch_shapes=[pltpu.VMEM((CONV_HALO + bm, D), CDT)],
        compiler_params=_cparams("parallel", "arbitrary"),
        name="ffn_up",
    )(hn, hn, w_up, w_up, conv_w, conv_w, conv_b, conv_b)


def _ffn_down_body(a_ref, w_ref, h_ref, g_ref, o_ref):
    k = pl.program_id(1)

    @pl.when(k == 0)
    def _():
        o_ref[...] = h_ref[...]

    o_ref[...] += _dot(a_ref[...], w_ref[...])

    @pl.when(k == pl.num_programs(1) - 1)
    def _():
        x = o_ref[...]
        y = x * lax.rsqrt(jnp.mean(x * x, axis=-1, keepdims=True) + RMS_EPS)
        o_ref[...] = y * g_ref[...]


def _ffn_down(act, w_down, h, g, bm=512, bk=FFN_K_TILE):
    S, F = act.shape
    D = h.shape[1]
    return pl.pallas_call(
        _ffn_down_body,
        grid=(S // bm, F // bk),
        in_specs=[pl.BlockSpec((bm, bk), lambda i, k: (i, k)),
                  pl.BlockSpec((bk, D), lambda i, k: (k, 0)),
                  pl.BlockSpec((bm, D), lambda i, k: (i, 0)),
                  pl.BlockSpec((1, D), lambda i, k: (0, 0))],
        out_specs=pl.BlockSpec((bm, D), lambda i, k: (i, 0)),
        out_shape=jax.ShapeDtypeStruct((S, D), jnp.float32),
        compiler_params=_cparams("parallel", "arbitrary"),
        name="ffn_down",
    )(act, w_down, h, g.reshape(1, D))


def _gate_weight(w_gate):
    D = w_gate.shape[0]
    per = HEADS_PER_GROUP * N_BRANCH
    w = w_gate.reshape(D, KV_GROUPS, per)
    w = jnp.pad(w, ((0, 0), (0, 0), (0, LANES - per)))
    return w.reshape(D, KV_GROUPS * LANES)


def _nsa_branch(hn, w_in, w_gate, cmp_w1_k, cmp_pe_k, cmp_w2_k, cmp_w1_v, cmp_pe_v, cmp_w2_v, rel_bias):
    qkv_cols = Q_WIDTH + 6 * KV_WIDTH
    w_gate = _gate_weight(w_gate)
    bm = 512
    qkv = _proj(hn, w_in, CDT, col0=POOL_WIDTH, n_cols=qkv_cols, pad_tiles=PAD_ROWS // bm, scaled_cols=Q_WIDTH,
                scale=HEAD_DIM ** -0.5 * LOG2E, bm=bm, name="proj_qkv")
    gates = _proj(hn, w_gate, jnp.float32, sigmoid=True, bm=bm, name="proj_gate")
    w1 = jnp.stack([cmp_w1_k, cmp_w1_v]).astype(CDT).reshape(2, 2, CMP_STRIDE * HEAD_DIM, HEAD_DIM)
    pe = jnp.stack([cmp_pe_k, cmp_pe_v]).astype(CDT).reshape(2, 2, CMP_STRIDE * HEAD_DIM)
    w2 = jnp.stack([cmp_w2_k, cmp_w2_v]).astype(CDT)
    kvc = _compress(qkv, w1, pe, w2)
    tabs = _bias_tiles(rel_bias)
    return _attention(qkv, gates, kvc, tabs)


def kernel(x, norm_mix_g, w_in, pool_w, pool_scale, cmp_pe_k, cmp_w1_k, cmp_w2_k, cmp_pe_v, cmp_w1_v, cmp_w2_v,
           rel_bias, w_out, norm_ffn_g, w_up, conv_w, conv_b, w_down, norm_final_g):
    B, S, D = x.shape
    assert B == 1 and w_in.shape[0] == 1, "single sequence, single layer"
    h = x.reshape(S, D)
    hn = _rmsnorm(h, norm_mix_g[0], CDT)
    n_main = POOL_WIDTH + Q_WIDTH + 6 * KV_WIDTH
    w_all = _cast_weight(w_in, 0, D, n_main)
    u_pool = _proj(hn, w_all, jnp.float32, n_cols=POOL_WIDTH, name="proj_pool")
    y_pool = _pool(u_pool, pool_w[0], pool_scale[0])
    y_nsa = _nsa_branch(hn, w_all, w_in[0][:, n_main:], cmp_w1_k[0], cmp_pe_k[0], cmp_w2_k[0],
                        cmp_w1_v[0], cmp_pe_v[0], cmp_w2_v[0], rel_bias)
    h = _outproj(y_nsa, y_pool, w_out[0].astype(CDT), h)
    hn = _rmsnorm(h, norm_ffn_g[0], CDT)
    F = w_down.shape[1]
    f_pad = -(-F // FFN_K_TILE) * FFN_K_TILE
    act = _ffn_up(hn, w_up[0], conv_w[0], conv_b[0].reshape(1, -1), f_pad)
    out = _ffn_down(act, _cast_weight(w_down, 0, f_pad, D), h, norm_final_g)
    return out.reshape(B, S, D)
```

```python
import functools
import math

import numpy as np
import jax
import jax.numpy as jnp
from jax import lax
from jax.experimental import pallas as pl
from jax.experimental.pallas import tpu as pltpu

POOL_WINDOWS = (2, 4, 8, 16)
POOL_GROUP = 256
POOL_WIDTH = POOL_GROUP * len(POOL_WINDOWS)
HEAD_DIM = 128
KV_GROUPS = 4
HEADS_PER_GROUP = 6
NSA_HEADS = KV_GROUPS * HEADS_PER_GROUP
Q_WIDTH = NSA_HEADS * HEAD_DIM
KV_WIDTH = KV_GROUPS * HEAD_DIM
CMP_BLOCK = 32
CMP_STRIDE = 16
SLC_BLOCK = 64
N_SELECT = 16
N_LOCAL_FORCED = 2
WINDOW = 512
N_BRANCH = 3
REL_BUCKETS = 32
REL_MAX_DIST = 128
CONV_WIDTH = 3
RMS_EPS = 1e-6
NEG = -1e30
LOG2E = math.log2(math.e)

LANES = 128
SUBLANES = 8
VMEM_LIMIT_BYTES = 56 * 1024 * 1024

CDT = jnp.bfloat16
QB = 128
PAD_ROWS = WINDOW
FAR_CHUNK = 512
NEAR_KEYS = FAR_CHUNK
FAR_UNROLL = 2
WIN_KEYS = WINDOW + QB
CMP_PER_TILE = QB // CMP_STRIDE
BAND_BACK = 16
BAND_ROWS = 24
BAND_TABLE = BAND_ROWS + BAND_BACK


def _cparams(*sem, flags=None):
    return pltpu.CompilerParams(dimension_semantics=sem, vmem_limit_bytes=VMEM_LIMIT_BYTES, flags=flags)


def _dot(a, b):
    return jnp.dot(a, b, preferred_element_type=jnp.float32)


def _dot_nt(a, b):
    return lax.dot_general(a, b, (((1,), (1,)), ((), ())), preferred_element_type=jnp.float32)


def _cast_body(x_ref, o_ref, *, rows_valid):
    x = x_ref[...]
    row = pl.program_id(0) * x.shape[0] + lax.broadcasted_iota(jnp.int32, x.shape, 0)
    o_ref[...] = jnp.where(row < rows_valid, x, 0.0).astype(o_ref.dtype)


def _cast_weight(w, layer, n_rows, n_cols, bm=256, bn=4096):
    rows = w.shape[1]
    bm, bn = min(bm, rows), min(bn, n_cols)
    assert n_rows % bm == 0 and n_cols % bn == 0 and (rows % bm == 0 or n_rows <= rows // bm * bm)
    last = rows // bm - 1
    return pl.pallas_call(
        functools.partial(_cast_body, rows_valid=rows),
        grid=(n_rows // bm, n_cols // bn),
        in_specs=[pl.BlockSpec((None, bm, bn), lambda i, j: (layer, jnp.minimum(i, last), j))],
        out_specs=pl.BlockSpec((bm, bn), lambda i, j: (i, j)),
        out_shape=jax.ShapeDtypeStruct((n_rows, n_cols), CDT),
        compiler_params=_cparams("parallel", "parallel"),
        name="cast_weight",
    )(w)


def _rmsnorm_body(x_ref, g_ref, o_ref):
    x = x_ref[...]
    y = x * lax.rsqrt(jnp.mean(x * x, axis=-1, keepdims=True) + RMS_EPS)
    o_ref[...] = (y * g_ref[...]).astype(o_ref.dtype)


def _rmsnorm(x, g, out_dtype, bm=256):
    S, D = x.shape
    return pl.pallas_call(
        _rmsnorm_body,
        grid=(S // bm,),
        in_specs=[pl.BlockSpec((bm, D), lambda i: (i, 0)), pl.BlockSpec((1, D), lambda i: (0, 0))],
        out_specs=pl.BlockSpec((bm, D), lambda i: (i, 0)),
        out_shape=jax.ShapeDtypeStruct((S, D), out_dtype),
        compiler_params=_cparams("parallel"),
        name="rmsnorm",
    )(x, g.reshape(1, D))


def _proj_body(a_ref, w_ref, o_ref, *, pad_tiles, n_scaled, scale, sigmoid):
    i = pl.program_id(0)
    j = pl.program_id(1)

    @pl.when(i < pad_tiles)
    def _():
        o_ref[...] = jnp.zeros_like(o_ref)

    @pl.when(i >= pad_tiles)
    def _():
        r = _dot_nt(a_ref[...], w_ref[...].astype(a_ref.dtype))
        if n_scaled:
            r = r * jnp.where(j < n_scaled, jnp.float32(scale), jnp.float32(1.0))
        if sigmoid:
            r = jax.nn.sigmoid(r)
        o_ref[...] = r.astype(o_ref.dtype)


def _proj(a, w_t, out_dtype, *, col0=0, n_cols=None, pad_tiles=0, scaled_cols=0, scale=1.0, sigmoid=False,
          bm=512, bn=1024, name="proj"):
    S, D = a.shape
    N = w_t.shape[0] if n_cols is None else n_cols
    bn = min(bn, N)
    assert scaled_cols % bn == 0 and col0 % bn == 0 and N % bn == 0
    n_scaled = scaled_cols // bn
    first = col0 // bn
    body = functools.partial(_proj_body, pad_tiles=pad_tiles, n_scaled=n_scaled, scale=scale, sigmoid=sigmoid)
    return pl.pallas_call(
        body,
        grid=(S // bm + pad_tiles, N // bn),
        in_specs=[pl.BlockSpec((bm, D), lambda i, j: (jnp.maximum(i - pad_tiles, 0), 0)),
                  pl.BlockSpec((bn, D), lambda i, j: (first + j, 0))],
        out_specs=pl.BlockSpec((bm, bn), lambda i, j: (i, j)),
        out_shape=jax.ShapeDtypeStruct((S + pad_tiles * bm, N), out_dtype),
        compiler_params=_cparams("parallel", "arbitrary"),
        name=name,
    )(a, w_t)


POOL_HALO = 16


def _pool_body(u_ref, halo_ref, w_ref, s_ref, o_ref):
    i = pl.program_id(0)
    bm = u_ref.shape[0]
    u = u_ref[...]
    halo = jnp.where(i > 0, halo_ref[...], 0.0)
    ext = jnp.concatenate([halo, u], axis=0)
    t = i * bm + lax.broadcasted_iota(jnp.int32, (bm, 1), 0)
    acc = ext
    sums = {}
    shift = 1
    while shift < POOL_WINDOWS[-1]:
        acc = acc + pltpu.roll(acc, shift, axis=0)
        shift *= 2
        sums[shift] = acc
    for gi, w in enumerate(POOL_WINDOWS):
        cols = slice(gi * POOL_GROUP, (gi + 1) * POOL_GROUP)
        cnt = jnp.minimum(t + 1, w).astype(jnp.float32)
        d = sums[w][POOL_HALO:, cols] / cnt - u[:, cols]
        y = _dot(d.astype(CDT), w_ref[gi])
        o_ref[:, cols] = (y * s_ref[:, cols]).astype(o_ref.dtype)


def _pool(u, pool_w, pool_scale, bm=512):
    S = u.shape[0]
    return pl.pallas_call(
        _pool_body,
        grid=(S // bm,),
        in_specs=[pl.BlockSpec((bm, POOL_WIDTH), lambda i: (i, 0)),
                  pl.BlockSpec((POOL_HALO, POOL_WIDTH), lambda i: (jnp.maximum(i * (bm // POOL_HALO) - 1, 0), 0)),
                  pl.BlockSpec((len(POOL_WINDOWS), POOL_GROUP, POOL_GROUP), lambda i: (0, 0, 0)),
                  pl.BlockSpec((1, POOL_WIDTH), lambda i: (0, 0))],
        out_specs=pl.BlockSpec((bm, POOL_WIDTH), lambda i: (i, 0)),
        out_shape=jax.ShapeDtypeStruct((S, POOL_WIDTH), CDT),
        compiler_params=_cparams("parallel"),
        name="pool",
    )(u, u, pool_w.astype(CDT), pool_scale.reshape(1, POOL_WIDTH))


def _compress_body(x_ref, w1_ref, pe_ref, w2_ref, o_ref, *, pad_chunks):
    x = x_ref[...]
    w_lo = w1_ref[0]
    w_hi = w1_ref[1]
    a = _dot(x, w_lo)
    b = _dot(x, w_hi)
    pe = _dot(jnp.broadcast_to(pe_ref[0:1], (SUBLANES, pe_ref.shape[1])), w_lo) \
        + _dot(jnp.broadcast_to(pe_ref[1:2], (SUBLANES, pe_ref.shape[1])), w_hi)
    n = o_ref.shape[0]
    b_next = pltpu.roll(b, b.shape[0] - 1, axis=0)[pad_chunks:pad_chunks + n]
    pre = a[pad_chunks:pad_chunks + n] + b_next + pe[0:1, :]
    h = pre * jax.nn.sigmoid(pre)
    o_ref[...] = _dot(h.astype(CDT), w2_ref[...]).astype(o_ref.dtype)


def _compress(qkv, w1, pe, w2):
    rows = qkv.shape[0]
    chunks = rows // CMP_STRIDE
    pad_chunks = PAD_ROWS // CMP_STRIDE
    n_out = chunks - pad_chunks
    G = KV_GROUPS
    x = qkv[:, Q_WIDTH:Q_WIDTH + 2 * KV_WIDTH].reshape(chunks, CMP_STRIDE, 2 * G, HEAD_DIM)
    x = x.transpose(2, 0, 1, 3).reshape(2 * G, chunks, CMP_STRIDE * HEAD_DIM)
    width = CMP_STRIDE * HEAD_DIM
    body = functools.partial(_compress_body, pad_chunks=pad_chunks)
    return pl.pallas_call(
        body,
        grid=(2, G),
        in_specs=[pl.BlockSpec((None, chunks, width), lambda kv, g: (kv * G + g, 0, 0)),
                  pl.BlockSpec((None, 2, width, HEAD_DIM), lambda kv, g: (kv, 0, 0, 0)),
                  pl.BlockSpec((None, 2, width), lambda kv, g: (kv, 0, 0)),
                  pl.BlockSpec((None, HEAD_DIM, HEAD_DIM), lambda kv, g: (kv, 0, 0))],
        out_specs=pl.BlockSpec((None, None, n_out, HEAD_DIM), lambda kv, g: (kv, g, 0, 0)),
        out_shape=jax.ShapeDtypeStruct((2, G, n_out, HEAD_DIM), CDT),
        compiler_params=_cparams("parallel", "parallel"),
        name="compress",
    )(x, w1, pe, w2)


def _rel_bucket_np(dist):
    n = np.maximum(dist, 0)
    max_exact = REL_BUCKETS // 2
    nf = np.maximum(n, 1).astype(np.float32)
    large = max_exact + (np.log(nf / max_exact) / math.log(REL_MAX_DIST / max_exact)
                         * (REL_BUCKETS - max_exact)).astype(np.int32)
    large = np.minimum(large, REL_BUCKETS - 1)
    return np.where(n < max_exact, n, large).astype(np.int32)


def _bias_index_tiles():
    q = np.arange(QB)[:, None]
    dist = NEAR_KEYS - QB + q - np.arange(NEAR_KEYS)[None, :]
    near = np.where(dist >= 0, _rel_bucket_np(dist), -1)
    dist = WINDOW + q - np.arange(WIN_KEYS)[None, :]
    win = np.where((dist >= 0) & (dist < WINDOW), _rel_bucket_np(dist), -1)
    w = np.arange(LANES)[None, :]
    dist = q - CMP_STRIDE * (w - BAND_BACK) - (CMP_BLOCK - 1)
    band = np.where(dist >= 0, _rel_bucket_np(dist), -1)
    return np.concatenate([near, win, band], axis=1).astype(np.int32)


def _bias_body(tbl_ref, idx_ref, o_ref):
    h = pl.program_id(0)
    idx = idx_ref[...]
    far = tbl_ref[REL_BUCKETS - 1, h]
    val = jnp.full(idx.shape, NEG, jnp.float32)
    for b in range(REL_BUCKETS):
        val = jnp.where(idx == b, (tbl_ref[b, h] - far) * LOG2E, val)
    o_ref[...] = val


def _bias_tiles(rel_bias):
    idx = jnp.asarray(_bias_index_tiles())
    width = idx.shape[1]
    return pl.pallas_call(
        _bias_body,
        grid=(NSA_HEADS,),
        in_specs=[pl.BlockSpec(memory_space=pltpu.SMEM),
                  pl.BlockSpec((QB, width), lambda h: (0, 0))],
        out_specs=pl.BlockSpec((None, QB, width), lambda h: (h, 0, 0)),
        out_shape=jax.ShapeDtypeStruct((NSA_HEADS, QB, width), jnp.float32),
        compiler_params=_cparams("arbitrary"),
        name="bias_tiles",
    )(rel_bias, idx)


def _softmax_cols(s):
    m = jnp.max(s, axis=0, keepdims=True)
    p = jnp.exp2(s - m)
    l = jnp.sum(p, axis=0, keepdims=True)
    return p, jnp.where(m > 0.5 * NEG, 1.0 / l, 0.0)


def _split3(x):
    hi = x.astype(CDT)
    r = x - hi.astype(jnp.float32)
    mid = r.astype(CDT)
    lo = (r - mid.astype(jnp.float32)).astype(CDT)
    return hi, mid, lo


def _select_blocks(slc_t, i):
    n_slc = slc_t.shape[0]
    t = i * QB + lax.broadcasted_iota(jnp.int32, (1, QB), 1)
    j_int = lax.broadcasted_iota(jnp.int32, (n_slc, QB), 0)
    j_idx = j_int.astype(jnp.float32)
    cur = t // SLC_BLOCK
    forced = (j_int == 0) | ((cur - j_int >= 0) & (cur - j_int < N_LOCAL_FORCED))
    score = jnp.where(forced, 1e9, jnp.where(j_int > cur, -1e9, slc_t))
    picked = jnp.zeros(score.shape, jnp.bool_)
    for _ in range(min(N_SELECT, n_slc)):
        m = jnp.max(score, axis=0, keepdims=True)
        first = jnp.min(jnp.where(score == m, j_idx, float(n_slc)), axis=0, keepdims=True)
        hit = j_idx == first
        picked = picked | hit
        score = jnp.where(hit, -3e38, score)
    return picked


def _attn_body(q_ref, gate_ref, kc_ref, vct_ref, ks_ref, vst_ref, kw_ref, vwt_ref, blk_ref, tab_ref, wov_ref,
               o_ref, kaug_ref, kwaug_ref, sc_ref, pc_ref, sw_ref, pw_ref, s_ref, p_ref, acc_ref):
    i = pl.program_id(1)
    H = HEADS_PER_GROUP
    n_cmp = kc_ref.shape[0]
    n_slc = wov_ref.shape[0]
    lanes = H * QB
    head = lambda x, h: x[:, h * QB:(h + 1) * QB]

    @pl.when(i == 0)
    def _():
        kaug_ref[:, :HEAD_DIM] = ks_ref[...]
        kaug_ref[:, HEAD_DIM:] = blk_ref[...]
        kwaug_ref[:, :HEAD_DIM] = kw_ref[...]
        row = lax.broadcasted_iota(jnp.int32, (kwaug_ref.shape[0], LANES), 0)
        kwaug_ref[:, HEAD_DIM:] = jnp.where(row < PAD_ROWS, 1.0, 0.0).astype(CDT)

    qb = q_ref[...]
    q_t = jnp.concatenate([qb[:, h * HEAD_DIM:(h + 1) * HEAD_DIM].astype(jnp.float32).T.astype(CDT)
                           for h in range(H)], axis=1)

    def augment(extra):
        return jnp.concatenate([q_t, extra.astype(CDT)], axis=0)

    def values_t(ref, chunk0, n):
        return jnp.concatenate([ref[chunk0 + c] for c in range(n)], axis=1)

    grp = lax.broadcasted_iota(jnp.int32, (LANES, lanes), 0)
    sc_ref[...] = _dot(kc_ref[...], augment(jnp.where(grp > i, NEG, 0.0)))
    band0 = jnp.maximum(CMP_PER_TILE * i - BAND_BACK, 0)
    tab0 = band0 - (CMP_PER_TILE * i - BAND_BACK)
    band_rows = pl.ds(pl.multiple_of(band0, SUBLANES), BAND_ROWS)
    sc_ref[band_rows, :] += tab_ref[pl.ds(pl.multiple_of(NEAR_KEYS + WIN_KEYS + tab0, SUBLANES), BAND_ROWS), :]
    s_c = sc_ref[...]
    m_c = jnp.max(s_c, axis=0, keepdims=True)
    p_c = jnp.exp2(s_c - m_c)
    l_c = jnp.sum(p_c, axis=0, keepdims=True)
    pc_ref[...] = p_c.astype(CDT)
    seen = m_c > 0.5 * NEG
    o_cmp_t = _dot(vct_ref[...], pc_ref[...]) * jnp.where(seen, 1.0 / l_c, 0.0)
    shift_c = jnp.where(seen, m_c + jnp.log2(l_c), -NEG)
    imp_t = functools.reduce(lambda a, b: a + b,
                             [jnp.exp2(sc_ref[:, h * QB:(h + 1) * QB] - head(shift_c, h)) for h in range(H)])

    wov_t = wov_ref[...]
    slc_t = sum(_dot(wov_t, part) for part in _split3(imp_t))
    drop = jnp.where(_select_blocks(slc_t, i), 0.0, 1.0)
    j_row = lax.broadcasted_iota(jnp.int32, (n_slc, QB), 0)
    near_tile = i + 1 - NEAR_KEYS // QB
    drop_far = jnp.where(j_row < near_tile * (QB // SLC_BLOCK), drop, 1.0)
    tile_heads = lambda d: jnp.concatenate([d] * H, axis=1)

    win0 = pl.multiple_of(i * QB, QB)
    first_row = lax.broadcasted_iota(jnp.int32, (LANES, lanes), 0) == 0
    sw_ref[...] = (_dot(kwaug_ref[pl.ds(win0, WIN_KEYS), :], augment(jnp.where(first_row, NEG, 0.0)))
                   + tab_ref[NEAR_KEYS:NEAR_KEYS + WIN_KEYS])
    p_w, scale_w = _softmax_cols(sw_ref[...])
    pw_ref[...] = p_w.astype(CDT)
    o_win_t = _dot(values_t(vwt_ref, win0 // LANES, WIN_KEYS // LANES), pw_ref[...]) * scale_w

    near0 = pl.multiple_of(PAD_ROWS + near_tile * QB, QB)
    s_ref[...] = _dot(kaug_ref[pl.ds(near0, NEAR_KEYS), :], augment(tile_heads(drop))) + tab_ref[:NEAR_KEYS]
    p_ref[...] = jnp.zeros(p_ref.shape, CDT)
    acc_ref[...] = jnp.zeros(acc_ref.shape, jnp.float32)
    q_far = augment(tile_heads(drop_far))

    n_chunks = (kaug_ref.shape[0] - PAD_ROWS) // FAR_CHUNK
    n_far = (jnp.maximum(near_tile, 0) * QB + FAR_CHUNK - 1) // FAR_CHUNK
    far_lanes = FAR_CHUNK // LANES

    def far_step(it, carry):
        a_prev, m, l = carry
        v0 = jnp.where(it == 1, near0 // LANES, PAD_ROWS // LANES + jnp.clip(it - 2, 0, n_chunks - 1) * far_lanes)
        acc_ref[...] = a_prev * acc_ref[...] + _dot(values_t(vst_ref, v0, far_lanes), p_ref[...])
        s_prev = s_ref[...]
        m_new = jnp.maximum(m, jnp.max(s_prev, axis=0, keepdims=True))
        a = jnp.exp2(m - m_new)
        p = jnp.exp2(s_prev - m_new)
        l = a * l + jnp.sum(p, axis=0, keepdims=True)
        p_ref[...] = p.astype(CDT)
        k0 = pl.multiple_of(jnp.where(it < n_far, PAD_ROWS + it * FAR_CHUNK, 0), FAR_CHUNK)
        s_ref[...] = _dot(kaug_ref[pl.ds(k0, FAR_CHUNK), :], q_far)
        return a, m_new, l

    def far_trip(k, carry):
        for u in range(FAR_UNROLL):
            carry = far_step(FAR_UNROLL * k + u, carry)
        return carry

    zero = scale_w - scale_w
    trips = (n_far + 2 + FAR_UNROLL - 1) // FAR_UNROLL
    _, _, l_s = lax.fori_loop(0, trips, far_trip, (zero + 1.0, zero + 0.1 * NEG, zero))
    o_slc_t = acc_ref[...] * (1.0 / l_s)

    gates_t = gate_ref[...].T
    for h in range(H):
        g = lambda b: gates_t[h * N_BRANCH + b:h * N_BRANCH + b + 1, :]
        o_t = g(0) * head(o_cmp_t, h) + g(1) * head(o_slc_t, h) + g(2) * head(o_win_t, h)
        o_ref[:, h * HEAD_DIM:(h + 1) * HEAD_DIM] = o_t.T.astype(o_ref.dtype)


def _block_membership(rows, n_slc):
    key = np.arange(rows)[:, None] - PAD_ROWS
    j = np.arange(n_slc)[None, :]
    member = np.where(key >= 0, key // SLC_BLOCK == j, True)
    return jnp.asarray(np.where(member, NEG, 0.0), dtype=CDT)


def _overlap_weights(n_cmp, n_slc):
    r = SLC_BLOCK // CMP_STRIDE
    lead = -(-CMP_BLOCK // CMP_STRIDE) - 1
    w = np.zeros((n_cmp, n_slc), np.float32)
    for o in range(-lead, r):
        s0 = o * CMP_STRIDE
        ov = max(0, min(s0 + CMP_BLOCK, SLC_BLOCK) - max(s0, 0))
        for j in range(n_slc):
            n = r * j + o
            if ov > 0 and 0 <= n < n_cmp - 1:
                w[n, j] = ov / CMP_STRIDE
    return jnp.asarray(w.T, dtype=CDT)


def _tile_group_columns(n_cmp):
    n = np.arange(n_cmp)[:, None]
    return jnp.asarray(n // CMP_PER_TILE == np.arange(LANES)[None, :], dtype=CDT)


def _attention(qkv, gates, kvc, tabs):
    rows = qkv.shape[0]
    S = rows - PAD_ROWS
    n_cmp = kvc.shape[2]
    n_slc = S // SLC_BLOCK
    assert n_cmp // CMP_PER_TILE <= LANES
    first = Q_WIDTH // LANES
    G = KV_GROUPS
    lanes = HEADS_PER_GROUP * QB
    kv_spec = lambda which: pl.BlockSpec((rows, LANES), lambda g, i: (0, first + which * G + g))
    vt_spec = lambda which: pl.BlockSpec((None, None, rows // LANES, HEAD_DIM, LANES), lambda g, i: (which, g, 0, 0, 0))
    width = tabs.shape[2]
    tabs_t = tabs.reshape(G, HEADS_PER_GROUP, QB, width).transpose(0, 3, 1, 2).reshape(G, width, lanes)
    kc_aug = jnp.concatenate([kvc[0], jnp.broadcast_to(_tile_group_columns(n_cmp), (G, n_cmp, LANES))], axis=-1)
    vc_t = kvc[1].transpose(0, 2, 1)
    v_cols = jnp.stack([qkv[:, (first + w * G) * LANES:(first + (w + 1) * G) * LANES] for w in (3, 5)])
    v_t = v_cols.reshape(2, rows // LANES, LANES, G, HEAD_DIM).transpose(0, 3, 1, 4, 2)
    return pl.pallas_call(
        _attn_body,
        grid=(G, S // QB),
        in_specs=[pl.BlockSpec((QB, HEADS_PER_GROUP * HEAD_DIM), lambda g, i: (i + PAD_ROWS // QB, g)),
                  pl.BlockSpec((QB, LANES), lambda g, i: (i, g)),
                  pl.BlockSpec((None, n_cmp, HEAD_DIM + LANES), lambda g, i: (g, 0, 0)),
                  pl.BlockSpec((None, HEAD_DIM, n_cmp), lambda g, i: (g, 0, 0)),
                  kv_spec(2), vt_spec(0), kv_spec(4), vt_spec(1),
                  pl.BlockSpec((rows, n_slc), lambda g, i: (0, 0)),
                  pl.BlockSpec((None, width, lanes), lambda g, i: (g, 0, 0)),
                  pl.BlockSpec((n_slc, n_cmp), lambda g, i: (0, 0))],
        out_specs=pl.BlockSpec((QB, HEADS_PER_GROUP * HEAD_DIM), lambda g, i: (i, g)),
        out_shape=jax.ShapeDtypeStruct((S, Q_WIDTH), CDT),
        scratch_shapes=[pltpu.VMEM((rows, HEAD_DIM + n_slc), CDT),
                        pltpu.VMEM((rows, HEAD_DIM + LANES), CDT),
                        pltpu.VMEM((n_cmp, lanes), jnp.float32),
                        pltpu.VMEM((n_cmp, lanes), CDT),
                        pltpu.VMEM((WIN_KEYS, lanes), jnp.float32),
                        pltpu.VMEM((WIN_KEYS, lanes), CDT),
                        pltpu.VMEM((FAR_CHUNK, lanes), jnp.float32),
                        pltpu.VMEM((FAR_CHUNK, lanes), CDT),
                        pltpu.VMEM((HEAD_DIM, lanes), jnp.float32)],
        compiler_params=_cparams("parallel", "arbitrary"),
        name="nsa",
    )(qkv, gates, kc_aug, vc_t, qkv, v_t, qkv, v_t, _block_membership(rows, n_slc), tabs_t,
      _overlap_weights(n_cmp, n_slc))


def _outproj_body(a1_ref, a2_ref, w1_ref, w2_ref, x_ref, o_ref):
    o_ref[...] = x_ref[...] + _dot(a1_ref[...], w1_ref[...]) + _dot(a2_ref[...], w2_ref[...])


def _outproj(y_nsa, y_pool, w_out, x, bm=512, bn=1024):
    S, D = x.shape
    bn = min(bn, D)
    k1, k2 = y_nsa.shape[1], y_pool.shape[1]
    return pl.pallas_call(
        _outproj_body,
        grid=(S // bm, D // bn),
        in_specs=[pl.BlockSpec((bm, k1), lambda i, j: (i, 0)),
                  pl.BlockSpec((bm, k2), lambda i, j: (i, 0)),
                  pl.BlockSpec((k1, bn), lambda i, j: (0, j)),
                  pl.BlockSpec((k2, bn), lambda i, j: (k1 // k2, j)),
                  pl.BlockSpec((bm, bn), lambda i, j: (i, j))],
        out_specs=pl.BlockSpec((bm, bn), lambda i, j: (i, j)),
        out_shape=jax.ShapeDtypeStruct((S, D), jnp.float32),
        compiler_params=_cparams("parallel", "arbitrary"),
        name="outproj",
    )(y_nsa, y_pool, w_out, w_out, x)


CONV_HALO = 16
FFN_SUBTILES = 2
FFN_K_TILE = 512


def _ffn_up_body(a_ref, halo_ref, wa_ref, wb_ref, cwa_ref, cwb_ref, cba_ref, cbb_ref, o_ref, lhs_ref, *, nb):
    i = pl.program_id(0)
    j = pl.program_id(1)

    @pl.when(j == 0)
    def _():
        lhs_ref[:CONV_HALO] = jnp.where(i > 0, halo_ref[...], jnp.zeros_like(halo_ref))
        lhs_ref[CONV_HALO:] = a_ref[...]

    def conv(lhs, w, cw_ref, cb_ref):
        u = _dot(lhs, w)
        c = cb_ref[...] + cw_ref[CONV_WIDTH - 1:CONV_WIDTH, :] * u[CONV_HALO:]
        for k in range(1, CONV_WIDTH):
            c = c + cw_ref[CONV_WIDTH - 1 - k:CONV_WIDTH - k, :] * pltpu.roll(u, k, axis=0)[CONV_HALO:]
        return c

    @pl.when(j < nb)
    def _():
        wa = wa_ref[...].astype(CDT)
        wb = wb_ref[...].astype(CDT)
        sub = o_ref.shape[0] // FFN_SUBTILES
        for s in range(FFN_SUBTILES):
            lhs = lhs_ref[s * sub:s * sub + CONV_HALO + sub]
            ca = conv(lhs, wa, cwa_ref, cba_ref)
            cb = conv(lhs, wb, cwb_ref, cbb_ref)
            o_ref[s * sub:(s + 1) * sub] = (ca * jax.nn.sigmoid(ca) * cb).astype(o_ref.dtype)

    @pl.when(j >= nb)
    def _():
        o_ref[...] = jnp.zeros_like(o_ref)


def _ffn_up(hn, w_up, conv_w, conv_b, f_pad, bm=1024, bn=256):
    S, D = hn.shape
    F = w_up.shape[1] // 2
    nb = F // bn
    col = lambda j: jnp.minimum(j, nb - 1)
    return pl.pallas_call(
        functools.partial(_ffn_up_body, nb=nb),
        grid=(S // bm, f_pad // bn),
        in_specs=[pl.BlockSpec((bm, D), lambda i, j: (i, 0)),
                  pl.BlockSpec((CONV_HALO, D), lambda i, j: (jnp.maximum(i * (bm // CONV_HALO) - 1, 0), 0)),
                  pl.BlockSpec((D, bn), lambda i, j: (0, col(j))),
                  pl.BlockSpec((D, bn), lambda i, j: (0, nb + col(j))),
                  pl.BlockSpec((CONV_WIDTH, bn), lambda i, j: (0, col(j))),
                  pl.BlockSpec((CONV_WIDTH, bn), lambda i, j: (0, nb + col(j))),
                  pl.BlockSpec((1, bn), lambda i, j: (0, col(j))),
                  pl.BlockSpec((1, bn), lambda i, j: (0, nb + col(j)))],
        out_specs=pl.BlockSpec((bm, bn), lambda i, j: (i, j)),
        out_shape=jax.ShapeDtypeStruct((S, f_pad), CDT),
        scratch_shapes=[pltpu.VMEM((CONV_HALO + bm, D), CDT)],
        compiler_params=_cparams("parallel", "arbitrary"),
        name="ffn_up",
    )(hn, hn, w_up, w_up, conv_w, conv_w, conv_b, conv_b)


def _ffn_down_body(a_ref, w_ref, h_ref, g_ref, o_ref):
    k = pl.program_id(1)

    @pl.when(k == 0)
    def _():
        o_ref[...] = h_ref[...]

    o_ref[...] += _dot(a_ref[...], w_ref[...])

    @pl.when(k == pl.num_programs(1) - 1)
    def _():
        x = o_ref[...]
        y = x * lax.rsqrt(jnp.mean(x * x, axis=-1, keepdims=True) + RMS_EPS)
        o_ref[...] = y * g_ref[...]


def _ffn_down(act, w_down, h, g, bm=512, bk=FFN_K_TILE):
    S, F = act.shape
    D = h.shape[1]
    return pl.pallas_call(
        _ffn_down_body,
        grid=(S // bm, F // bk),
        in_specs=[pl.BlockSpec((bm, bk), lambda i, k: (i, k)),
                  pl.BlockSpec((bk, D), lambda i, k: (k, 0)),
                  pl.BlockSpec((bm, D), lambda i, k: (i, 0)),
                  pl.BlockSpec((1, D), lambda i, k: (0, 0))],
        out_specs=pl.BlockSpec((bm, D), lambda i, k: (i, 0)),
        out_shape=jax.ShapeDtypeStruct((S, D), jnp.float32),
        compiler_params=_cparams("parallel", "arbitrary"),
        name="ffn_down",
    )(act, w_down, h, g.reshape(1, D))


def _gate_weight(w_gate_t):
    D = w_gate_t.shape[1]
    per = HEADS_PER_GROUP * N_BRANCH
    w = w_gate_t.reshape(KV_GROUPS, per, D)
    w = jnp.pad(w, ((0, 0), (0, LANES - per), (0, 0)))
    return w.reshape(KV_GROUPS * LANES, D)


def _nsa_branch(hn, w_in, w_gate, cmp_w1_k, cmp_pe_k, cmp_w2_k, cmp_w1_v, cmp_pe_v, cmp_w2_v, rel_bias):
    qkv_cols = Q_WIDTH + 6 * KV_WIDTH
    w_gate = _gate_weight(w_gate)
    bm = 512
    qkv = _proj(hn, w_in, CDT, col0=POOL_WIDTH, n_cols=qkv_cols, pad_tiles=PAD_ROWS // bm, scaled_cols=Q_WIDTH,
                scale=HEAD_DIM ** -0.5 * LOG2E, bm=bm, name="proj_qkv")
    gates = _proj(hn, w_gate, jnp.float32, sigmoid=True, bm=bm, name="proj_gate")
    w1 = jnp.stack([cmp_w1_k, cmp_w1_v]).astype(CDT).reshape(2, 2, CMP_STRIDE * HEAD_DIM, HEAD_DIM)
    pe = jnp.stack([cmp_pe_k, cmp_pe_v]).astype(CDT).reshape(2, 2, CMP_STRIDE * HEAD_DIM)
    w2 = jnp.stack([cmp_w2_k, cmp_w2_v]).astype(CDT)
    kvc = _compress(qkv, w1, pe, w2)
    tabs = _bias_tiles(rel_bias)
    return _attention(qkv, gates, kvc, tabs)


def kernel(x, norm_mix_g, w_in, pool_w, pool_scale, cmp_pe_k, cmp_w1_k, cmp_w2_k, cmp_pe_v, cmp_w1_v, cmp_w2_v,
           rel_bias, w_out, norm_ffn_g, w_up, conv_w, conv_b, w_down, norm_final_g):
    B, S, D = x.shape
    assert B == 1 and w_in.shape[0] == 1, "single sequence, single layer"
    h = x.reshape(S, D)
    hn = _rmsnorm(h, norm_mix_g[0], CDT)
    n_main = POOL_WIDTH + Q_WIDTH + 6 * KV_WIDTH
    w_in_t = w_in.transpose(0, 2, 1)
    w_all = _cast_weight(w_in_t, 0, n_main, D, bm=512)
    u_pool = _proj(hn, w_all, jnp.float32, n_cols=POOL_WIDTH, name="proj_pool")
    y_pool = _pool(u_pool, pool_w[0], pool_scale[0])
    y_nsa = _nsa_branch(hn, w_all, w_in_t[0, n_main:], cmp_w1_k[0], cmp_pe_k[0], cmp_w2_k[0],
                        cmp_w1_v[0], cmp_pe_v[0], cmp_w2_v[0], rel_bias)
    h = _outproj(y_nsa, y_pool, w_out[0].astype(CDT), h)
    hn = _rmsnorm(h, norm_ffn_g[0], CDT)
    F = w_down.shape[1]
    f_pad = -(-F // FFN_K_TILE) * FFN_K_TILE
    act = _ffn_up(hn, w_up[0], conv_w[0], conv_b[0].reshape(1, -1), f_pad)
    out = _ffn_down(act, _cast_weight(w_down, 0, f_pad, D), h, norm_final_g)
    return out.reshape(B, S, D)
```

```python
import functools
import math

import numpy as np
import jax
import jax.numpy as jnp
from jax import lax
from jax.experimental import pallas as pl
from jax.experimental.pallas import tpu as pltpu

POOL_WINDOWS = (2, 4, 8, 16)
POOL_GROUP = 256
POOL_WIDTH = POOL_GROUP * len(POOL_WINDOWS)
HEAD_DIM = 128
KV_GROUPS = 4
HEADS_PER_GROUP = 6
NSA_HEADS = KV_GROUPS * HEADS_PER_GROUP
Q_WIDTH = NSA_HEADS * HEAD_DIM
KV_WIDTH = KV_GROUPS * HEAD_DIM
CMP_BLOCK = 32
CMP_STRIDE = 16
SLC_BLOCK = 64
N_SELECT = 16
N_LOCAL_FORCED = 2
WINDOW = 512
N_BRANCH = 3
REL_BUCKETS = 32
REL_MAX_DIST = 128
CONV_WIDTH = 3
RMS_EPS = 1e-6
NEG = -1e30
LOG2E = math.log2(math.e)

LANES = 128
SUBLANES = 8
VMEM_LIMIT_BYTES = 56 * 1024 * 1024

CDT = jnp.bfloat16
QB = 128
PAD_ROWS = WINDOW
FAR_CHUNK = 512
NEAR_KEYS = FAR_CHUNK
FAR_UNROLL = 2
WIN_KEYS = WINDOW + QB
CMP_PER_TILE = QB // CMP_STRIDE
BAND_BACK = 16
BAND_ROWS = 24
BAND_TABLE = BAND_ROWS + BAND_BACK


def _cparams(*sem, flags=None):
    return pltpu.CompilerParams(dimension_semantics=sem, vmem_limit_bytes=VMEM_LIMIT_BYTES, flags=flags)


def _dot(a, b):
    return jnp.dot(a, b, preferred_element_type=jnp.float32)


def _dot_nt(a, b):
    return lax.dot_general(a, b, (((1,), (1,)), ((), ())), preferred_element_type=jnp.float32)


def _cast_body(x_ref, o_ref, *, rows_valid):
    x = x_ref[...]
    row = pl.program_id(0) * x.shape[0] + lax.broadcasted_iota(jnp.int32, x.shape, 0)
    o_ref[...] = jnp.where(row < rows_valid, x, 0.0).astype(o_ref.dtype)


def _cast_weight(w, layer, n_rows, n_cols, bm=256, bn=4096):
    rows = w.shape[1]
    bm, bn = min(bm, rows), min(bn, n_cols)
    assert n_rows % bm == 0 and n_cols % bn == 0 and (rows % bm == 0 or n_rows <= rows // bm * bm)
    last = rows // bm - 1
    return pl.pallas_call(
        functools.partial(_cast_body, rows_valid=rows),
        grid=(n_rows // bm, n_cols // bn),
        in_specs=[pl.BlockSpec((None, bm, bn), lambda i, j: (layer, jnp.minimum(i, last), j))],
        out_specs=pl.BlockSpec((bm, bn), lambda i, j: (i, j)),
        out_shape=jax.ShapeDtypeStruct((n_rows, n_cols), CDT),
        compiler_params=_cparams("parallel", "parallel"),
        name="cast_weight",
    )(w)


def _rmsnorm_body(x_ref, g_ref, o_ref):
    x = x_ref[...]
    y = x * lax.rsqrt(jnp.mean(x * x, axis=-1, keepdims=True) + RMS_EPS)
    o_ref[...] = (y * g_ref[...]).astype(o_ref.dtype)


def _rmsnorm(x, g, out_dtype, bm=256):
    S, D = x.shape
    return pl.pallas_call(
        _rmsnorm_body,
        grid=(S // bm,),
        in_specs=[pl.BlockSpec((bm, D), lambda i: (i, 0)), pl.BlockSpec((1, D), lambda i: (0, 0))],
        out_specs=pl.BlockSpec((bm, D), lambda i: (i, 0)),
        out_shape=jax.ShapeDtypeStruct((S, D), out_dtype),
        compiler_params=_cparams("parallel"),
        name="rmsnorm",
    )(x, g.reshape(1, D))


def _proj_body(a_ref, w_ref, o_ref, *, pad_tiles, n_scaled, scale, sigmoid):
    i = pl.program_id(0)
    j = pl.program_id(1)

    @pl.when(i < pad_tiles)
    def _():
        o_ref[...] = jnp.zeros_like(o_ref)

    @pl.when(i >= pad_tiles)
    def _():
        r = _dot_nt(a_ref[...], w_ref[...].astype(a_ref.dtype))
        if n_scaled:
            r = r * jnp.where(j < n_scaled, jnp.float32(scale), jnp.float32(1.0))
        if sigmoid:
            r = jax.nn.sigmoid(r)
        o_ref[...] = r.astype(o_ref.dtype)


def _proj(a, w_t, out_dtype, *, col0=0, n_cols=None, pad_tiles=0, scaled_cols=0, scale=1.0, sigmoid=False,
          bm=512, bn=1024, name="proj"):
    S, D = a.shape
    N = w_t.shape[0] if n_cols is None else n_cols
    bn = min(bn, N)
    assert scaled_cols % bn == 0 and col0 % bn == 0 and N % bn == 0
    n_scaled = scaled_cols // bn
    first = col0 // bn
    body = functools.partial(_proj_body, pad_tiles=pad_tiles, n_scaled=n_scaled, scale=scale, sigmoid=sigmoid)
    return pl.pallas_call(
        body,
        grid=(S // bm + pad_tiles, N // bn),
        in_specs=[pl.BlockSpec((bm, D), lambda i, j: (jnp.maximum(i - pad_tiles, 0), 0)),
                  pl.BlockSpec((bn, D), lambda i, j: (first + j, 0))],
        out_specs=pl.BlockSpec((bm, bn), lambda i, j: (i, j)),
        out_shape=jax.ShapeDtypeStruct((S + pad_tiles * bm, N), out_dtype),
        compiler_params=_cparams("parallel", "arbitrary"),
        name=name,
    )(a, w_t)


POOL_HALO = 16


def _pool_body(u_ref, halo_ref, w_ref, s_ref, o_ref):
    i = pl.program_id(0)
    bm = u_ref.shape[0]
    u = u_ref[...]
    halo = jnp.where(i > 0, halo_ref[...], 0.0)
    ext = jnp.concatenate([halo, u], axis=0)
    t = i * bm + lax.broadcasted_iota(jnp.int32, (bm, 1), 0)
    acc = ext
    sums = {}
    shift = 1
    while shift < POOL_WINDOWS[-1]:
        acc = acc + pltpu.roll(acc, shift, axis=0)
        shift *= 2
        sums[shift] = acc
    for gi, w in enumerate(POOL_WINDOWS):
        cols = slice(gi * POOL_GROUP, (gi + 1) * POOL_GROUP)
        cnt = jnp.minimum(t + 1, w).astype(jnp.float32)
        d = sums[w][POOL_HALO:, cols] / cnt - u[:, cols]
        y = _dot(d.astype(CDT), w_ref[gi])
        o_ref[:, cols] = (y * s_ref[:, cols]).astype(o_ref.dtype)


def _pool(u, pool_w, pool_scale, bm=512):
    S = u.shape[0]
    return pl.pallas_call(
        _pool_body,
        grid=(S // bm,),
        in_specs=[pl.BlockSpec((bm, POOL_WIDTH), lambda i: (i, 0)),
                  pl.BlockSpec((POOL_HALO, POOL_WIDTH), lambda i: (jnp.maximum(i * (bm // POOL_HALO) - 1, 0), 0)),
                  pl.BlockSpec((len(POOL_WINDOWS), POOL_GROUP, POOL_GROUP), lambda i: (0, 0, 0)),
                  pl.BlockSpec((1, POOL_WIDTH), lambda i: (0, 0))],
        out_specs=pl.BlockSpec((bm, POOL_WIDTH), lambda i: (i, 0)),
        out_shape=jax.ShapeDtypeStruct((S, POOL_WIDTH), CDT),
        compiler_params=_cparams("parallel"),
        name="pool",
    )(u, u, pool_w.astype(CDT), pool_scale.reshape(1, POOL_WIDTH))


def _compress_body(x_ref, w1_ref, pe_ref, w2_ref, o_ref, *, pad_chunks):
    x = x_ref[...]
    w_lo = w1_ref[0]
    w_hi = w1_ref[1]
    a = _dot(x, w_lo)
    b = _dot(x, w_hi)
    pe = _dot(jnp.broadcast_to(pe_ref[0:1], (SUBLANES, pe_ref.shape[1])), w_lo) \
        + _dot(jnp.broadcast_to(pe_ref[1:2], (SUBLANES, pe_ref.shape[1])), w_hi)
    n = o_ref.shape[0]
    b_next = pltpu.roll(b, b.shape[0] - 1, axis=0)[pad_chunks:pad_chunks + n]
    pre = a[pad_chunks:pad_chunks + n] + b_next + pe[0:1, :]
    h = pre * jax.nn.sigmoid(pre)
    o_ref[...] = _dot(h.astype(CDT), w2_ref[...]).astype(o_ref.dtype)


def _compress(qkv, w1, pe, w2):
    rows = qkv.shape[0]
    chunks = rows // CMP_STRIDE
    pad_chunks = PAD_ROWS // CMP_STRIDE
    n_out = chunks - pad_chunks
    G = KV_GROUPS
    x = qkv[:, Q_WIDTH:Q_WIDTH + 2 * KV_WIDTH].reshape(chunks, CMP_STRIDE, 2 * G, HEAD_DIM)
    x = x.transpose(2, 0, 1, 3).reshape(2 * G, chunks, CMP_STRIDE * HEAD_DIM)
    width = CMP_STRIDE * HEAD_DIM
    body = functools.partial(_compress_body, pad_chunks=pad_chunks)
    return pl.pallas_call(
        body,
        grid=(2, G),
        in_specs=[pl.BlockSpec((None, chunks, width), lambda kv, g: (kv * G + g, 0, 0)),
                  pl.BlockSpec((None, 2, width, HEAD_DIM), lambda kv, g: (kv, 0, 0, 0)),
                  pl.BlockSpec((None, 2, width), lambda kv, g: (kv, 0, 0)),
                  pl.BlockSpec((None, HEAD_DIM, HEAD_DIM), lambda kv, g: (kv, 0, 0))],
        out_specs=pl.BlockSpec((None, None, n_out, HEAD_DIM), lambda kv, g: (kv, g, 0, 0)),
        out_shape=jax.ShapeDtypeStruct((2, G, n_out, HEAD_DIM), CDT),
        compiler_params=_cparams("parallel", "parallel"),
        name="compress",
    )(x, w1, pe, w2)


def _rel_bucket_np(dist):
    n = np.maximum(dist, 0)
    max_exact = REL_BUCKETS // 2
    nf = np.maximum(n, 1).astype(np.float32)
    large = max_exact + (np.log(nf / max_exact) / math.log(REL_MAX_DIST / max_exact)
                         * (REL_BUCKETS - max_exact)).astype(np.int32)
    large = np.minimum(large, REL_BUCKETS - 1)
    return np.where(n < max_exact, n, large).astype(np.int32)


def _bias_index_tiles():
    assert BAND_TABLE <= LANES and BAND_BACK % SUBLANES == 0 and BAND_ROWS % SUBLANES == 0
    q = np.arange(QB)[:, None]
    dist = NEAR_KEYS - QB + q - np.arange(NEAR_KEYS)[None, :]
    near = np.where(dist >= 0, _rel_bucket_np(dist), -1)
    dist = WINDOW + q - np.arange(WIN_KEYS)[None, :]
    win = np.where((dist >= 0) & (dist < WINDOW), _rel_bucket_np(dist), -1)
    w = np.arange(LANES)[None, :]
    dist = q - CMP_STRIDE * (w - BAND_BACK) - (CMP_BLOCK - 1)
    band = np.where(dist >= 0, _rel_bucket_np(dist), -1)
    return np.concatenate([near, win, band], axis=1).T.astype(np.int32)


def _bias_body(tbl_ref, idx_ref, o_ref):
    h = pl.program_id(0)
    idx = idx_ref[...]
    far = tbl_ref[REL_BUCKETS - 1, h]
    val = jnp.full(idx.shape, NEG, jnp.float32)
    for b in range(REL_BUCKETS):
        val = jnp.where(idx == b, (tbl_ref[b, h] - far) * LOG2E, val)
    o_ref[...] = val


def _bias_tiles(rel_bias):
    idx = jnp.asarray(_bias_index_tiles())
    width = idx.shape[0]
    H = HEADS_PER_GROUP
    return pl.pallas_call(
        _bias_body,
        grid=(NSA_HEADS,),
        in_specs=[pl.BlockSpec(memory_space=pltpu.SMEM),
                  pl.BlockSpec((width, QB), lambda h: (0, 0))],
        out_specs=pl.BlockSpec((None, width, QB), lambda h: (h // H, 0, h % H)),
        out_shape=jax.ShapeDtypeStruct((KV_GROUPS, width, H * QB), jnp.float32),
        compiler_params=_cparams("arbitrary"),
        name="bias_tiles",
    )(rel_bias, idx)


def _softmax_cols(s):
    m = jnp.max(s, axis=0, keepdims=True)
    p = jnp.exp2(s - m)
    l = jnp.sum(p, axis=0, keepdims=True)
    return p, jnp.where(m > 0.5 * NEG, 1.0 / l, 0.0)


def _split3(x):
    hi = x.astype(CDT)
    r = x - hi.astype(jnp.float32)
    mid = r.astype(CDT)
    lo = (r - mid.astype(jnp.float32)).astype(CDT)
    return hi, mid, lo


def _select_blocks(slc_t, i):
    n_slc = slc_t.shape[0]
    t = i * QB + lax.broadcasted_iota(jnp.int32, (1, QB), 1)
    j_int = lax.broadcasted_iota(jnp.int32, (n_slc, QB), 0)
    j_idx = j_int.astype(jnp.float32)
    cur = t // SLC_BLOCK
    forced = (j_int == 0) | ((cur - j_int >= 0) & (cur - j_int < N_LOCAL_FORCED))
    score = jnp.where(forced, 1e9, jnp.where(j_int > cur, -1e9, slc_t))
    picked = jnp.zeros(score.shape, jnp.bool_)
    for _ in range(min(N_SELECT, n_slc)):
        m = jnp.max(score, axis=0, keepdims=True)
        first = jnp.min(jnp.where(score == m, j_idx, float(n_slc)), axis=0, keepdims=True)
        hit = j_idx == first
        picked = picked | hit
        score = jnp.where(hit, -3e38, score)
    return picked


def _attn_body(q_ref, gate_ref, kc_ref, vct_ref, ks_ref, vst_ref, kw_ref, vwt_ref, blk_ref, tab_ref, wov_ref,
               o_ref, kaug_ref, kwaug_ref, sc_ref, pc_ref, sw_ref, pw_ref, s_ref, p_ref, acc_ref):
    i = pl.program_id(1)
    H = HEADS_PER_GROUP
    n_cmp = kc_ref.shape[0]
    n_slc = wov_ref.shape[0]
    lanes = H * QB
    head = lambda x, h: x[:, h * QB:(h + 1) * QB]

    @pl.when(i == 0)
    def _():
        kaug_ref[:, :HEAD_DIM] = ks_ref[...]
        kaug_ref[:, HEAD_DIM:] = blk_ref[...]
        kwaug_ref[:, :HEAD_DIM] = kw_ref[...]
        row = lax.broadcasted_iota(jnp.int32, (kwaug_ref.shape[0], LANES), 0)
        kwaug_ref[:, HEAD_DIM:] = jnp.where(row < PAD_ROWS, 1.0, 0.0).astype(CDT)

    qb = q_ref[...]
    q_t = jnp.concatenate([qb[:, h * HEAD_DIM:(h + 1) * HEAD_DIM].astype(jnp.float32).T.astype(CDT)
                           for h in range(H)], axis=1)

    def augment(extra):
        return jnp.concatenate([q_t, extra.astype(CDT)], axis=0)

    def values_t(ref, chunk0, n):
        return jnp.concatenate([ref[chunk0 + c] for c in range(n)], axis=1)

    grp = lax.broadcasted_iota(jnp.int32, (LANES, lanes), 0)
    sc_ref[...] = _dot(kc_ref[...], augment(jnp.where(grp > i, NEG, 0.0)))
    band0 = jnp.maximum(CMP_PER_TILE * i - BAND_BACK, 0)
    tab0 = band0 - (CMP_PER_TILE * i - BAND_BACK)
    band_rows = pl.ds(pl.multiple_of(band0, SUBLANES), BAND_ROWS)
    sc_ref[band_rows, :] += tab_ref[pl.ds(pl.multiple_of(NEAR_KEYS + WIN_KEYS + tab0, SUBLANES), BAND_ROWS), :]
    s_c = sc_ref[...]
    m_c = jnp.max(s_c, axis=0, keepdims=True)
    p_c = jnp.exp2(s_c - m_c)
    l_c = jnp.sum(p_c, axis=0, keepdims=True)
    pc_ref[...] = p_c.astype(CDT)
    seen = m_c > 0.5 * NEG
    o_cmp_t = _dot(vct_ref[...], pc_ref[...]) * jnp.where(seen, 1.0 / l_c, 0.0)
    shift_c = jnp.where(seen, m_c + jnp.log2(l_c), -NEG)
    imp_t = functools.reduce(lambda a, b: a + b,
                             [jnp.exp2(sc_ref[:, h * QB:(h + 1) * QB] - head(shift_c, h)) for h in range(H)])

    wov_t = wov_ref[...]
    slc_t = sum(_dot(wov_t, part) for part in _split3(imp_t))
    drop = jnp.where(_select_blocks(slc_t, i), 0.0, 1.0)
    j_row = lax.broadcasted_iota(jnp.int32, (n_slc, QB), 0)
    near_tile = i + 1 - NEAR_KEYS // QB
    drop_far = jnp.where(j_row < near_tile * (QB // SLC_BLOCK), drop, 1.0)
    tile_heads = lambda d: jnp.concatenate([d] * H, axis=1)

    win0 = pl.multiple_of(i * QB, QB)
    first_row = lax.broadcasted_iota(jnp.int32, (LANES, lanes), 0) == 0
    sw_ref[...] = (_dot(kwaug_ref[pl.ds(win0, WIN_KEYS), :], augment(jnp.where(first_row, NEG, 0.0)))
                   + tab_ref[NEAR_KEYS:NEAR_KEYS + WIN_KEYS])
    p_w, scale_w = _softmax_cols(sw_ref[...])
    pw_ref[...] = p_w.astype(CDT)
    o_win_t = _dot(values_t(vwt_ref, win0 // LANES, WIN_KEYS // LANES), pw_ref[...]) * scale_w

    near0 = pl.multiple_of(PAD_ROWS + near_tile * QB, QB)
    s_ref[...] = _dot(kaug_ref[pl.ds(near0, NEAR_KEYS), :], augment(tile_heads(drop))) + tab_ref[:NEAR_KEYS]
    p_ref[...] = jnp.zeros(p_ref.shape, CDT)
    acc_ref[...] = jnp.zeros(acc_ref.shape, jnp.float32)
    q_far = augment(tile_heads(drop_far))

    n_chunks = (kaug_ref.shape[0] - PAD_ROWS) // FAR_CHUNK
    n_far = (jnp.maximum(near_tile, 0) * QB + FAR_CHUNK - 1) // FAR_CHUNK
    far_lanes = FAR_CHUNK // LANES

    def far_step(it, carry):
        a_prev, m, l = carry
        v0 = jnp.where(it == 1, near0 // LANES, PAD_ROWS // LANES + jnp.clip(it - 2, 0, n_chunks - 1) * far_lanes)
        acc_ref[...] = a_prev * acc_ref[...] + _dot(values_t(vst_ref, v0, far_lanes), p_ref[...])
        s_prev = s_ref[...]
        m_new = jnp.maximum(m, jnp.max(s_prev, axis=0, keepdims=True))
        a = jnp.exp2(m - m_new)
        p = jnp.exp2(s_prev - m_new)
        l = a * l + jnp.sum(p, axis=0, keepdims=True)
        p_ref[...] = p.astype(CDT)
        k0 = pl.multiple_of(jnp.where(it < n_far, PAD_ROWS + it * FAR_CHUNK, 0), FAR_CHUNK)
        s_ref[...] = _dot(kaug_ref[pl.ds(k0, FAR_CHUNK), :], q_far)
        return a, m_new, l

    def far_trip(k, carry):
        for u in range(FAR_UNROLL):
            carry = far_step(FAR_UNROLL * k + u, carry)
        return carry

    zero = scale_w - scale_w
    steps = n_far + 2
    trips = steps // FAR_UNROLL
    carry = lax.fori_loop(0, trips, far_trip, (zero + 1.0, zero + 0.1 * NEG, zero))
    for u in range(FAR_UNROLL - 1):
        it = FAR_UNROLL * trips + u
        carry = lax.cond(it < steps, functools.partial(far_step, it), lambda c: c, carry)
    l_s = carry[2]
    o_slc_t = acc_ref[...] * (1.0 / l_s)

    gates_t = gate_ref[...].T
    for h in range(H):
        g = lambda b: gates_t[h * N_BRANCH + b:h * N_BRANCH + b + 1, :]
        o_t = g(0) * head(o_cmp_t, h) + g(1) * head(o_slc_t, h) + g(2) * head(o_win_t, h)
        o_ref[:, h * HEAD_DIM:(h + 1) * HEAD_DIM] = o_t.T.astype(o_ref.dtype)


def _block_membership(rows, n_slc):
    key = np.arange(rows)[:, None] - PAD_ROWS
    j = np.arange(n_slc)[None, :]
    member = np.where(key >= 0, key // SLC_BLOCK == j, True)
    return jnp.asarray(np.where(member, NEG, 0.0), dtype=CDT)


def _overlap_weights(n_cmp, n_slc):
    r = SLC_BLOCK // CMP_STRIDE
    lead = -(-CMP_BLOCK // CMP_STRIDE) - 1
    w = np.zeros((n_cmp, n_slc), np.float32)
    for o in range(-lead, r):
        s0 = o * CMP_STRIDE
        ov = max(0, min(s0 + CMP_BLOCK, SLC_BLOCK) - max(s0, 0))
        for j in range(n_slc):
            n = r * j + o
            if ov > 0 and 0 <= n < n_cmp - 1:
                w[n, j] = ov / CMP_STRIDE
    return jnp.asarray(w.T, dtype=CDT)


def _tile_group_columns(n_cmp):
    n = np.arange(n_cmp)[:, None]
    return jnp.asarray(n // CMP_PER_TILE == np.arange(LANES)[None, :], dtype=CDT)


def _attention(qkv, gates, kvc, tabs):
    rows = qkv.shape[0]
    S = rows - PAD_ROWS
    n_cmp = kvc.shape[2]
    n_slc = S // SLC_BLOCK
    assert n_cmp // CMP_PER_TILE <= LANES
    first = Q_WIDTH // LANES
    G = KV_GROUPS
    lanes = HEADS_PER_GROUP * QB
    kv_spec = lambda which: pl.BlockSpec((rows, LANES), lambda g, i: (0, first + which * G + g))
    vt_spec = lambda which: pl.BlockSpec((None, None, rows // LANES, HEAD_DIM, LANES), lambda g, i: (which, g, 0, 0, 0))
    width = tabs.shape[1]
    kc_aug = jnp.concatenate([kvc[0], jnp.broadcast_to(_tile_group_columns(n_cmp), (G, n_cmp, LANES))], axis=-1)
    vc_t = kvc[1].transpose(0, 2, 1)
    v_cols = jnp.stack([qkv[:, (first + w * G) * LANES:(first + (w + 1) * G) * LANES] for w in (3, 5)])
    v_t = v_cols.reshape(2, rows // LANES, LANES, G, HEAD_DIM).transpose(0, 3, 1, 4, 2)
    return pl.pallas_call(
        _attn_body,
        grid=(G, S // QB),
        in_specs=[pl.BlockSpec((QB, HEADS_PER_GROUP * HEAD_DIM), lambda g, i: (i + PAD_ROWS // QB, g)),
                  pl.BlockSpec((QB, LANES), lambda g, i: (i, g)),
                  pl.BlockSpec((None, n_cmp, HEAD_DIM + LANES), lambda g, i: (g, 0, 0)),
                  pl.BlockSpec((None, HEAD_DIM, n_cmp), lambda g, i: (g, 0, 0)),
                  kv_spec(2), vt_spec(0), kv_spec(4), vt_spec(1),
                  pl.BlockSpec((rows, n_slc), lambda g, i: (0, 0)),
                  pl.BlockSpec((None, width, lanes), lambda g, i: (g, 0, 0)),
                  pl.BlockSpec((n_slc, n_cmp), lambda g, i: (0, 0))],
        out_specs=pl.BlockSpec((QB, HEADS_PER_GROUP * HEAD_DIM), lambda g, i: (i, g)),
        out_shape=jax.ShapeDtypeStruct((S, Q_WIDTH), CDT),
        scratch_shapes=[pltpu.VMEM((rows, HEAD_DIM + n_slc), CDT),
                        pltpu.VMEM((rows, HEAD_DIM + LANES), CDT),
                        pltpu.VMEM((n_cmp, lanes), jnp.float32),
                        pltpu.VMEM((n_cmp, lanes), CDT),
                        pltpu.VMEM((WIN_KEYS, lanes), jnp.float32),
                        pltpu.VMEM((WIN_KEYS, lanes), CDT),
                        pltpu.VMEM((FAR_CHUNK, lanes), jnp.float32),
                        pltpu.VMEM((FAR_CHUNK, lanes), CDT),
                        pltpu.VMEM((HEAD_DIM, lanes), jnp.float32)],
        compiler_params=_cparams("parallel", "arbitrary"),
        name="nsa",
    )(qkv, gates, kc_aug, vc_t, qkv, v_t, qkv, v_t, _block_membership(rows, n_slc), tabs,
      _overlap_weights(n_cmp, n_slc))


def _outproj_body(a1_ref, a2_ref, w1_ref, w2_ref, x_ref, o_ref):
    o_ref[...] = x_ref[...] + _dot(a1_ref[...], w1_ref[...]) + _dot(a2_ref[...], w2_ref[...])


def _outproj(y_nsa, y_pool, w_out, x, bm=512, bn=1024):
    S, D = x.shape
    bn = min(bn, D)
    k1, k2 = y_nsa.shape[1], y_pool.shape[1]
    return pl.pallas_call(
        _outproj_body,
        grid=(S // bm, D // bn),
        in_specs=[pl.BlockSpec((bm, k1), lambda i, j: (i, 0)),
                  pl.BlockSpec((bm, k2), lambda i, j: (i, 0)),
                  pl.BlockSpec((k1, bn), lambda i, j: (0, j)),
                  pl.BlockSpec((k2, bn), lambda i, j: (k1 // k2, j)),
                  pl.BlockSpec((bm, bn), lambda i, j: (i, j))],
        out_specs=pl.BlockSpec((bm, bn), lambda i, j: (i, j)),
        out_shape=jax.ShapeDtypeStruct((S, D), jnp.float32),
        compiler_params=_cparams("parallel", "arbitrary"),
        name="outproj",
    )(y_nsa, y_pool, w_out, w_out, x)


CONV_HALO = 16
FFN_SUBTILES = 2
FFN_K_TILE = 512


def _ffn_up_body(a_ref, halo_ref, wa_ref, wb_ref, cwa_ref, cwb_ref, cba_ref, cbb_ref, o_ref, lhs_ref, *, nb):
    i = pl.program_id(0)
    j = pl.program_id(1)

    @pl.when(j == 0)
    def _():
        lhs_ref[:CONV_HALO] = jnp.where(i > 0, halo_ref[...], jnp.zeros_like(halo_ref))
        lhs_ref[CONV_HALO:] = a_ref[...]

    def conv(lhs, w, cw_ref, cb_ref):
        u = _dot(lhs, w)
        c = cb_ref[...] + cw_ref[CONV_WIDTH - 1:CONV_WIDTH, :] * u[CONV_HALO:]
        for k in range(1, CONV_WIDTH):
            c = c + cw_ref[CONV_WIDTH - 1 - k:CONV_WIDTH - k, :] * pltpu.roll(u, k, axis=0)[CONV_HALO:]
        return c

    @pl.when(j < nb)
    def _():
        wa = wa_ref[...].astype(CDT)
        wb = wb_ref[...].astype(CDT)
        sub = o_ref.shape[0] // FFN_SUBTILES
        for s in range(FFN_SUBTILES):
            lhs = lhs_ref[s * sub:s * sub + CONV_HALO + sub]
            ca = conv(lhs, wa, cwa_ref, cba_ref)
            cb = conv(lhs, wb, cwb_ref, cbb_ref)
            o_ref[s * sub:(s + 1) * sub] = (ca * jax.nn.sigmoid(ca) * cb).astype(o_ref.dtype)

    @pl.when(j >= nb)
    def _():
        o_ref[...] = jnp.zeros_like(o_ref)


def _ffn_up(hn, w_up, conv_w, conv_b, f_pad, bm=1024, bn=256):
    S, D = hn.shape
    F = w_up.shape[1] // 2
    nb = F // bn
    col = lambda j: jnp.minimum(j, nb - 1)
    return pl.pallas_call(
        functools.partial(_ffn_up_body, nb=nb),
        grid=(S // bm, f_pad // bn),
        in_specs=[pl.BlockSpec((bm, D), lambda i, j: (i, 0)),
                  pl.BlockSpec((CONV_HALO, D), lambda i, j: (jnp.maximum(i * (bm // CONV_HALO) - 1, 0), 0)),
                  pl.BlockSpec((D, bn), lambda i, j: (0, col(j))),
                  pl.BlockSpec((D, bn), lambda i, j: (0, nb + col(j))),
                  pl.BlockSpec((CONV_WIDTH, bn), lambda i, j: (0, col(j))),
                  pl.BlockSpec((CONV_WIDTH, bn), lambda i, j: (0, nb + col(j))),
                  pl.BlockSpec((1, bn), lambda i, j: (0, col(j))),
                  pl.BlockSpec((1, bn), lambda i, j: (0, nb + col(j)))],
        out_specs=pl.BlockSpec((bm, bn), lambda i, j: (i, j)),
        out_shape=jax.ShapeDtypeStruct((S, f_pad), CDT),
        scratch_shapes=[pltpu.VMEM((CONV_HALO + bm, D), CDT)],
        compiler_params=_cparams("parallel", "arbitrary"),
        name="ffn_up",
    )(hn, hn, w_up, w_up, conv_w, conv_w, conv_b, conv_b)


def _ffn_down_body(a_ref, w_ref, h_ref, g_ref, o_ref):
    k = pl.program_id(1)

    @pl.when(k == 0)
    def _():
        o_ref[...] = h_ref[...]

    o_ref[...] += _dot(a_ref[...], w_ref[...])

    @pl.when(k == pl.num_programs(1) - 1)
    def _():
        x = o_ref[...]
        y = x * lax.rsqrt(jnp.mean(x * x, axis=-1, keepdims=True) + RMS_EPS)
        o_ref[...] = y * g_ref[...]


def _ffn_down(act, w_down, h, g, bm=512, bk=FFN_K_TILE):
    S, F = act.shape
    D = h.shape[1]
    return pl.pallas_call(
        _ffn_down_body,
        grid=(S // bm, F // bk),
        in_specs=[pl.BlockSpec((bm, bk), lambda i, k: (i, k)),
                  pl.BlockSpec((bk, D), lambda i, k: (k, 0)),
                  pl.BlockSpec((bm, D), lambda i, k: (i, 0)),
                  pl.BlockSpec((1, D), lambda i, k: (0, 0))],
        out_specs=pl.BlockSpec((bm, D), lambda i, k: (i, 0)),
        out_shape=jax.ShapeDtypeStruct((S, D), jnp.float32),
        compiler_params=_cparams("parallel", "arbitrary"),
        name="ffn_down",
    )(act, w_down, h, g.reshape(1, D))


def _gate_weight(w_gate_t):
    D = w_gate_t.shape[1]
    per = HEADS_PER_GROUP * N_BRANCH
    w = w_gate_t.reshape(KV_GROUPS, per, D)
    w = jnp.pad(w, ((0, 0), (0, LANES - per), (0, 0)))
    return w.reshape(KV_GROUPS * LANES, D)


def _nsa_branch(hn, w_in, w_gate, cmp_w1_k, cmp_pe_k, cmp_w2_k, cmp_w1_v, cmp_pe_v, cmp_w2_v, rel_bias):
    qkv_cols = Q_WIDTH + 6 * KV_WIDTH
    w_gate = _gate_weight(w_gate)
    bm = 512
    qkv = _proj(hn, w_in, CDT, col0=POOL_WIDTH, n_cols=qkv_cols, pad_tiles=PAD_ROWS // bm, scaled_cols=Q_WIDTH,
                scale=HEAD_DIM ** -0.5 * LOG2E, bm=bm, name="proj_qkv")
    gates = _proj(hn, w_gate, jnp.float32, sigmoid=True, bm=bm, name="proj_gate")
    w1 = jnp.stack([cmp_w1_k, cmp_w1_v]).astype(CDT).reshape(2, 2, CMP_STRIDE * HEAD_DIM, HEAD_DIM)
    pe = jnp.stack([cmp_pe_k, cmp_pe_v]).astype(CDT).reshape(2, 2, CMP_STRIDE * HEAD_DIM)
    w2 = jnp.stack([cmp_w2_k, cmp_w2_v]).astype(CDT)
    kvc = _compress(qkv, w1, pe, w2)
    tabs = _bias_tiles(rel_bias)
    return _attention(qkv, gates, kvc, tabs)


def kernel(x, norm_mix_g, w_in, pool_w, pool_scale, cmp_pe_k, cmp_w1_k, cmp_w2_k, cmp_pe_v, cmp_w1_v, cmp_w2_v,
           rel_bias, w_out, norm_ffn_g, w_up, conv_w, conv_b, w_down, norm_final_g):
    B, S, D = x.shape
    assert B == 1 and w_in.shape[0] == 1, "single sequence, single layer"
    h = x.reshape(S, D)
    hn = _rmsnorm(h, norm_mix_g[0], CDT)
    n_main = POOL_WIDTH + Q_WIDTH + 6 * KV_WIDTH
    w_in_t = w_in.transpose(0, 2, 1)
    w_all = _cast_weight(w_in_t, 0, n_main, D, bm=512)
    u_pool = _proj(hn, w_all, jnp.float32, n_cols=POOL_WIDTH, name="proj_pool")
    y_pool = _pool(u_pool, pool_w[0], pool_scale[0])
    y_nsa = _nsa_branch(hn, w_all, w_in_t[0, n_main:], cmp_w1_k[0], cmp_pe_k[0], cmp_w2_k[0],
                        cmp_w1_v[0], cmp_pe_v[0], cmp_w2_v[0], rel_bias)
    h = _outproj(y_nsa, y_pool, w_out[0].astype(CDT), h)
    hn = _rmsnorm(h, norm_ffn_g[0], CDT)
    F = w_down.shape[1]
    f_pad = -(-F // FFN_K_TILE) * FFN_K_TILE
    act = _ffn_up(hn, w_up[0], conv_w[0], conv_b[0].reshape(1, -1), f_pad)
    out = _ffn_down(act, _cast_weight(w_down, 0, f_pad, D), h, norm_final_g)
    return out.reshape(B, S, D)
```

```python
import functools
import math

import numpy as np
import jax
import jax.numpy as jnp
from jax import lax
from jax.experimental import pallas as pl
from jax.experimental.pallas import tpu as pltpu

POOL_WINDOWS = (2, 4, 8, 16)
POOL_GROUP = 256
POOL_WIDTH = POOL_GROUP * len(POOL_WINDOWS)
HEAD_DIM = 128
KV_GROUPS = 4
HEADS_PER_GROUP = 6
NSA_HEADS = KV_GROUPS * HEADS_PER_GROUP
Q_WIDTH = NSA_HEADS * HEAD_DIM
KV_WIDTH = KV_GROUPS * HEAD_DIM
CMP_BLOCK = 32
CMP_STRIDE = 16
SLC_BLOCK = 64
N_SELECT = 16
N_LOCAL_FORCED = 2
WINDOW = 512
N_BRANCH = 3
REL_BUCKETS = 32
REL_MAX_DIST = 128
CONV_WIDTH = 3
RMS_EPS = 1e-6
NEG = -1e30
LOG2E = math.log2(math.e)

LANES = 128
SUBLANES = 8
VMEM_LIMIT_BYTES = 56 * 1024 * 1024

CDT = jnp.bfloat16
QB = 128
PAD_ROWS = WINDOW
FAR_CHUNK = 512
NEAR_KEYS = FAR_CHUNK
FAR_UNROLL = 2
WIN_KEYS = WINDOW + QB
CMP_PER_TILE = QB // CMP_STRIDE
BAND_BACK = 16
BAND_ROWS = 24
BAND_TABLE = BAND_ROWS + BAND_BACK


def _cparams(*sem, flags=None):
    return pltpu.CompilerParams(dimension_semantics=sem, vmem_limit_bytes=VMEM_LIMIT_BYTES, flags=flags)


def _dot(a, b):
    return jnp.dot(a, b, preferred_element_type=jnp.float32)


def _dot_nt(a, b):
    return lax.dot_general(a, b, (((1,), (1,)), ((), ())), preferred_element_type=jnp.float32)


def _cast_body(x_ref, o_ref, *, rows_valid):
    x = x_ref[...]
    row = pl.program_id(0) * x.shape[0] + lax.broadcasted_iota(jnp.int32, x.shape, 0)
    o_ref[...] = jnp.where(row < rows_valid, x, 0.0).astype(o_ref.dtype)


def _cast_weight(w, layer, n_rows, n_cols, bm=256, bn=4096):
    rows = w.shape[1]
    bm, bn = min(bm, rows), min(bn, n_cols)
    assert n_rows % bm == 0 and n_cols % bn == 0 and (rows % bm == 0 or n_rows <= rows // bm * bm)
    last = rows // bm - 1
    return pl.pallas_call(
        functools.partial(_cast_body, rows_valid=rows),
        grid=(n_rows // bm, n_cols // bn),
        in_specs=[pl.BlockSpec((None, bm, bn), lambda i, j: (layer, jnp.minimum(i, last), j))],
        out_specs=pl.BlockSpec((bm, bn), lambda i, j: (i, j)),
        out_shape=jax.ShapeDtypeStruct((n_rows, n_cols), CDT),
        compiler_params=_cparams("parallel", "parallel"),
        name="cast_weight",
    )(w)


def _rmsnorm_body(x_ref, g_ref, o_ref):
    x = x_ref[...]
    y = x * lax.rsqrt(jnp.mean(x * x, axis=-1, keepdims=True) + RMS_EPS)
    o_ref[...] = (y * g_ref[...]).astype(o_ref.dtype)


def _rmsnorm(x, g, out_dtype, bm=256):
    S, D = x.shape
    return pl.pallas_call(
        _rmsnorm_body,
        grid=(S // bm,),
        in_specs=[pl.BlockSpec((bm, D), lambda i: (i, 0)), pl.BlockSpec((1, D), lambda i: (0, 0))],
        out_specs=pl.BlockSpec((bm, D), lambda i: (i, 0)),
        out_shape=jax.ShapeDtypeStruct((S, D), out_dtype),
        compiler_params=_cparams("parallel"),
        name="rmsnorm",
    )(x, g.reshape(1, D))


def _in_proj_body(x_ref, g_ref, w_ref, wg_ref, pool_ref, qkv_ref, gate_ref, hn_ref, *, pad_tiles, n_main, n_scaled, scale):
    i = pl.program_id(0)
    j = pl.program_id(1)
    live = i >= pad_tiles

    @pl.when(live & (j == 0))
    def _():
        x = x_ref[...]
        y = x * lax.rsqrt(jnp.mean(x * x, axis=-1, keepdims=True) + RMS_EPS)
        hn_ref[...] = (y * g_ref[...]).astype(hn_ref.dtype)
        pool_ref[...] = _dot_nt(hn_ref[...], w_ref[...])

    @pl.when(live & (j > 0) & (j < n_main))
    def _():
        r = _dot_nt(hn_ref[...], w_ref[...])
        qkv_ref[...] = (r * jnp.where(j <= n_scaled, jnp.float32(scale), jnp.float32(1.0))).astype(qkv_ref.dtype)

    @pl.when(jnp.logical_not(live) & (j > 0) & (j < n_main))
    def _():
        qkv_ref[...] = jnp.zeros_like(qkv_ref)

    @pl.when(live & (j == n_main))
    def _():
        gate_ref[...] = jax.nn.sigmoid(_dot_nt(hn_ref[...], wg_ref[...]))


def _in_proj(x, g, w_t, wg_t, *, pad_tiles, scale, bm=512, bn=1024):
    S, D = x.shape
    n_main = w_t.shape[0] // bn
    qkv_cols = w_t.shape[0] - POOL_WIDTH
    assert POOL_WIDTH == bn and Q_WIDTH % bn == 0 and qkv_cols % bn == 0
    n_gate = wg_t.shape[0]
    row = lambda i: jnp.maximum(i - pad_tiles, 0)
    body = functools.partial(_in_proj_body, pad_tiles=pad_tiles, n_main=n_main, n_scaled=Q_WIDTH // bn, scale=scale)
    return pl.pallas_call(
        body,
        grid=(S // bm + pad_tiles, n_main + 1),
        in_specs=[pl.BlockSpec((bm, D), lambda i, j: (row(i), 0), pipeline_mode=pl.Buffered(1)),
                  pl.BlockSpec((1, D), lambda i, j: (0, 0)),
                  pl.BlockSpec((bn, D), lambda i, j: (jnp.minimum(j, n_main - 1), 0)),
                  pl.BlockSpec((n_gate, D), lambda i, j: (0, 0), pipeline_mode=pl.Buffered(1))],
        out_specs=[pl.BlockSpec((bm, bn), lambda i, j: (row(i), 0)),
                   pl.BlockSpec((bm, bn), lambda i, j: (i, jnp.clip(j - 1, 0, n_main - 2))),
                   pl.BlockSpec((bm, n_gate), lambda i, j: (row(i), 0))],
        out_shape=[jax.ShapeDtypeStruct((S, POOL_WIDTH), jnp.float32),
                   jax.ShapeDtypeStruct((S + pad_tiles * bm, qkv_cols), CDT),
                   jax.ShapeDtypeStruct((S, n_gate), jnp.float32)],
        scratch_shapes=[pltpu.VMEM((bm, D), CDT)],
        compiler_params=_cparams("parallel", "arbitrary"),
        name="in_proj",
    )(x, g.reshape(1, D), w_t, wg_t)


POOL_HALO = 16


def _pool_body(u_ref, halo_ref, w_ref, s_ref, o_ref):
    i = pl.program_id(0)
    bm = u_ref.shape[0]
    u = u_ref[...]
    halo = jnp.where(i > 0, halo_ref[...], 0.0)
    ext = jnp.concatenate([halo, u], axis=0)
    t = i * bm + lax.broadcasted_iota(jnp.int32, (bm, 1), 0)
    acc = ext
    sums = {}
    shift = 1
    while shift < POOL_WINDOWS[-1]:
        acc = acc + pltpu.roll(acc, shift, axis=0)
        shift *= 2
        sums[shift] = acc
    for gi, w in enumerate(POOL_WINDOWS):
        cols = slice(gi * POOL_GROUP, (gi + 1) * POOL_GROUP)
        cnt = jnp.minimum(t + 1, w).astype(jnp.float32)
        d = sums[w][POOL_HALO:, cols] / cnt - u[:, cols]
        y = _dot(d.astype(CDT), w_ref[gi])
        o_ref[:, cols] = (y * s_ref[:, cols]).astype(o_ref.dtype)


def _pool(u, pool_w, pool_scale, bm=512):
    S = u.shape[0]
    return pl.pallas_call(
        _pool_body,
        grid=(S // bm,),
        in_specs=[pl.BlockSpec((bm, POOL_WIDTH), lambda i: (i, 0)),
                  pl.BlockSpec((POOL_HALO, POOL_WIDTH), lambda i: (jnp.maximum(i * (bm // POOL_HALO) - 1, 0), 0)),
                  pl.BlockSpec((len(POOL_WINDOWS), POOL_GROUP, POOL_GROUP), lambda i: (0, 0, 0)),
                  pl.BlockSpec((1, POOL_WIDTH), lambda i: (0, 0))],
        out_specs=pl.BlockSpec((bm, POOL_WIDTH), lambda i: (i, 0)),
        out_shape=jax.ShapeDtypeStruct((S, POOL_WIDTH), CDT),
        compiler_params=_cparams("parallel"),
        name="pool",
    )(u, u, pool_w.astype(CDT), pool_scale.reshape(1, POOL_WIDTH))


def _compress_body(x_ref, w1_ref, pe_ref, w2_ref, o_ref, *, pad_chunks):
    x = x_ref[...]
    w_lo = w1_ref[0]
    w_hi = w1_ref[1]
    a = _dot(x, w_lo)
    b = _dot(x, w_hi)
    pe = _dot(jnp.broadcast_to(pe_ref[0:1], (SUBLANES, pe_ref.shape[1])), w_lo) \
        + _dot(jnp.broadcast_to(pe_ref[1:2], (SUBLANES, pe_ref.shape[1])), w_hi)
    n = o_ref.shape[0]
    b_next = pltpu.roll(b, b.shape[0] - 1, axis=0)[pad_chunks:pad_chunks + n]
    pre = a[pad_chunks:pad_chunks + n] + b_next + pe[0:1, :]
    h = pre * jax.nn.sigmoid(pre)
    o_ref[...] = _dot(h.astype(CDT), w2_ref[...]).astype(o_ref.dtype)


def _compress(qkv, w1, pe, w2):
    rows = qkv.shape[0]
    chunks = rows // CMP_STRIDE
    pad_chunks = PAD_ROWS // CMP_STRIDE
    n_out = chunks - pad_chunks
    G = KV_GROUPS
    x = qkv[:, Q_WIDTH:Q_WIDTH + 2 * KV_WIDTH].reshape(chunks, CMP_STRIDE, 2 * G, HEAD_DIM)
    x = x.transpose(2, 0, 1, 3).reshape(2 * G, chunks, CMP_STRIDE * HEAD_DIM)
    width = CMP_STRIDE * HEAD_DIM
    body = functools.partial(_compress_body, pad_chunks=pad_chunks)
    return pl.pallas_call(
        body,
        grid=(2, G),
        in_specs=[pl.BlockSpec((None, chunks, width), lambda kv, g: (kv * G + g, 0, 0)),
                  pl.BlockSpec((None, 2, width, HEAD_DIM), lambda kv, g: (kv, 0, 0, 0)),
                  pl.BlockSpec((None, 2, width), lambda kv, g: (kv, 0, 0)),
                  pl.BlockSpec((None, HEAD_DIM, HEAD_DIM), lambda kv, g: (kv, 0, 0))],
        out_specs=pl.BlockSpec((None, None, n_out, HEAD_DIM), lambda kv, g: (kv, g, 0, 0)),
        out_shape=jax.ShapeDtypeStruct((2, G, n_out, HEAD_DIM), CDT),
        compiler_params=_cparams("parallel", "parallel"),
        name="compress",
    )(x, w1, pe, w2)


def _rel_bucket_np(dist):
    n = np.maximum(dist, 0)
    max_exact = REL_BUCKETS // 2
    nf = np.maximum(n, 1).astype(np.float32)
    large = max_exact + (np.log(nf / max_exact) / math.log(REL_MAX_DIST / max_exact)
                         * (REL_BUCKETS - max_exact)).astype(np.int32)
    large = np.minimum(large, REL_BUCKETS - 1)
    return np.where(n < max_exact, n, large).astype(np.int32)


def _bias_index_tiles():
    assert BAND_TABLE <= LANES and BAND_BACK % SUBLANES == 0 and BAND_ROWS % SUBLANES == 0
    q = np.arange(QB)[:, None]
    dist = NEAR_KEYS - QB + q - np.arange(NEAR_KEYS)[None, :]
    near = np.where(dist >= 0, _rel_bucket_np(dist), -1)
    dist = WINDOW + q - np.arange(WIN_KEYS)[None, :]
    win = np.where((dist >= 0) & (dist < WINDOW), _rel_bucket_np(dist), -1)
    w = np.arange(LANES)[None, :]
    dist = q - CMP_STRIDE * (w - BAND_BACK) - (CMP_BLOCK - 1)
    band = np.where(dist >= 0, _rel_bucket_np(dist), -1)
    return np.concatenate([near, win, band], axis=1).T.astype(np.int32)


def _bias_body(tbl_ref, idx_ref, o_ref):
    h = pl.program_id(0)
    idx = idx_ref[...]
    far = tbl_ref[REL_BUCKETS - 1, h]
    val = jnp.full(idx.shape, NEG, jnp.float32)
    for b in range(REL_BUCKETS):
        val = jnp.where(idx == b, (tbl_ref[b, h] - far) * LOG2E, val)
    o_ref[...] = val


def _bias_tiles(rel_bias):
    idx = jnp.asarray(_bias_index_tiles())
    width = idx.shape[0]
    H = HEADS_PER_GROUP
    return pl.pallas_call(
        _bias_body,
        grid=(NSA_HEADS,),
        in_specs=[pl.BlockSpec(memory_space=pltpu.SMEM),
                  pl.BlockSpec((width, QB), lambda h: (0, 0))],
        out_specs=pl.BlockSpec((None, width, QB), lambda h: (h // H, 0, h % H)),
        out_shape=jax.ShapeDtypeStruct((KV_GROUPS, width, H * QB), jnp.float32),
        compiler_params=_cparams("arbitrary"),
        name="bias_tiles",
    )(rel_bias, idx)


def _softmax_cols(s):
    m = jnp.max(s, axis=0, keepdims=True)
    p = jnp.exp2(s - m)
    l = jnp.sum(p, axis=0, keepdims=True)
    return p, jnp.where(m > 0.5 * NEG, 1.0 / l, 0.0)


def _split3(x):
    hi = x.astype(CDT)
    r = x - hi.astype(jnp.float32)
    mid = r.astype(CDT)
    lo = (r - mid.astype(jnp.float32)).astype(CDT)
    return hi, mid, lo


def _select_blocks(slc_t, i):
    n_slc = slc_t.shape[0]
    t = i * QB + lax.broadcasted_iota(jnp.int32, (1, QB), 1)
    j_int = lax.broadcasted_iota(jnp.int32, (n_slc, QB), 0)
    j_idx = j_int.astype(jnp.float32)
    cur = t // SLC_BLOCK
    forced = (j_int == 0) | ((cur - j_int >= 0) & (cur - j_int < N_LOCAL_FORCED))
    score = jnp.where(forced, 1e9, jnp.where(j_int > cur, -1e9, slc_t))
    picked = jnp.zeros(score.shape, jnp.bool_)
    for _ in range(min(N_SELECT, n_slc)):
        m = jnp.max(score, axis=0, keepdims=True)
        first = jnp.min(jnp.where(score == m, j_idx, float(n_slc)), axis=0, keepdims=True)
        hit = j_idx == first
        picked = picked | hit
        score = jnp.where(hit, -3e38, score)
    return picked


def _attn_body(q_ref, gate_ref, kc_ref, vct_ref, ks_ref, vst_ref, kw_ref, vwt_ref, blk_ref, tab_ref, wov_ref,
               o_ref, kaug_ref, kwaug_ref, sc_ref, pc_ref, sw_ref, pw_ref, s_ref, p_ref, acc_ref):
    i = pl.program_id(1)
    H = HEADS_PER_GROUP
    n_cmp = kc_ref.shape[0]
    n_slc = wov_ref.shape[0]
    lanes = H * QB
    head = lambda x, h: x[:, h * QB:(h + 1) * QB]

    @pl.when(i == 0)
    def _():
        kaug_ref[:, :HEAD_DIM] = ks_ref[...]
        kaug_ref[:, HEAD_DIM:] = blk_ref[...]
        kwaug_ref[:, :HEAD_DIM] = kw_ref[...]
        row = lax.broadcasted_iota(jnp.int32, (kwaug_ref.shape[0], LANES), 0)
        kwaug_ref[:, HEAD_DIM:] = jnp.where(row < PAD_ROWS, 1.0, 0.0).astype(CDT)

    qb = q_ref[...]
    q_t = jnp.concatenate([qb[:, h * HEAD_DIM:(h + 1) * HEAD_DIM].astype(jnp.float32).T.astype(CDT)
                           for h in range(H)], axis=1)

    def augment(extra):
        return jnp.concatenate([q_t, extra.astype(CDT)], axis=0)

    def values_t(ref, chunk0, n):
        return jnp.concatenate([ref[chunk0 + c] for c in range(n)], axis=1)

    grp = lax.broadcasted_iota(jnp.int32, (LANES, lanes), 0)
    sc_ref[...] = _dot(kc_ref[...], augment(jnp.where(grp > i, NEG, 0.0)))
    band0 = jnp.maximum(CMP_PER_TILE * i - BAND_BACK, 0)
    tab0 = band0 - (CMP_PER_TILE * i - BAND_BACK)
    band_rows = pl.ds(pl.multiple_of(band0, SUBLANES), BAND_ROWS)
    sc_ref[band_rows, :] += tab_ref[pl.ds(pl.multiple_of(NEAR_KEYS + WIN_KEYS + tab0, SUBLANES), BAND_ROWS), :]
    s_c = sc_ref[...]
    m_c = jnp.max(s_c, axis=0, keepdims=True)
    p_c = jnp.exp2(s_c - m_c)
    l_c = jnp.sum(p_c, axis=0, keepdims=True)
    pc_ref[...] = p_c.astype(CDT)
    seen = m_c > 0.5 * NEG
    o_cmp_t = _dot(vct_ref[...], pc_ref[...]) * jnp.where(seen, 1.0 / l_c, 0.0)
    shift_c = jnp.where(seen, m_c + jnp.log2(l_c), -NEG)
    imp_t = functools.reduce(lambda a, b: a + b,
                             [jnp.exp2(sc_ref[:, h * QB:(h + 1) * QB] - head(shift_c, h)) for h in range(H)])

    wov_t = wov_ref[...]
    slc_t = sum(_dot(wov_t, part) for part in _split3(imp_t))
    drop = jnp.where(_select_blocks(slc_t, i), 0.0, 1.0)
    j_row = lax.broadcasted_iota(jnp.int32, (n_slc, QB), 0)
    near_tile = i + 1 - NEAR_KEYS // QB
    drop_far = jnp.where(j_row < near_tile * (QB // SLC_BLOCK), drop, 1.0)
    tile_heads = lambda d: jnp.concatenate([d] * H, axis=1)

    win0 = pl.multiple_of(i * QB, QB)
    first_row = lax.broadcasted_iota(jnp.int32, (LANES, lanes), 0) == 0
    sw_ref[...] = (_dot(kwaug_ref[pl.ds(win0, WIN_KEYS), :], augment(jnp.where(first_row, NEG, 0.0)))
                   + tab_ref[NEAR_KEYS:NEAR_KEYS + WIN_KEYS])
    p_w, scale_w = _softmax_cols(sw_ref[...])
    pw_ref[...] = p_w.astype(CDT)
    o_win_t = _dot(values_t(vwt_ref, win0 // LANES, WIN_KEYS // LANES), pw_ref[...]) * scale_w

    near0 = pl.multiple_of(PAD_ROWS + near_tile * QB, QB)
    s_ref[...] = _dot(kaug_ref[pl.ds(near0, NEAR_KEYS), :], augment(tile_heads(drop))) + tab_ref[:NEAR_KEYS]
    p_ref[...] = jnp.zeros(p_ref.shape, CDT)
    acc_ref[...] = jnp.zeros(acc_ref.shape, jnp.float32)
    q_far = augment(tile_heads(drop_far))

    n_chunks = (kaug_ref.shape[0] - PAD_ROWS) // FAR_CHUNK
    n_far = (jnp.maximum(near_tile, 0) * QB + FAR_CHUNK - 1) // FAR_CHUNK
    far_lanes = FAR_CHUNK // LANES

    def far_step(it, carry):
        a_prev, m, l = carry
        v0 = jnp.where(it == 1, near0 // LANES, PAD_ROWS // LANES + jnp.clip(it - 2, 0, n_chunks - 1) * far_lanes)
        acc_ref[...] = a_prev * acc_ref[...] + _dot(values_t(vst_ref, v0, far_lanes), p_ref[...])
        s_prev = s_ref[...]
        m_new = jnp.maximum(m, jnp.max(s_prev, axis=0, keepdims=True))
        a = jnp.exp2(m - m_new)
        p = jnp.exp2(s_prev - m_new)
        l = a * l + jnp.sum(p, axis=0, keepdims=True)
        p_ref[...] = p.astype(CDT)
        k0 = pl.multiple_of(jnp.where(it < n_far, PAD_ROWS + it * FAR_CHUNK, 0), FAR_CHUNK)
        s_ref[...] = _dot(kaug_ref[pl.ds(k0, FAR_CHUNK), :], q_far)
        return a, m_new, l

    def far_trip(k, carry):
        for u in range(FAR_UNROLL):
            carry = far_step(FAR_UNROLL * k + u, carry)
        return carry

    zero = scale_w - scale_w
    steps = n_far + 2
    trips = steps // FAR_UNROLL
    carry = lax.fori_loop(0, trips, far_trip, (zero + 1.0, zero + 0.1 * NEG, zero))
    for u in range(FAR_UNROLL - 1):
        it = FAR_UNROLL * trips + u
        carry = lax.cond(it < steps, functools.partial(far_step, it), lambda c: c, carry)
    l_s = carry[2]
    o_slc_t = acc_ref[...] * (1.0 / l_s)

    gates_t = gate_ref[...].T
    for h in range(H):
        g = lambda b: gates_t[h * N_BRANCH + b:h * N_BRANCH + b + 1, :]
        o_t = g(0) * head(o_cmp_t, h) + g(1) * head(o_slc_t, h) + g(2) * head(o_win_t, h)
        o_ref[:, h * HEAD_DIM:(h + 1) * HEAD_DIM] = o_t.T.astype(o_ref.dtype)


def _block_membership(rows, n_slc):
    key = np.arange(rows)[:, None] - PAD_ROWS
    j = np.arange(n_slc)[None, :]
    member = np.where(key >= 0, key // SLC_BLOCK == j, True)
    return jnp.asarray(np.where(member, NEG, 0.0), dtype=CDT)


def _overlap_weights(n_cmp, n_slc):
    r = SLC_BLOCK // CMP_STRIDE
    lead = -(-CMP_BLOCK // CMP_STRIDE) - 1
    w = np.zeros((n_cmp, n_slc), np.float32)
    for o in range(-lead, r):
        s0 = o * CMP_STRIDE
        ov = max(0, min(s0 + CMP_BLOCK, SLC_BLOCK) - max(s0, 0))
        for j in range(n_slc):
            n = r * j + o
            if ov > 0 and 0 <= n < n_cmp - 1:
                w[n, j] = ov / CMP_STRIDE
    return jnp.asarray(w.T, dtype=CDT)


def _tile_group_columns(n_cmp):
    n = np.arange(n_cmp)[:, None]
    return jnp.asarray(n // CMP_PER_TILE == np.arange(LANES)[None, :], dtype=CDT)


def _attention(qkv, gates, kvc, tabs):
    rows = qkv.shape[0]
    S = rows - PAD_ROWS
    n_cmp = kvc.shape[2]
    n_slc = S // SLC_BLOCK
    assert n_cmp // CMP_PER_TILE <= LANES
    first = Q_WIDTH // LANES
    G = KV_GROUPS
    lanes = HEADS_PER_GROUP * QB
    kv_spec = lambda which: pl.BlockSpec((rows, LANES), lambda g, i: (0, first + which * G + g))
    vt_spec = lambda which: pl.BlockSpec((None, None, rows // LANES, HEAD_DIM, LANES), lambda g, i: (which, g, 0, 0, 0))
    width = tabs.shape[1]
    kc_aug = jnp.concatenate([kvc[0], jnp.broadcast_to(_tile_group_columns(n_cmp), (G, n_cmp, LANES))], axis=-1)
    vc_t = kvc[1].transpose(0, 2, 1)
    v_cols = jnp.stack([qkv[:, (first + w * G) * LANES:(first + (w + 1) * G) * LANES] for w in (3, 5)])
    v_t = v_cols.reshape(2, rows // LANES, LANES, G, HEAD_DIM).transpose(0, 3, 1, 4, 2)
    return pl.pallas_call(
        _attn_body,
        grid=(G, S // QB),
        in_specs=[pl.BlockSpec((QB, HEADS_PER_GROUP * HEAD_DIM), lambda g, i: (i + PAD_ROWS // QB, g)),
                  pl.BlockSpec((QB, LANES), lambda g, i: (i, g)),
                  pl.BlockSpec((None, n_cmp, HEAD_DIM + LANES), lambda g, i: (g, 0, 0)),
                  pl.BlockSpec((None, HEAD_DIM, n_cmp), lambda g, i: (g, 0, 0)),
                  kv_spec(2), vt_spec(0), kv_spec(4), vt_spec(1),
                  pl.BlockSpec((rows, n_slc), lambda g, i: (0, 0)),
                  pl.BlockSpec((None, width, lanes), lambda g, i: (g, 0, 0)),
                  pl.BlockSpec((n_slc, n_cmp), lambda g, i: (0, 0))],
        out_specs=pl.BlockSpec((QB, HEADS_PER_GROUP * HEAD_DIM), lambda g, i: (i, g)),
        out_shape=jax.ShapeDtypeStruct((S, Q_WIDTH), CDT),
        scratch_shapes=[pltpu.VMEM((rows, HEAD_DIM + n_slc), CDT),
                        pltpu.VMEM((rows, HEAD_DIM + LANES), CDT),
                        pltpu.VMEM((n_cmp, lanes), jnp.float32),
                        pltpu.VMEM((n_cmp, lanes), CDT),
                        pltpu.VMEM((WIN_KEYS, lanes), jnp.float32),
                        pltpu.VMEM((WIN_KEYS, lanes), CDT),
                        pltpu.VMEM((FAR_CHUNK, lanes), jnp.float32),
                        pltpu.VMEM((FAR_CHUNK, lanes), CDT),
                        pltpu.VMEM((HEAD_DIM, lanes), jnp.float32)],
        compiler_params=_cparams("parallel", "arbitrary"),
        name="nsa",
    )(qkv, gates, kc_aug, vc_t, qkv, v_t, qkv, v_t, _block_membership(rows, n_slc), tabs,
      _overlap_weights(n_cmp, n_slc))


def _outproj_body(a1_ref, a2_ref, w1_ref, w2_ref, x_ref, o_ref):
    o_ref[...] = x_ref[...] + _dot(a1_ref[...], w1_ref[...]) + _dot(a2_ref[...], w2_ref[...])


def _outproj(y_nsa, y_pool, w_out, x, bm=512, bn=1024):
    S, D = x.shape
    bn = min(bn, D)
    k1, k2 = y_nsa.shape[1], y_pool.shape[1]
    return pl.pallas_call(
        _outproj_body,
        grid=(S // bm, D // bn),
        in_specs=[pl.BlockSpec((bm, k1), lambda i, j: (i, 0)),
                  pl.BlockSpec((bm, k2), lambda i, j: (i, 0)),
                  pl.BlockSpec((k1, bn), lambda i, j: (0, j)),
                  pl.BlockSpec((k2, bn), lambda i, j: (k1 // k2, j)),
                  pl.BlockSpec((bm, bn), lambda i, j: (i, j))],
        out_specs=pl.BlockSpec((bm, bn), lambda i, j: (i, j)),
        out_shape=jax.ShapeDtypeStruct((S, D), jnp.float32),
        compiler_params=_cparams("parallel", "arbitrary"),
        name="outproj",
    )(y_nsa, y_pool, w_out, w_out, x)


CONV_HALO = 16
FFN_SUBTILES = 2
FFN_K_TILE = 512


def _ffn_up_body(a_ref, halo_ref, wa_ref, wb_ref, cwa_ref, cwb_ref, cba_ref, cbb_ref, o_ref, lhs_ref, *, nb):
    i = pl.program_id(0)
    j = pl.program_id(1)

    @pl.when(j == 0)
    def _():
        lhs_ref[:CONV_HALO] = jnp.where(i > 0, halo_ref[...], jnp.zeros_like(halo_ref))
        lhs_ref[CONV_HALO:] = a_ref[...]

    def conv(lhs, w, cw_ref, cb_ref):
        u = _dot(lhs, w)
        c = cb_ref[...] + cw_ref[CONV_WIDTH - 1:CONV_WIDTH, :] * u[CONV_HALO:]
        for k in range(1, CONV_WIDTH):
            c = c + cw_ref[CONV_WIDTH - 1 - k:CONV_WIDTH - k, :] * pltpu.roll(u, k, axis=0)[CONV_HALO:]
        return c

    @pl.when(j < nb)
    def _():
        wa = wa_ref[...].astype(CDT)
        wb = wb_ref[...].astype(CDT)
        sub = o_ref.shape[0] // FFN_SUBTILES
        for s in range(FFN_SUBTILES):
            lhs = lhs_ref[s * sub:s * sub + CONV_HALO + sub]
            ca = conv(lhs, wa, cwa_ref, cba_ref)
            cb = conv(lhs, wb, cwb_ref, cbb_ref)
            o_ref[s * sub:(s + 1) * sub] = (ca * jax.nn.sigmoid(ca) * cb).astype(o_ref.dtype)

    @pl.when(j >= nb)
    def _():
        o_ref[...] = jnp.zeros_like(o_ref)


def _ffn_up(hn, w_up, conv_w, conv_b, f_pad, bm=1024, bn=256):
    S, D = hn.shape
    F = w_up.shape[1] // 2
    nb = F // bn
    col = lambda j: jnp.minimum(j, nb - 1)
    return pl.pallas_call(
        functools.partial(_ffn_up_body, nb=nb),
        grid=(S // bm, f_pad // bn),
        in_specs=[pl.BlockSpec((bm, D), lambda i, j: (i, 0)),
                  pl.BlockSpec((CONV_HALO, D), lambda i, j: (jnp.maximum(i * (bm // CONV_HALO) - 1, 0), 0)),
                  pl.BlockSpec((D, bn), lambda i, j: (0, col(j))),
                  pl.BlockSpec((D, bn), lambda i, j: (0, nb + col(j))),
                  pl.BlockSpec((CONV_WIDTH, bn), lambda i, j: (0, col(j))),
                  pl.BlockSpec((CONV_WIDTH, bn), lambda i, j: (0, nb + col(j))),
                  pl.BlockSpec((1, bn), lambda i, j: (0, col(j))),
                  pl.BlockSpec((1, bn), lambda i, j: (0, nb + col(j)))],
        out_specs=pl.BlockSpec((bm, bn), lambda i, j: (i, j)),
        out_shape=jax.ShapeDtypeStruct((S, f_pad), CDT),
        scratch_shapes=[pltpu.VMEM((CONV_HALO + bm, D), CDT)],
        compiler_params=_cparams("parallel", "arbitrary"),
        name="ffn_up",
    )(hn, hn, w_up, w_up, conv_w, conv_w, conv_b, conv_b)


def _ffn_down_body(a_ref, w_ref, h_ref, g_ref, o_ref):
    k = pl.program_id(1)

    @pl.when(k == 0)
    def _():
        o_ref[...] = h_ref[...]

    o_ref[...] += _dot(a_ref[...], w_ref[...])

    @pl.when(k == pl.num_programs(1) - 1)
    def _():
        x = o_ref[...]
        y = x * lax.rsqrt(jnp.mean(x * x, axis=-1, keepdims=True) + RMS_EPS)
        o_ref[...] = y * g_ref[...]


def _ffn_down(act, w_down, h, g, bm=512, bk=FFN_K_TILE):
    S, F = act.shape
    D = h.shape[1]
    return pl.pallas_call(
        _ffn_down_body,
        grid=(S // bm, F // bk),
        in_specs=[pl.BlockSpec((bm, bk), lambda i, k: (i, k)),
                  pl.BlockSpec((bk, D), lambda i, k: (k, 0)),
                  pl.BlockSpec((bm, D), lambda i, k: (i, 0)),
                  pl.BlockSpec((1, D), lambda i, k: (0, 0))],
        out_specs=pl.BlockSpec((bm, D), lambda i, k: (i, 0)),
        out_shape=jax.ShapeDtypeStruct((S, D), jnp.float32),
        compiler_params=_cparams("parallel", "arbitrary"),
        name="ffn_down",
    )(act, w_down, h, g.reshape(1, D))


def _gate_weight(w_gate_t):
    D = w_gate_t.shape[1]
    per = HEADS_PER_GROUP * N_BRANCH
    w = w_gate_t.reshape(KV_GROUPS, per, D)
    w = jnp.pad(w, ((0, 0), (0, LANES - per), (0, 0)))
    return w.reshape(KV_GROUPS * LANES, D)


def _nsa_branch(qkv, gates, cmp_w1_k, cmp_pe_k, cmp_w2_k, cmp_w1_v, cmp_pe_v, cmp_w2_v, rel_bias):
    w1 = jnp.stack([cmp_w1_k, cmp_w1_v]).astype(CDT).reshape(2, 2, CMP_STRIDE * HEAD_DIM, HEAD_DIM)
    pe = jnp.stack([cmp_pe_k, cmp_pe_v]).astype(CDT).reshape(2, 2, CMP_STRIDE * HEAD_DIM)
    w2 = jnp.stack([cmp_w2_k, cmp_w2_v]).astype(CDT)
    kvc = _compress(qkv, w1, pe, w2)
    tabs = _bias_tiles(rel_bias)
    return _attention(qkv, gates, kvc, tabs)


def kernel(x, norm_mix_g, w_in, pool_w, pool_scale, cmp_pe_k, cmp_w1_k, cmp_w2_k, cmp_pe_v, cmp_w1_v, cmp_w2_v,
           rel_bias, w_out, norm_ffn_g, w_up, conv_w, conv_b, w_down, norm_final_g):
    B, S, D = x.shape
    assert B == 1 and w_in.shape[0] == 1, "single sequence, single layer"
    h = x.reshape(S, D)
    n_main = POOL_WIDTH + Q_WIDTH + 6 * KV_WIDTH
    w_in_t = w_in.transpose(0, 2, 1)
    w_all = _cast_weight(w_in_t, 0, n_main, D, bm=512)
    w_gate = _gate_weight(w_in_t[0, n_main:]).astype(CDT)
    bm = 512
    u_pool, qkv, gates = _in_proj(h, norm_mix_g[0], w_all, w_gate, pad_tiles=PAD_ROWS // bm,
                                  scale=HEAD_DIM ** -0.5 * LOG2E, bm=bm)
    y_pool = _pool(u_pool, pool_w[0], pool_scale[0])
    y_nsa = _nsa_branch(qkv, gates, cmp_w1_k[0], cmp_pe_k[0], cmp_w2_k[0],
                        cmp_w1_v[0], cmp_pe_v[0], cmp_w2_v[0], rel_bias)
    h = _outproj(y_nsa, y_pool, w_out[0].astype(CDT), h)
    hn = _rmsnorm(h, norm_ffn_g[0], CDT)
    F = w_down.shape[1]
    f_pad = -(-F // FFN_K_TILE) * FFN_K_TILE
    act = _ffn_up(hn, w_up[0], conv_w[0], conv_b[0].reshape(1, -1), f_pad)
    out = _ffn_down(act, _cast_weight(w_down, 0, f_pad, D), h, norm_final_g)
    return out.reshape(B, S, D)
```

```python
import functools
import math

import numpy as np
import jax
import jax.numpy as jnp
from jax import lax
from jax.experimental import pallas as pl
from jax.experimental.pallas import tpu as pltpu

POOL_WINDOWS = (2, 4, 8, 16)
POOL_GROUP = 256
POOL_WIDTH = POOL_GROUP * len(POOL_WINDOWS)
HEAD_DIM = 128
KV_GROUPS = 4
HEADS_PER_GROUP = 6
NSA_HEADS = KV_GROUPS * HEADS_PER_GROUP
Q_WIDTH = NSA_HEADS * HEAD_DIM
KV_WIDTH = KV_GROUPS * HEAD_DIM
CMP_BLOCK = 32
CMP_STRIDE = 16
SLC_BLOCK = 64
N_SELECT = 16
N_LOCAL_FORCED = 2
WINDOW = 512
N_BRANCH = 3
REL_BUCKETS = 32
REL_MAX_DIST = 128
CONV_WIDTH = 3
RMS_EPS = 1e-6
NEG = -1e30
LOG2E = math.log2(math.e)

LANES = 128
SUBLANES = 8
VMEM_LIMIT_BYTES = 56 * 1024 * 1024

CDT = jnp.bfloat16
QB = 128
PAD_ROWS = WINDOW
FAR_CHUNK = 512
NEAR_KEYS = FAR_CHUNK
FAR_UNROLL = 2
WIN_KEYS = WINDOW + QB
CMP_PER_TILE = QB // CMP_STRIDE
BAND_BACK = 16
BAND_ROWS = 24
BAND_TABLE = BAND_ROWS + BAND_BACK


def _cparams(*sem):
    return pltpu.CompilerParams(dimension_semantics=sem, vmem_limit_bytes=VMEM_LIMIT_BYTES)


def _dot(a, b):
    return jnp.dot(a, b, preferred_element_type=jnp.float32)


def _dot_nt(a, b):
    return lax.dot_general(a, b, (((1,), (1,)), ((), ())), preferred_element_type=jnp.float32)


def _cast_body(x_ref, o_ref, *, rows_valid):
    x = x_ref[...]
    row = pl.program_id(0) * x.shape[0] + lax.broadcasted_iota(jnp.int32, x.shape, 0)
    o_ref[...] = jnp.where(row < rows_valid, x, 0.0).astype(o_ref.dtype)


def _cast_weight(w, layer, n_rows, n_cols, bm=256, bn=4096):
    rows = w.shape[1]
    bm, bn = min(bm, rows), min(bn, n_cols)
    assert n_rows % bm == 0 and n_cols % bn == 0 and (rows % bm == 0 or n_rows <= rows // bm * bm)
    last = rows // bm - 1
    return pl.pallas_call(
        functools.partial(_cast_body, rows_valid=rows),
        grid=(n_rows // bm, n_cols // bn),
        in_specs=[pl.BlockSpec((None, bm, bn), lambda i, j: (layer, jnp.minimum(i, last), j))],
        out_specs=pl.BlockSpec((bm, bn), lambda i, j: (i, j)),
        out_shape=jax.ShapeDtypeStruct((n_rows, n_cols), CDT),
        compiler_params=_cparams("parallel", "parallel"),
        name="cast_weight",
    )(w)


def _rmsnorm_body(x_ref, g_ref, o_ref):
    x = x_ref[...]
    y = x * lax.rsqrt(jnp.mean(x * x, axis=-1, keepdims=True) + RMS_EPS)
    o_ref[...] = (y * g_ref[...]).astype(o_ref.dtype)


def _rmsnorm(x, g, out_dtype, bm=512):
    S, D = x.shape
    return pl.pallas_call(
        _rmsnorm_body,
        grid=(S // bm,),
        in_specs=[pl.BlockSpec((bm, D), lambda i: (i, 0)), pl.BlockSpec((1, D), lambda i: (0, 0))],
        out_specs=pl.BlockSpec((bm, D), lambda i: (i, 0)),
        out_shape=jax.ShapeDtypeStruct((S, D), out_dtype),
        compiler_params=_cparams("parallel"),
        name="rmsnorm",
    )(x, g.reshape(1, D))


def _proj_body(a_ref, w_ref, o_ref, *, pad_tiles, n_scaled, scale, sigmoid):
    i = pl.program_id(0)
    j = pl.program_id(1)

    @pl.when(i < pad_tiles)
    def _():
        o_ref[...] = jnp.zeros_like(o_ref)

    @pl.when(i >= pad_tiles)
    def _():
        r = _dot_nt(a_ref[...], w_ref[...].astype(a_ref.dtype))
        if n_scaled:
            r = r * jnp.where(j < n_scaled, jnp.float32(scale), jnp.float32(1.0))
        if sigmoid:
            r = jax.nn.sigmoid(r)
        o_ref[...] = r.astype(o_ref.dtype)


def _proj(a, w_t, out_dtype, *, col0=0, n_cols=None, pad_tiles=0, scaled_cols=0, scale=1.0, sigmoid=False,
          bm=512, bn=1024, name="proj"):
    S, D = a.shape
    N = w_t.shape[0] if n_cols is None else n_cols
    bn = min(bn, N)
    assert scaled_cols % bn == 0 and col0 % bn == 0 and N % bn == 0
    n_scaled = scaled_cols // bn
    first = col0 // bn
    body = functools.partial(_proj_body, pad_tiles=pad_tiles, n_scaled=n_scaled, scale=scale, sigmoid=sigmoid)
    return pl.pallas_call(
        body,
        grid=(S // bm + pad_tiles, N // bn),
        in_specs=[pl.BlockSpec((bm, D), lambda i, j: (jnp.maximum(i - pad_tiles, 0), 0)),
                  pl.BlockSpec((bn, D), lambda i, j: (first + j, 0))],
        out_specs=pl.BlockSpec((bm, bn), lambda i, j: (i, j)),
        out_shape=jax.ShapeDtypeStruct((S + pad_tiles * bm, N), out_dtype),
        compiler_params=_cparams("parallel", "arbitrary"),
        name=name,
    )(a, w_t)


POOL_HALO = 16


def _pool_body(u_ref, halo_ref, w_ref, s_ref, o_ref):
    i = pl.program_id(0)
    bm = u_ref.shape[0]
    u = u_ref[...]
    halo = jnp.where(i > 0, halo_ref[...], 0.0)
    ext = jnp.concatenate([halo, u], axis=0)
    t = i * bm + lax.broadcasted_iota(jnp.int32, (bm, 1), 0)
    acc = ext
    sums = {}
    shift = 1
    while shift < POOL_WINDOWS[-1]:
        acc = acc + pltpu.roll(acc, shift, axis=0)
        shift *= 2
        sums[shift] = acc
    for gi, w in enumerate(POOL_WINDOWS):
        cols = slice(gi * POOL_GROUP, (gi + 1) * POOL_GROUP)
        cnt = jnp.minimum(t + 1, w).astype(jnp.float32)
        d = sums[w][POOL_HALO:, cols] / cnt - u[:, cols]
        y = _dot(d.astype(CDT), w_ref[gi])
        o_ref[:, cols] = (y * s_ref[:, cols]).astype(o_ref.dtype)


def _pool(u, pool_w, pool_scale, bm=512):
    S = u.shape[0]
    return pl.pallas_call(
        _pool_body,
        grid=(S // bm,),
        in_specs=[pl.BlockSpec((bm, POOL_WIDTH), lambda i: (i, 0)),
                  pl.BlockSpec((POOL_HALO, POOL_WIDTH), lambda i: (jnp.maximum(i * (bm // POOL_HALO) - 1, 0), 0)),
                  pl.BlockSpec((len(POOL_WINDOWS), POOL_GROUP, POOL_GROUP), lambda i: (0, 0, 0)),
                  pl.BlockSpec((1, POOL_WIDTH), lambda i: (0, 0))],
        out_specs=pl.BlockSpec((bm, POOL_WIDTH), lambda i: (i, 0)),
        out_shape=jax.ShapeDtypeStruct((S, POOL_WIDTH), CDT),
        compiler_params=_cparams("parallel"),
        name="pool",
    )(u, u, pool_w.astype(CDT), pool_scale.reshape(1, POOL_WIDTH))


def _compress_body(x_ref, w1_ref, pe_ref, w2_ref, o_ref, *, pad_chunks):
    x = x_ref[...]
    w_lo = w1_ref[0]
    w_hi = w1_ref[1]
    a = _dot(x, w_lo)
    b = _dot(x, w_hi)
    pe = _dot(jnp.broadcast_to(pe_ref[0:1], (SUBLANES, pe_ref.shape[1])), w_lo) \
        + _dot(jnp.broadcast_to(pe_ref[1:2], (SUBLANES, pe_ref.shape[1])), w_hi)
    n = o_ref.shape[0]
    b_next = pltpu.roll(b, b.shape[0] - 1, axis=0)[pad_chunks:pad_chunks + n]
    pre = a[pad_chunks:pad_chunks + n] + b_next + pe[0:1, :]
    h = pre * jax.nn.sigmoid(pre)
    o_ref[...] = _dot(h.astype(CDT), w2_ref[...]).astype(o_ref.dtype)


def _compress(qkv, w1, pe, w2):
    rows = qkv.shape[0]
    chunks = rows // CMP_STRIDE
    pad_chunks = PAD_ROWS // CMP_STRIDE
    n_out = chunks - pad_chunks
    G = KV_GROUPS
    x = qkv[:, Q_WIDTH:Q_WIDTH + 2 * KV_WIDTH].reshape(chunks, CMP_STRIDE, 2 * G, HEAD_DIM)
    x = x.transpose(2, 0, 1, 3).reshape(2 * G, chunks, CMP_STRIDE * HEAD_DIM)
    width = CMP_STRIDE * HEAD_DIM
    body = functools.partial(_compress_body, pad_chunks=pad_chunks)
    return pl.pallas_call(
        body,
        grid=(2, G),
        in_specs=[pl.BlockSpec((None, chunks, width), lambda kv, g: (kv * G + g, 0, 0)),
                  pl.BlockSpec((None, 2, width, HEAD_DIM), lambda kv, g: (kv, 0, 0, 0)),
                  pl.BlockSpec((None, 2, width), lambda kv, g: (kv, 0, 0)),
                  pl.BlockSpec((None, HEAD_DIM, HEAD_DIM), lambda kv, g: (kv, 0, 0))],
        out_specs=pl.BlockSpec((None, None, n_out, HEAD_DIM), lambda kv, g: (kv, g, 0, 0)),
        out_shape=jax.ShapeDtypeStruct((2, G, n_out, HEAD_DIM), CDT),
        compiler_params=_cparams("parallel", "parallel"),
        name="compress",
    )(x, w1, pe, w2)


def _rel_bucket_np(dist):
    n = np.maximum(dist, 0)
    max_exact = REL_BUCKETS // 2
    nf = np.maximum(n, 1).astype(np.float32)
    large = max_exact + (np.log(nf / max_exact) / math.log(REL_MAX_DIST / max_exact)
                         * (REL_BUCKETS - max_exact)).astype(np.int32)
    large = np.minimum(large, REL_BUCKETS - 1)
    return np.where(n < max_exact, n, large).astype(np.int32)


def _bias_index_tiles():
    assert BAND_TABLE <= LANES and BAND_BACK % SUBLANES == 0 and BAND_ROWS % SUBLANES == 0
    q = np.arange(QB)[:, None]
    dist = NEAR_KEYS - QB + q - np.arange(NEAR_KEYS)[None, :]
    near = np.where(dist >= 0, _rel_bucket_np(dist), -1)
    dist = WINDOW + q - np.arange(WIN_KEYS)[None, :]
    win = np.where((dist >= 0) & (dist < WINDOW), _rel_bucket_np(dist), -1)
    w = np.arange(LANES)[None, :]
    dist = q - CMP_STRIDE * (w - BAND_BACK) - (CMP_BLOCK - 1)
    band = np.where(dist >= 0, _rel_bucket_np(dist), -1)
    return np.concatenate([near, win, band], axis=1).T.astype(np.int32)


def _bias_body(tbl_ref, idx_ref, o_ref):
    h = pl.program_id(0)
    idx = idx_ref[...]
    far = tbl_ref[REL_BUCKETS - 1, h]
    val = jnp.full(idx.shape, NEG, jnp.float32)
    for b in range(REL_BUCKETS):
        val = jnp.where(idx == b, (tbl_ref[b, h] - far) * LOG2E, val)
    o_ref[...] = val


def _bias_tiles(rel_bias):
    idx = jnp.asarray(_bias_index_tiles())
    width = idx.shape[0]
    H = HEADS_PER_GROUP
    return pl.pallas_call(
        _bias_body,
        grid=(NSA_HEADS,),
        in_specs=[pl.BlockSpec(memory_space=pltpu.SMEM),
                  pl.BlockSpec((width, QB), lambda h: (0, 0))],
        out_specs=pl.BlockSpec((None, width, QB), lambda h: (h // H, 0, h % H)),
        out_shape=jax.ShapeDtypeStruct((KV_GROUPS, width, H * QB), jnp.float32),
        compiler_params=_cparams("arbitrary"),
        name="bias_tiles",
    )(rel_bias, idx)


def _softmax_cols(s):
    m = jnp.max(s, axis=0, keepdims=True)
    p = jnp.exp2(s - m)
    l = jnp.sum(p, axis=0, keepdims=True)
    return p, jnp.where(m > 0.5 * NEG, 1.0 / l, 0.0)


def _split3(x):
    hi = x.astype(CDT)
    r = x - hi.astype(jnp.float32)
    mid = r.astype(CDT)
    lo = (r - mid.astype(jnp.float32)).astype(CDT)
    return hi, mid, lo


def _select_blocks(slc_t, i):
    n_slc = slc_t.shape[0]
    t = i * QB + lax.broadcasted_iota(jnp.int32, (1, QB), 1)
    j_int = lax.broadcasted_iota(jnp.int32, (n_slc, QB), 0)
    j_idx = j_int.astype(jnp.float32)
    cur = t // SLC_BLOCK
    forced = (j_int == 0) | ((cur - j_int >= 0) & (cur - j_int < N_LOCAL_FORCED))
    score = jnp.where(forced, 1e9, jnp.where(j_int > cur, -1e9, slc_t))
    picked = jnp.zeros(score.shape, jnp.bool_)
    for _ in range(min(N_SELECT, n_slc)):
        m = jnp.max(score, axis=0, keepdims=True)
        first = jnp.min(jnp.where(score == m, j_idx, float(n_slc)), axis=0, keepdims=True)
        hit = j_idx == first
        picked = picked | hit
        score = jnp.where(hit, -3e38, score)
    return picked


def _attn_body(q_ref, gate_ref, kc_ref, vct_ref, ks_ref, vst_ref, kw_ref, vwt_ref, blk_ref, tab_ref, wov_ref,
               o_ref, kaug_ref, kwaug_ref, sc_ref, pc_ref, sw_ref, pw_ref, s_ref, p_ref, acc_ref):
    i = pl.program_id(1)
    H = HEADS_PER_GROUP
    n_cmp = kc_ref.shape[0]
    n_slc = wov_ref.shape[0]
    lanes = H * QB
    head = lambda x, h: x[:, h * QB:(h + 1) * QB]

    @pl.when(i == 0)
    def _():
        kaug_ref[:, :HEAD_DIM] = ks_ref[...]
        kaug_ref[:, HEAD_DIM:] = blk_ref[...]
        kwaug_ref[:, :HEAD_DIM] = kw_ref[...]
        row = lax.broadcasted_iota(jnp.int32, (kwaug_ref.shape[0], LANES), 0)
        kwaug_ref[:, HEAD_DIM:] = jnp.where(row < PAD_ROWS, 1.0, 0.0).astype(CDT)

    qb = q_ref[...]
    q_t = jnp.concatenate([qb[:, h * HEAD_DIM:(h + 1) * HEAD_DIM].astype(jnp.float32).T.astype(CDT)
                           for h in range(H)], axis=1)

    def augment(extra):
        return jnp.concatenate([q_t, extra.astype(CDT)], axis=0)

    def values_t(ref, chunk0, n):
        return jnp.concatenate([ref[chunk0 + c] for c in range(n)], axis=1)

    grp = lax.broadcasted_iota(jnp.int32, (LANES, lanes), 0)
    sc_ref[...] = _dot(kc_ref[...], augment(jnp.where(grp > i, NEG, 0.0)))
    band0 = jnp.maximum(CMP_PER_TILE * i - BAND_BACK, 0)
    tab0 = band0 - (CMP_PER_TILE * i - BAND_BACK)
    band_rows = pl.ds(pl.multiple_of(band0, SUBLANES), BAND_ROWS)
    sc_ref[band_rows, :] += tab_ref[pl.ds(pl.multiple_of(NEAR_KEYS + WIN_KEYS + tab0, SUBLANES), BAND_ROWS), :]
    s_c = sc_ref[...]
    m_c = jnp.max(s_c, axis=0, keepdims=True)
    p_c = jnp.exp2(s_c - m_c)
    l_c = jnp.sum(p_c, axis=0, keepdims=True)
    pc_ref[...] = p_c.astype(CDT)
    seen = m_c > 0.5 * NEG
    o_cmp_t = _dot(vct_ref[...], pc_ref[...]) * jnp.where(seen, 1.0 / l_c, 0.0)
    shift_c = jnp.where(seen, m_c + jnp.log2(l_c), -NEG)
    imp_t = functools.reduce(lambda a, b: a + b,
                             [jnp.exp2(sc_ref[:, h * QB:(h + 1) * QB] - head(shift_c, h)) for h in range(H)])

    wov_t = wov_ref[...]
    slc_t = sum(_dot(wov_t, part) for part in _split3(imp_t))
    drop = jnp.where(_select_blocks(slc_t, i), 0.0, 1.0)
    j_row = lax.broadcasted_iota(jnp.int32, (n_slc, QB), 0)
    near_tile = i + 1 - NEAR_KEYS // QB
    drop_far = jnp.where(j_row < near_tile * (QB // SLC_BLOCK), drop, 1.0)
    tile_heads = lambda d: jnp.concatenate([d] * H, axis=1)

    win0 = pl.multiple_of(i * QB, QB)
    first_row = lax.broadcasted_iota(jnp.int32, (LANES, lanes), 0) == 0
    sw_ref[...] = (_dot(kwaug_ref[pl.ds(win0, WIN_KEYS), :], augment(jnp.where(first_row, NEG, 0.0)))
                   + tab_ref[NEAR_KEYS:NEAR_KEYS + WIN_KEYS])
    p_w, scale_w = _softmax_cols(sw_ref[...])
    pw_ref[...] = p_w.astype(CDT)
    o_win_t = _dot(values_t(vwt_ref, win0 // LANES, WIN_KEYS // LANES), pw_ref[...]) * scale_w

    near0 = pl.multiple_of(PAD_ROWS + near_tile * QB, QB)
    s_ref[...] = _dot(kaug_ref[pl.ds(near0, NEAR_KEYS), :], augment(tile_heads(drop))) + tab_ref[:NEAR_KEYS]
    p_ref[...] = jnp.zeros(p_ref.shape, CDT)
    acc_ref[...] = jnp.zeros(acc_ref.shape, jnp.float32)
    q_far = augment(tile_heads(drop_far))

    n_chunks = (kaug_ref.shape[0] - PAD_ROWS) // FAR_CHUNK
    n_far = (jnp.maximum(near_tile, 0) * QB + FAR_CHUNK - 1) // FAR_CHUNK
    far_lanes = FAR_CHUNK // LANES

    def far_step(it, carry):
        a_prev, m, l = carry
        v0 = jnp.where(it == 1, near0 // LANES, PAD_ROWS // LANES + jnp.clip(it - 2, 0, n_chunks - 1) * far_lanes)
        acc_ref[...] = a_prev * acc_ref[...] + _dot(values_t(vst_ref, v0, far_lanes), p_ref[...])
        s_prev = s_ref[...]
        m_new = jnp.maximum(m, jnp.max(s_prev, axis=0, keepdims=True))
        a = jnp.exp2(m - m_new)
        p = jnp.exp2(s_prev - m_new)
        l = a * l + jnp.sum(p, axis=0, keepdims=True)
        p_ref[...] = p.astype(CDT)
        k0 = pl.multiple_of(jnp.where(it < n_far, PAD_ROWS + it * FAR_CHUNK, 0), FAR_CHUNK)
        s_ref[...] = _dot(kaug_ref[pl.ds(k0, FAR_CHUNK), :], q_far)
        return a, m_new, l

    def far_trip(k, carry):
        for u in range(FAR_UNROLL):
            carry = far_step(FAR_UNROLL * k + u, carry)
        return carry

    zero = scale_w - scale_w
    steps = n_far + 2
    trips = steps // FAR_UNROLL
    carry = lax.fori_loop(0, trips, far_trip, (zero + 1.0, zero + 0.1 * NEG, zero))
    for u in range(FAR_UNROLL - 1):
        it = FAR_UNROLL * trips + u
        carry = lax.cond(it < steps, functools.partial(far_step, it), lambda c: c, carry)
    l_s = carry[2]
    o_slc_t = acc_ref[...] * (1.0 / l_s)

    gates_t = gate_ref[...].T
    for h in range(H):
        g = lambda b: gates_t[h * N_BRANCH + b:h * N_BRANCH + b + 1, :]
        o_t = g(0) * head(o_cmp_t, h) + g(1) * head(o_slc_t, h) + g(2) * head(o_win_t, h)
        o_ref[:, h * HEAD_DIM:(h + 1) * HEAD_DIM] = o_t.T.astype(o_ref.dtype)


def _block_membership(rows, n_slc):
    key = np.arange(rows)[:, None] - PAD_ROWS
    j = np.arange(n_slc)[None, :]
    member = np.where(key >= 0, key // SLC_BLOCK == j, True)
    return jnp.asarray(np.where(member, NEG, 0.0), dtype=CDT)


def _overlap_weights(n_cmp, n_slc):
    r = SLC_BLOCK // CMP_STRIDE
    lead = -(-CMP_BLOCK // CMP_STRIDE) - 1
    w = np.zeros((n_cmp, n_slc), np.float32)
    for o in range(-lead, r):
        s0 = o * CMP_STRIDE
        ov = max(0, min(s0 + CMP_BLOCK, SLC_BLOCK) - max(s0, 0))
        for j in range(n_slc):
            n = r * j + o
            if ov > 0 and 0 <= n < n_cmp - 1:
                w[n, j] = ov / CMP_STRIDE
    return jnp.asarray(w.T, dtype=CDT)


def _tile_group_columns(n_cmp):
    n = np.arange(n_cmp)[:, None]
    return jnp.asarray(n // CMP_PER_TILE == np.arange(LANES)[None, :], dtype=CDT)


def _attention(qkv, gates, kvc, tabs):
    rows = qkv.shape[0]
    S = rows - PAD_ROWS
    n_cmp = kvc.shape[2]
    n_slc = S // SLC_BLOCK
    assert n_cmp // CMP_PER_TILE <= LANES
    first = Q_WIDTH // LANES
    G = KV_GROUPS
    lanes = HEADS_PER_GROUP * QB
    kv_spec = lambda which: pl.BlockSpec((rows, LANES), lambda g, i: (0, first + which * G + g))
    vt_spec = lambda which: pl.BlockSpec((None, None, rows // LANES, HEAD_DIM, LANES), lambda g, i: (which, g, 0, 0, 0))
    width = tabs.shape[1]
    kc_aug = jnp.concatenate([kvc[0], jnp.broadcast_to(_tile_group_columns(n_cmp), (G, n_cmp, LANES))], axis=-1)
    vc_t = kvc[1].transpose(0, 2, 1)
    v_cols = jnp.stack([qkv[:, (first + w * G) * LANES:(first + (w + 1) * G) * LANES] for w in (3, 5)])
    v_t = v_cols.reshape(2, rows // LANES, LANES, G, HEAD_DIM).transpose(0, 3, 1, 4, 2)
    return pl.pallas_call(
        _attn_body,
        grid=(G, S // QB),
        in_specs=[pl.BlockSpec((QB, HEADS_PER_GROUP * HEAD_DIM), lambda g, i: (i + PAD_ROWS // QB, g)),
                  pl.BlockSpec((QB, LANES), lambda g, i: (i, g)),
                  pl.BlockSpec((None, n_cmp, HEAD_DIM + LANES), lambda g, i: (g, 0, 0)),
                  pl.BlockSpec((None, HEAD_DIM, n_cmp), lambda g, i: (g, 0, 0)),
                  kv_spec(2), vt_spec(0), kv_spec(4), vt_spec(1),
                  pl.BlockSpec((rows, n_slc), lambda g, i: (0, 0)),
                  pl.BlockSpec((None, width, lanes), lambda g, i: (g, 0, 0)),
                  pl.BlockSpec((n_slc, n_cmp), lambda g, i: (0, 0))],
        out_specs=pl.BlockSpec((QB, HEADS_PER_GROUP * HEAD_DIM), lambda g, i: (i, g)),
        out_shape=jax.ShapeDtypeStruct((S, Q_WIDTH), CDT),
        scratch_shapes=[pltpu.VMEM((rows, HEAD_DIM + n_slc), CDT),
                        pltpu.VMEM((rows, HEAD_DIM + LANES), CDT),
                        pltpu.VMEM((n_cmp, lanes), jnp.float32),
                        pltpu.VMEM((n_cmp, lanes), CDT),
                        pltpu.VMEM((WIN_KEYS, lanes), jnp.float32),
                        pltpu.VMEM((WIN_KEYS, lanes), CDT),
                        pltpu.VMEM((FAR_CHUNK, lanes), jnp.float32),
                        pltpu.VMEM((FAR_CHUNK, lanes), CDT),
                        pltpu.VMEM((HEAD_DIM, lanes), jnp.float32)],
        compiler_params=_cparams("parallel", "arbitrary"),
        name="nsa",
    )(qkv, gates, kc_aug, vc_t, qkv, v_t, qkv, v_t, _block_membership(rows, n_slc), tabs,
      _overlap_weights(n_cmp, n_slc))


def _outproj_body(a1_ref, a2_ref, w1_ref, w2_ref, x_ref, o_ref):
    o_ref[...] = x_ref[...] + _dot(a1_ref[...], w1_ref[...]) + _dot(a2_ref[...], w2_ref[...])


def _outproj(y_nsa, y_pool, w_out, x, bm=512, bn=1024):
    S, D = x.shape
    bn = min(bn, D)
    k1, k2 = y_nsa.shape[1], y_pool.shape[1]
    return pl.pallas_call(
        _outproj_body,
        grid=(S // bm, D // bn),
        in_specs=[pl.BlockSpec((bm, k1), lambda i, j: (i, 0)),
                  pl.BlockSpec((bm, k2), lambda i, j: (i, 0)),
                  pl.BlockSpec((k1, bn), lambda i, j: (0, j)),
                  pl.BlockSpec((k2, bn), lambda i, j: (k1 // k2, j)),
                  pl.BlockSpec((bm, bn), lambda i, j: (i, j))],
        out_specs=pl.BlockSpec((bm, bn), lambda i, j: (i, j)),
        out_shape=jax.ShapeDtypeStruct((S, D), jnp.float32),
        compiler_params=_cparams("parallel", "arbitrary"),
        name="outproj",
    )(y_nsa, y_pool, w_out, w_out, x)


CONV_HALO = 16
FFN_SUBTILES = 2
FFN_K_TILE = 512


def _ffn_up_body(a_ref, halo_ref, wa_ref, wb_ref, cwa_ref, cwb_ref, cba_ref, cbb_ref, o_ref, lhs_ref, *, nb):
    i = pl.program_id(0)
    j = pl.program_id(1)

    @pl.when(j == 0)
    def _():
        lhs_ref[:CONV_HALO] = jnp.where(i > 0, halo_ref[...], jnp.zeros_like(halo_ref))
        lhs_ref[CONV_HALO:] = a_ref[...]

    def conv(lhs, w, cw_ref, cb_ref):
        u = _dot(lhs, w)
        c = cb_ref[...] + cw_ref[CONV_WIDTH - 1:CONV_WIDTH, :] * u[CONV_HALO:]
        for k in range(1, CONV_WIDTH):
            c = c + cw_ref[CONV_WIDTH - 1 - k:CONV_WIDTH - k, :] * pltpu.roll(u, k, axis=0)[CONV_HALO:]
        return c

    @pl.when(j < nb)
    def _():
        wa = wa_ref[...].astype(CDT)
        wb = wb_ref[...].astype(CDT)
        sub = o_ref.shape[0] // FFN_SUBTILES
        for s in range(FFN_SUBTILES):
            lhs = lhs_ref[s * sub:s * sub + CONV_HALO + sub]
            ca = conv(lhs, wa, cwa_ref, cba_ref)
            cb = conv(lhs, wb, cwb_ref, cbb_ref)
            o_ref[s * sub:(s + 1) * sub] = (ca * jax.nn.sigmoid(ca) * cb).astype(o_ref.dtype)

    @pl.when(j >= nb)
    def _():
        o_ref[...] = jnp.zeros_like(o_ref)


def _ffn_up(hn, w_up, conv_w, conv_b, f_pad, bm=1024, bn=256):
    S, D = hn.shape
    F = w_up.shape[1] // 2
    nb = F // bn
    col = lambda j: jnp.minimum(j, nb - 1)
    return pl.pallas_call(
        functools.partial(_ffn_up_body, nb=nb),
        grid=(S // bm, f_pad // bn),
        in_specs=[pl.BlockSpec((bm, D), lambda i, j: (i, 0)),
                  pl.BlockSpec((CONV_HALO, D), lambda i, j: (jnp.maximum(i * (bm // CONV_HALO) - 1, 0), 0)),
                  pl.BlockSpec((D, bn), lambda i, j: (0, col(j))),
                  pl.BlockSpec((D, bn), lambda i, j: (0, nb + col(j))),
                  pl.BlockSpec((CONV_WIDTH, bn), lambda i, j: (0, col(j))),
                  pl.BlockSpec((CONV_WIDTH, bn), lambda i, j: (0, nb + col(j))),
                  pl.BlockSpec((1, bn), lambda i, j: (0, col(j))),
                  pl.BlockSpec((1, bn), lambda i, j: (0, nb + col(j)))],
        out_specs=pl.BlockSpec((bm, bn), lambda i, j: (i, j)),
        out_shape=jax.ShapeDtypeStruct((S, f_pad), CDT),
        scratch_shapes=[pltpu.VMEM((CONV_HALO + bm, D), CDT)],
        compiler_params=_cparams("parallel", "arbitrary"),
        name="ffn_up",
    )(hn, hn, w_up, w_up, conv_w, conv_w, conv_b, conv_b)


def _ffn_down_body(a_ref, w_ref, h_ref, g_ref, o_ref):
    k = pl.program_id(1)

    @pl.when(k == 0)
    def _():
        o_ref[...] = h_ref[...]

    o_ref[...] += _dot(a_ref[...], w_ref[...])

    @pl.when(k == pl.num_programs(1) - 1)
    def _():
        x = o_ref[...]
        y = x * lax.rsqrt(jnp.mean(x * x, axis=-1, keepdims=True) + RMS_EPS)
        o_ref[...] = y * g_ref[...]


def _ffn_down(act, w_down, h, g, bm=512, bk=FFN_K_TILE):
    S, F = act.shape
    D = h.shape[1]
    return pl.pallas_call(
        _ffn_down_body,
        grid=(S // bm, F // bk),
        in_specs=[pl.BlockSpec((bm, bk), lambda i, k: (i, k)),
                  pl.BlockSpec((bk, D), lambda i, k: (k, 0)),
                  pl.BlockSpec((bm, D), lambda i, k: (i, 0)),
                  pl.BlockSpec((1, D), lambda i, k: (0, 0))],
        out_specs=pl.BlockSpec((bm, D), lambda i, k: (i, 0)),
        out_shape=jax.ShapeDtypeStruct((S, D), jnp.float32),
        compiler_params=_cparams("parallel", "arbitrary"),
        name="ffn_down",
    )(act, w_down, h, g.reshape(1, D))


def _gate_weight(w_gate_t):
    D = w_gate_t.shape[1]
    per = HEADS_PER_GROUP * N_BRANCH
    w = w_gate_t.reshape(KV_GROUPS, per, D)
    w = jnp.pad(w, ((0, 0), (0, LANES - per), (0, 0)))
    return w.reshape(KV_GROUPS * LANES, D)


def _nsa_branch(hn, w_in, w_gate, cmp_w1_k, cmp_pe_k, cmp_w2_k, cmp_w1_v, cmp_pe_v, cmp_w2_v, rel_bias):
    qkv_cols = Q_WIDTH + 6 * KV_WIDTH
    w_gate = _gate_weight(w_gate)
    bm = 512
    qkv = _proj(hn, w_in, CDT, col0=POOL_WIDTH, n_cols=qkv_cols, pad_tiles=PAD_ROWS // bm, scaled_cols=Q_WIDTH,
                scale=HEAD_DIM ** -0.5 * LOG2E, bm=bm, name="proj_qkv")
    gates = _proj(hn, w_gate, jnp.float32, sigmoid=True, bm=bm, name="proj_gate")
    w1 = jnp.stack([cmp_w1_k, cmp_w1_v]).astype(CDT).reshape(2, 2, CMP_STRIDE * HEAD_DIM, HEAD_DIM)
    pe = jnp.stack([cmp_pe_k, cmp_pe_v]).astype(CDT).reshape(2, 2, CMP_STRIDE * HEAD_DIM)
    w2 = jnp.stack([cmp_w2_k, cmp_w2_v]).astype(CDT)
    kvc = _compress(qkv, w1, pe, w2)
    tabs = _bias_tiles(rel_bias)
    return _attention(qkv, gates, kvc, tabs)


def kernel(x, norm_mix_g, w_in, pool_w, pool_scale, cmp_pe_k, cmp_w1_k, cmp_w2_k, cmp_pe_v, cmp_w1_v, cmp_w2_v,
           rel_bias, w_out, norm_ffn_g, w_up, conv_w, conv_b, w_down, norm_final_g):
    B, S, D = x.shape
    assert B == 1 and w_in.shape[0] == 1, "single sequence, single layer"
    h = x.reshape(S, D)
    hn = _rmsnorm(h, norm_mix_g[0], CDT)
    n_main = POOL_WIDTH + Q_WIDTH + 6 * KV_WIDTH
    w_in_t = w_in.transpose(0, 2, 1)
    w_all = _cast_weight(w_in_t, 0, n_main, D, bm=512)
    u_pool = _proj(hn, w_all, jnp.float32, n_cols=POOL_WIDTH, name="proj_pool")
    y_pool = _pool(u_pool, pool_w[0], pool_scale[0])
    y_nsa = _nsa_branch(hn, w_all, w_in_t[0, n_main:], cmp_w1_k[0], cmp_pe_k[0], cmp_w2_k[0],
                        cmp_w1_v[0], cmp_pe_v[0], cmp_w2_v[0], rel_bias)
    h = _outproj(y_nsa, y_pool, w_out[0].astype(CDT), h)
    hn = _rmsnorm(h, norm_ffn_g[0], CDT)
    F = w_down.shape[1]
    f_pad = -(-F // FFN_K_TILE) * FFN_K_TILE
    act = _ffn_up(hn, w_up[0], conv_w[0], conv_b[0].reshape(1, -1), f_pad)
    out = _ffn_down(act, _cast_weight(w_down, 0, f_pad, D), h, norm_final_g)
    return out.reshape(B, S, D)
```

```python
import functools
import math

import numpy as np
import jax
import jax.numpy as jnp
from jax import lax
from jax.experimental import pallas as pl
from jax.experimental.pallas import tpu as pltpu

POOL_WINDOWS = (2, 4, 8, 16)
POOL_GROUP = 256
POOL_WIDTH = POOL_GROUP * len(POOL_WINDOWS)
HEAD_DIM = 128
KV_GROUPS = 4
HEADS_PER_GROUP = 6
NSA_HEADS = KV_GROUPS * HEADS_PER_GROUP
Q_WIDTH = NSA_HEADS * HEAD_DIM
KV_WIDTH = KV_GROUPS * HEAD_DIM
CMP_BLOCK = 32
CMP_STRIDE = 16
SLC_BLOCK = 64
N_SELECT = 16
N_LOCAL_FORCED = 2
WINDOW = 512
N_BRANCH = 3
REL_BUCKETS = 32
REL_MAX_DIST = 128
CONV_WIDTH = 3
RMS_EPS = 1e-6
NEG = -1e30
LOG2E = math.log2(math.e)

LANES = 128
SUBLANES = 8
VMEM_LIMIT_BYTES = 56 * 1024 * 1024

CDT = jnp.bfloat16
QB = 128
PAD_ROWS = WINDOW
FAR_CHUNK = 512
NEAR_KEYS = FAR_CHUNK
FAR_UNROLL = 2
WIN_KEYS = WINDOW + QB
CMP_PER_TILE = QB // CMP_STRIDE
BAND_BACK = 16
BAND_ROWS = 24
BAND_TABLE = BAND_ROWS + BAND_BACK


def _cparams(*sem, flags=None):
    return pltpu.CompilerParams(dimension_semantics=sem, vmem_limit_bytes=VMEM_LIMIT_BYTES, flags=flags)


def _dot(a, b):
    return jnp.dot(a, b, preferred_element_type=jnp.float32)


def _dot_nt(a, b):
    return lax.dot_general(a, b, (((1,), (1,)), ((), ())), preferred_element_type=jnp.float32)


def _cast_body(x_ref, o_ref, *, rows_valid):
    x = x_ref[...]
    row = pl.program_id(0) * x.shape[0] + lax.broadcasted_iota(jnp.int32, x.shape, 0)
    o_ref[...] = jnp.where(row < rows_valid, x, 0.0).astype(o_ref.dtype)


def _cast_weight(w, layer, n_rows, n_cols, bm=256, bn=4096):
    rows = w.shape[1]
    bm, bn = min(bm, rows), min(bn, n_cols)
    assert n_rows % bm == 0 and n_cols % bn == 0 and (rows % bm == 0 or n_rows <= rows // bm * bm)
    last = rows // bm - 1
    return pl.pallas_call(
        functools.partial(_cast_body, rows_valid=rows),
        grid=(n_rows // bm, n_cols // bn),
        in_specs=[pl.BlockSpec((None, bm, bn), lambda i, j: (layer, jnp.minimum(i, last), j))],
        out_specs=pl.BlockSpec((bm, bn), lambda i, j: (i, j)),
        out_shape=jax.ShapeDtypeStruct((n_rows, n_cols), CDT),
        compiler_params=_cparams("parallel", "parallel"),
        name="cast_weight",
    )(w)


def _rmsnorm_body(x_ref, g_ref, o_ref):
    x = x_ref[...]
    y = x * lax.rsqrt(jnp.mean(x * x, axis=-1, keepdims=True) + RMS_EPS)
    o_ref[...] = (y * g_ref[...]).astype(o_ref.dtype)


def _rmsnorm(x, g, out_dtype, bm=256):
    S, D = x.shape
    return pl.pallas_call(
        _rmsnorm_body,
        grid=(S // bm,),
        in_specs=[pl.BlockSpec((bm, D), lambda i: (i, 0)), pl.BlockSpec((1, D), lambda i: (0, 0))],
        out_specs=pl.BlockSpec((bm, D), lambda i: (i, 0)),
        out_shape=jax.ShapeDtypeStruct((S, D), out_dtype),
        compiler_params=_cparams("parallel"),
        name="rmsnorm",
    )(x, g.reshape(1, D))


def _proj_body(a_ref, w_ref, o_ref, *, pad_tiles, n_scaled, scale, sigmoid):
    i = pl.program_id(0)
    j = pl.program_id(1)

    @pl.when(i < pad_tiles)
    def _():
        o_ref[...] = jnp.zeros_like(o_ref)

    @pl.when(i >= pad_tiles)
    def _():
        r = _dot_nt(a_ref[...], w_ref[...].astype(a_ref.dtype))
        if n_scaled:
            r = r * jnp.where(j < n_scaled, jnp.float32(scale), jnp.float32(1.0))
        if sigmoid:
            r = jax.nn.sigmoid(r)
        o_ref[...] = r.astype(o_ref.dtype)


def _proj(a, w_t, out_dtype, *, col0=0, n_cols=None, pad_tiles=0, scaled_cols=0, scale=1.0, sigmoid=False,
          bm=512, bn=1024, name="proj"):
    S, D = a.shape
    N = w_t.shape[0] if n_cols is None else n_cols
    bn = min(bn, N)
    assert scaled_cols % bn == 0 and col0 % bn == 0 and N % bn == 0
    n_scaled = scaled_cols // bn
    first = col0 // bn
    body = functools.partial(_proj_body, pad_tiles=pad_tiles, n_scaled=n_scaled, scale=scale, sigmoid=sigmoid)
    return pl.pallas_call(
        body,
        grid=(S // bm + pad_tiles, N // bn),
        in_specs=[pl.BlockSpec((bm, D), lambda i, j: (jnp.maximum(i - pad_tiles, 0), 0)),
                  pl.BlockSpec((bn, D), lambda i, j: (first + j, 0))],
        out_specs=pl.BlockSpec((bm, bn), lambda i, j: (i, j)),
        out_shape=jax.ShapeDtypeStruct((S + pad_tiles * bm, N), out_dtype),
        compiler_params=_cparams("parallel", "arbitrary"),
        name=name,
    )(a, w_t)


POOL_HALO = 16


def _pool_body(u_ref, halo_ref, w_ref, s_ref, o_ref):
    i = pl.program_id(0)
    bm = u_ref.shape[0]
    u = u_ref[...]
    halo = jnp.where(i > 0, halo_ref[...], 0.0)
    ext = jnp.concatenate([halo, u], axis=0)
    t = i * bm + lax.broadcasted_iota(jnp.int32, (bm, 1), 0)
    acc = ext
    sums = {}
    shift = 1
    while shift < POOL_WINDOWS[-1]:
        acc = acc + pltpu.roll(acc, shift, axis=0)
        shift *= 2
        sums[shift] = acc
    for gi, w in enumerate(POOL_WINDOWS):
        cols = slice(gi * POOL_GROUP, (gi + 1) * POOL_GROUP)
        cnt = jnp.minimum(t + 1, w).astype(jnp.float32)
        d = sums[w][POOL_HALO:, cols] / cnt - u[:, cols]
        y = _dot(d.astype(CDT), w_ref[gi])
        o_ref[:, cols] = (y * s_ref[:, cols]).astype(o_ref.dtype)


def _pool(u, pool_w, pool_scale, bm=512):
    S = u.shape[0]
    return pl.pallas_call(
        _pool_body,
        grid=(S // bm,),
        in_specs=[pl.BlockSpec((bm, POOL_WIDTH), lambda i: (i, 0)),
                  pl.BlockSpec((POOL_HALO, POOL_WIDTH), lambda i: (jnp.maximum(i * (bm // POOL_HALO) - 1, 0), 0)),
                  pl.BlockSpec((len(POOL_WINDOWS), POOL_GROUP, POOL_GROUP), lambda i: (0, 0, 0)),
                  pl.BlockSpec((1, POOL_WIDTH), lambda i: (0, 0))],
        out_specs=pl.BlockSpec((bm, POOL_WIDTH), lambda i: (i, 0)),
        out_shape=jax.ShapeDtypeStruct((S, POOL_WIDTH), CDT),
        compiler_params=_cparams("parallel"),
        name="pool",
    )(u, u, pool_w.astype(CDT), pool_scale.reshape(1, POOL_WIDTH))


def _compress_body(x_ref, w1_ref, pe_ref, w2_ref, o_ref, *, pad_chunks):
    x = x_ref[...]
    w_lo = w1_ref[0]
    w_hi = w1_ref[1]
    a = _dot(x, w_lo)
    b = _dot(x, w_hi)
    pe = _dot(jnp.broadcast_to(pe_ref[0:1], (SUBLANES, pe_ref.shape[1])), w_lo) \
        + _dot(jnp.broadcast_to(pe_ref[1:2], (SUBLANES, pe_ref.shape[1])), w_hi)
    n = o_ref.shape[0]
    b_next = pltpu.roll(b, b.shape[0] - 1, axis=0)[pad_chunks:pad_chunks + n]
    pre = a[pad_chunks:pad_chunks + n] + b_next + pe[0:1, :]
    h = pre * jax.nn.sigmoid(pre)
    o_ref[...] = _dot(h.astype(CDT), w2_ref[...]).astype(o_ref.dtype)


def _compress(qkv, w1, pe, w2):
    rows = qkv.shape[0]
    chunks = rows // CMP_STRIDE
    pad_chunks = PAD_ROWS // CMP_STRIDE
    n_out = chunks - pad_chunks
    G = KV_GROUPS
    x = qkv[:, Q_WIDTH:Q_WIDTH + 2 * KV_WIDTH].reshape(chunks, CMP_STRIDE, 2 * G, HEAD_DIM)
    x = x.transpose(2, 0, 1, 3).reshape(2 * G, chunks, CMP_STRIDE * HEAD_DIM)
    width = CMP_STRIDE * HEAD_DIM
    body = functools.partial(_compress_body, pad_chunks=pad_chunks)
    return pl.pallas_call(
        body,
        grid=(2, G),
        in_specs=[pl.BlockSpec((None, chunks, width), lambda kv, g: (kv * G + g, 0, 0)),
                  pl.BlockSpec((None, 2, width, HEAD_DIM), lambda kv, g: (kv, 0, 0, 0)),
                  pl.BlockSpec((None, 2, width), lambda kv, g: (kv, 0, 0)),
                  pl.BlockSpec((None, HEAD_DIM, HEAD_DIM), lambda kv, g: (kv, 0, 0))],
        out_specs=pl.BlockSpec((None, None, n_out, HEAD_DIM), lambda kv, g: (kv, g, 0, 0)),
        out_shape=jax.ShapeDtypeStruct((2, G, n_out, HEAD_DIM), CDT),
        compiler_params=_cparams("parallel", "parallel"),
        name="compress",
    )(x, w1, pe, w2)


def _rel_bucket_np(dist):
    n = np.maximum(dist, 0)
    max_exact = REL_BUCKETS // 2
    nf = np.maximum(n, 1).astype(np.float32)
    large = max_exact + (np.log(nf / max_exact) / math.log(REL_MAX_DIST / max_exact)
                         * (REL_BUCKETS - max_exact)).astype(np.int32)
    large = np.minimum(large, REL_BUCKETS - 1)
    return np.where(n < max_exact, n, large).astype(np.int32)


def _bias_index_tiles():
    assert BAND_TABLE <= LANES and BAND_BACK % SUBLANES == 0 and BAND_ROWS % SUBLANES == 0
    q = np.arange(QB)[:, None]
    dist = NEAR_KEYS - QB + q - np.arange(NEAR_KEYS)[None, :]
    near = np.where(dist >= 0, _rel_bucket_np(dist), -1)
    dist = WINDOW + q - np.arange(WIN_KEYS)[None, :]
    win = np.where((dist >= 0) & (dist < WINDOW), _rel_bucket_np(dist), -1)
    w = np.arange(LANES)[None, :]
    dist = q - CMP_STRIDE * (w - BAND_BACK) - (CMP_BLOCK - 1)
    band = np.where(dist >= 0, _rel_bucket_np(dist), -1)
    return np.concatenate([near, win, band], axis=1).T.astype(np.int32)


def _bias_body(tbl_ref, idx_ref, o_ref):
    h = pl.program_id(0)
    idx = idx_ref[...]
    far = tbl_ref[REL_BUCKETS - 1, h]
    val = jnp.full(idx.shape, NEG, jnp.float32)
    for b in range(REL_BUCKETS):
        val = jnp.where(idx == b, (tbl_ref[b, h] - far) * LOG2E, val)
    o_ref[...] = val


def _bias_tiles(rel_bias):
    idx = jnp.asarray(_bias_index_tiles())
    width = idx.shape[0]
    H = HEADS_PER_GROUP
    return pl.pallas_call(
        _bias_body,
        grid=(NSA_HEADS,),
        in_specs=[pl.BlockSpec(memory_space=pltpu.SMEM),
                  pl.BlockSpec((width, QB), lambda h: (0, 0))],
        out_specs=pl.BlockSpec((None, width, QB), lambda h: (h // H, 0, h % H)),
        out_shape=jax.ShapeDtypeStruct((KV_GROUPS, width, H * QB), jnp.float32),
        compiler_params=_cparams("arbitrary"),
        name="bias_tiles",
    )(rel_bias, idx)


def _softmax_cols(s):
    m = jnp.max(s, axis=0, keepdims=True)
    p = jnp.exp2(s - m)
    l = jnp.sum(p, axis=0, keepdims=True)
    return p, jnp.where(m > 0.5 * NEG, 1.0 / l, 0.0)


def _split3(x):
    hi = x.astype(CDT)
    r = x - hi.astype(jnp.float32)
    mid = r.astype(CDT)
    lo = (r - mid.astype(jnp.float32)).astype(CDT)
    return hi, mid, lo


def _select_blocks(slc_t, i):
    n_slc = slc_t.shape[0]
    t = i * QB + lax.broadcasted_iota(jnp.int32, (1, QB), 1)
    j_int = lax.broadcasted_iota(jnp.int32, (n_slc, QB), 0)
    j_idx = j_int.astype(jnp.float32)
    cur = t // SLC_BLOCK
    forced = (j_int == 0) | ((cur - j_int >= 0) & (cur - j_int < N_LOCAL_FORCED))
    score = jnp.where(forced, 1e9, jnp.where(j_int > cur, -1e9, slc_t))
    picked = jnp.zeros(score.shape, jnp.bool_)
    for _ in range(min(N_SELECT, n_slc)):
        m = jnp.max(score, axis=0, keepdims=True)
        first = jnp.min(jnp.where(score == m, j_idx, float(n_slc)), axis=0, keepdims=True)
        hit = j_idx == first
        picked = picked | hit
        score = jnp.where(hit, -3e38, score)
    return picked


def _attn_body(q_ref, gate_ref, kc_ref, vct_ref, ks_ref, vst_ref, kw_ref, vwt_ref, blk_ref, tab_ref, wov_ref,
               o_ref, kaug_ref, kwaug_ref, sc_ref, pc_ref, sw_ref, pw_ref, s_ref, p_ref, acc_ref):
    i = pl.program_id(1)
    H = HEADS_PER_GROUP
    n_cmp = kc_ref.shape[0]
    n_slc = wov_ref.shape[0]
    lanes = H * QB
    head = lambda x, h: x[:, h * QB:(h + 1) * QB]

    @pl.when(i == 0)
    def _():
        kaug_ref[:, :HEAD_DIM] = ks_ref[...]
        kaug_ref[:, HEAD_DIM:] = blk_ref[...]
        kwaug_ref[:, :HEAD_DIM] = kw_ref[...]
        row = lax.broadcasted_iota(jnp.int32, (kwaug_ref.shape[0], LANES), 0)
        kwaug_ref[:, HEAD_DIM:] = jnp.where(row < PAD_ROWS, 1.0, 0.0).astype(CDT)

    qb = q_ref[...]
    q_t = jnp.concatenate([qb[:, h * HEAD_DIM:(h + 1) * HEAD_DIM].astype(jnp.float32).T.astype(CDT)
                           for h in range(H)], axis=1)

    def augment(extra):
        return jnp.concatenate([q_t, extra.astype(CDT)], axis=0)

    def values_t(ref, chunk0, n):
        return jnp.concatenate([ref[chunk0 + c] for c in range(n)], axis=1)

    grp = lax.broadcasted_iota(jnp.int32, (LANES, lanes), 0)
    sc_ref[...] = _dot(kc_ref[...], augment(jnp.where(grp > i, NEG, 0.0)))
    band0 = jnp.maximum(CMP_PER_TILE * i - BAND_BACK, 0)
    tab0 = band0 - (CMP_PER_TILE * i - BAND_BACK)
    band_rows = pl.ds(pl.multiple_of(band0, SUBLANES), BAND_ROWS)
    sc_ref[band_rows, :] += tab_ref[pl.ds(pl.multiple_of(NEAR_KEYS + WIN_KEYS + tab0, SUBLANES), BAND_ROWS), :]
    s_c = sc_ref[...]
    m_c = jnp.max(s_c, axis=0, keepdims=True)
    p_c = jnp.exp2(s_c - m_c)
    l_c = jnp.sum(p_c, axis=0, keepdims=True)
    pc_ref[...] = p_c.astype(CDT)
    seen = m_c > 0.5 * NEG
    o_cmp_t = _dot(vct_ref[...], pc_ref[...]) * jnp.where(seen, 1.0 / l_c, 0.0)
    shift_c = jnp.where(seen, m_c + jnp.log2(l_c), -NEG)
    imp_t = functools.reduce(lambda a, b: a + b,
                             [jnp.exp2(sc_ref[:, h * QB:(h + 1) * QB] - head(shift_c, h)) for h in range(H)])

    wov_t = wov_ref[...]
    slc_t = sum(_dot(wov_t, part) for part in _split3(imp_t))
    drop = jnp.where(_select_blocks(slc_t, i), 0.0, 1.0)
    j_row = lax.broadcasted_iota(jnp.int32, (n_slc, QB), 0)
    near_tile = i + 1 - NEAR_KEYS // QB
    drop_far = jnp.where(j_row < near_tile * (QB // SLC_BLOCK), drop, 1.0)
    tile_heads = lambda d: jnp.concatenate([d] * H, axis=1)

    win0 = pl.multiple_of(i * QB, QB)
    first_row = lax.broadcasted_iota(jnp.int32, (LANES, lanes), 0) == 0
    sw_ref[...] = (_dot(kwaug_ref[pl.ds(win0, WIN_KEYS), :], augment(jnp.where(first_row, NEG, 0.0)))
                   + tab_ref[NEAR_KEYS:NEAR_KEYS + WIN_KEYS])
    p_w, scale_w = _softmax_cols(sw_ref[...])
    pw_ref[...] = p_w.astype(CDT)
    o_win_t = _dot(values_t(vwt_ref, win0 // LANES, WIN_KEYS // LANES), pw_ref[...]) * scale_w

    near0 = pl.multiple_of(PAD_ROWS + near_tile * QB, QB)
    s_ref[...] = _dot(kaug_ref[pl.ds(near0, NEAR_KEYS), :], augment(tile_heads(drop))) + tab_ref[:NEAR_KEYS]
    p_ref[...] = jnp.zeros(p_ref.shape, CDT)
    acc_ref[...] = jnp.zeros(acc_ref.shape, jnp.float32)
    q_far = augment(tile_heads(drop_far))

    n_chunks = (kaug_ref.shape[0] - PAD_ROWS) // FAR_CHUNK
    n_far = (jnp.maximum(near_tile, 0) * QB + FAR_CHUNK - 1) // FAR_CHUNK
    far_lanes = FAR_CHUNK // LANES

    def far_step(it, carry):
        a_prev, m, l = carry
        v0 = jnp.where(it == 1, near0 // LANES, PAD_ROWS // LANES + jnp.clip(it - 2, 0, n_chunks - 1) * far_lanes)
        acc_ref[...] = a_prev * acc_ref[...] + _dot(values_t(vst_ref, v0, far_lanes), p_ref[...])
        s_prev = s_ref[...]
        m_new = jnp.maximum(m, jnp.max(s_prev, axis=0, keepdims=True))
        a = jnp.exp2(m - m_new)
        p = jnp.exp2(s_prev - m_new)
        l = a * l + jnp.sum(p, axis=0, keepdims=True)
        p_ref[...] = p.astype(CDT)
        k0 = pl.multiple_of(jnp.where(it < n_far, PAD_ROWS + it * FAR_CHUNK, 0), FAR_CHUNK)
        s_ref[...] = _dot(kaug_ref[pl.ds(k0, FAR_CHUNK), :], q_far)
        return a, m_new, l

    def far_trip(k, carry):
        for u in range(FAR_UNROLL):
            carry = far_step(FAR_UNROLL * k + u, carry)
        return carry

    zero = scale_w - scale_w
    steps = n_far + 2
    trips = steps // FAR_UNROLL
    carry = lax.fori_loop(0, trips, far_trip, (zero + 1.0, zero + 0.1 * NEG, zero))
    for u in range(FAR_UNROLL - 1):
        it = FAR_UNROLL * trips + u
        carry = lax.cond(it < steps, functools.partial(far_step, it), lambda c: c, carry)
    l_s = carry[2]
    o_slc_t = acc_ref[...] * (1.0 / l_s)

    gates_t = gate_ref[...].T
    for h in range(H):
        g = lambda b: gates_t[h * N_BRANCH + b:h * N_BRANCH + b + 1, :]
        o_t = g(0) * head(o_cmp_t, h) + g(1) * head(o_slc_t, h) + g(2) * head(o_win_t, h)
        o_ref[:, h * HEAD_DIM:(h + 1) * HEAD_DIM] = o_t.T.astype(o_ref.dtype)


def _block_membership(rows, n_slc):
    key = np.arange(rows)[:, None] - PAD_ROWS
    j = np.arange(n_slc)[None, :]
    member = np.where(key >= 0, key // SLC_BLOCK == j, True)
    return jnp.asarray(np.where(member, NEG, 0.0), dtype=CDT)


def _overlap_weights(n_cmp, n_slc):
    r = SLC_BLOCK // CMP_STRIDE
    lead = -(-CMP_BLOCK // CMP_STRIDE) - 1
    w = np.zeros((n_cmp, n_slc), np.float32)
    for o in range(-lead, r):
        s0 = o * CMP_STRIDE
        ov = max(0, min(s0 + CMP_BLOCK, SLC_BLOCK) - max(s0, 0))
        for j in range(n_slc):
            n = r * j + o
            if ov > 0 and 0 <= n < n_cmp - 1:
                w[n, j] = ov / CMP_STRIDE
    return jnp.asarray(w.T, dtype=CDT)


def _tile_group_columns(n_cmp):
    n = np.arange(n_cmp)[:, None]
    return jnp.asarray(n // CMP_PER_TILE == np.arange(LANES)[None, :], dtype=CDT)


def _attention(qkv, gates, kvc, tabs):
    rows = qkv.shape[0]
    S = rows - PAD_ROWS
    n_cmp = kvc.shape[2]
    n_slc = S // SLC_BLOCK
    assert n_cmp // CMP_PER_TILE <= LANES
    first = Q_WIDTH // LANES
    G = KV_GROUPS
    lanes = HEADS_PER_GROUP * QB
    kv_spec = lambda which: pl.BlockSpec((rows, LANES), lambda g, i: (0, first + which * G + g))
    vt_spec = lambda which: pl.BlockSpec((None, None, rows // LANES, HEAD_DIM, LANES), lambda g, i: (which, g, 0, 0, 0))
    width = tabs.shape[1]
    kc_aug = jnp.concatenate([kvc[0], jnp.broadcast_to(_tile_group_columns(n_cmp), (G, n_cmp, LANES))], axis=-1)
    vc_t = kvc[1].transpose(0, 2, 1)
    v_cols = jnp.stack([qkv[:, (first + w * G) * LANES:(first + (w + 1) * G) * LANES] for w in (3, 5)])
    v_t = v_cols.reshape(2, rows // LANES, LANES, G, HEAD_DIM).transpose(0, 3, 1, 4, 2)
    return pl.pallas_call(
        _attn_body,
        grid=(G, S // QB),
        in_specs=[pl.BlockSpec((QB, HEADS_PER_GROUP * HEAD_DIM), lambda g, i: (i + PAD_ROWS // QB, g)),
                  pl.BlockSpec((QB, LANES), lambda g, i: (i, g)),
                  pl.BlockSpec((None, n_cmp, HEAD_DIM + LANES), lambda g, i: (g, 0, 0)),
                  pl.BlockSpec((None, HEAD_DIM, n_cmp), lambda g, i: (g, 0, 0)),
                  kv_spec(2), vt_spec(0), kv_spec(4), vt_spec(1),
                  pl.BlockSpec((rows, n_slc), lambda g, i: (0, 0)),
                  pl.BlockSpec((None, width, lanes), lambda g, i: (g, 0, 0)),
                  pl.BlockSpec((n_slc, n_cmp), lambda g, i: (0, 0))],
        out_specs=pl.BlockSpec((QB, HEADS_PER_GROUP * HEAD_DIM), lambda g, i: (i, g)),
        out_shape=jax.ShapeDtypeStruct((S, Q_WIDTH), CDT),
        scratch_shapes=[pltpu.VMEM((rows, HEAD_DIM + n_slc), CDT),
                        pltpu.VMEM((rows, HEAD_DIM + LANES), CDT),
                        pltpu.VMEM((n_cmp, lanes), jnp.float32),
                        pltpu.VMEM((n_cmp, lanes), CDT),
                        pltpu.VMEM((WIN_KEYS, lanes), jnp.float32),
                        pltpu.VMEM((WIN_KEYS, lanes), CDT),
                        pltpu.VMEM((FAR_CHUNK, lanes), jnp.float32),
                        pltpu.VMEM((FAR_CHUNK, lanes), CDT),
                        pltpu.VMEM((HEAD_DIM, lanes), jnp.float32)],
        compiler_params=_cparams("parallel", "arbitrary"),
        name="nsa",
    )(qkv, gates, kc_aug, vc_t, qkv, v_t, qkv, v_t, _block_membership(rows, n_slc), tabs,
      _overlap_weights(n_cmp, n_slc))


def _outproj_body(a1_ref, a2_ref, w1_ref, w2_ref, x_ref, o_ref):
    o_ref[...] = x_ref[...] + _dot(a1_ref[...], w1_ref[...]) + _dot(a2_ref[...], w2_ref[...])


def _outproj(y_nsa, y_pool, w_out, x, bm=512, bn=1024):
    S, D = x.shape
    bn = min(bn, D)
    k1, k2 = y_nsa.shape[1], y_pool.shape[1]
    return pl.pallas_call(
        _outproj_body,
        grid=(S // bm, D // bn),
        in_specs=[pl.BlockSpec((bm, k1), lambda i, j: (i, 0)),
                  pl.BlockSpec((bm, k2), lambda i, j: (i, 0)),
                  pl.BlockSpec((k1, bn), lambda i, j: (0, j)),
                  pl.BlockSpec((k2, bn), lambda i, j: (k1 // k2, j)),
                  pl.BlockSpec((bm, bn), lambda i, j: (i, j))],
        out_specs=pl.BlockSpec((bm, bn), lambda i, j: (i, j)),
        out_shape=jax.ShapeDtypeStruct((S, D), jnp.float32),
        compiler_params=_cparams("parallel", "arbitrary"),
        name="outproj",
    )(y_nsa, y_pool, w_out, w_out, x)


CONV_HALO = 16
FFN_SUBTILES = 4
FFN_K_TILE = 512


def _ffn_up_body(a_ref, halo_ref, wa_ref, wb_ref, cwa_ref, cwb_ref, cba_ref, cbb_ref, o_ref, lhs_ref, *, nb):
    i = pl.program_id(0)
    j = pl.program_id(1)

    @pl.when(j == 0)
    def _():
        lhs_ref[:CONV_HALO] = jnp.where(i > 0, halo_ref[...], jnp.zeros_like(halo_ref))
        lhs_ref[CONV_HALO:] = a_ref[...]

    def conv(lhs, w, cw_ref, cb_ref):
        u = _dot(lhs, w)
        c = cb_ref[...] + cw_ref[CONV_WIDTH - 1:CONV_WIDTH, :] * u[CONV_HALO:]
        for k in range(1, CONV_WIDTH):
            c = c + cw_ref[CONV_WIDTH - 1 - k:CONV_WIDTH - k, :] * pltpu.roll(u, k, axis=0)[CONV_HALO:]
        return c

    @pl.when(j < nb)
    def _():
        wa = wa_ref[...].astype(CDT)
        wb = wb_ref[...].astype(CDT)
        sub = o_ref.shape[0] // FFN_SUBTILES
        for s in range(FFN_SUBTILES):
            lhs = lhs_ref[s * sub:s * sub + CONV_HALO + sub]
            ca = conv(lhs, wa, cwa_ref, cba_ref)
            cb = conv(lhs, wb, cwb_ref, cbb_ref)
            o_ref[s * sub:(s + 1) * sub] = (ca * jax.nn.sigmoid(ca) * cb).astype(o_ref.dtype)

    @pl.when(j >= nb)
    def _():
        o_ref[...] = jnp.zeros_like(o_ref)


def _ffn_up(hn, w_up, conv_w, conv_b, f_pad, bm=2048, bn=256):
    S, D = hn.shape
    F = w_up.shape[1] // 2
    nb = F // bn
    col = lambda j: jnp.minimum(j, nb - 1)
    return pl.pallas_call(
        functools.partial(_ffn_up_body, nb=nb),
        grid=(S // bm, f_pad // bn),
        in_specs=[pl.BlockSpec((bm, D), lambda i, j: (i, 0), pipeline_mode=pl.Buffered(1)),
                  pl.BlockSpec((CONV_HALO, D), lambda i, j: (jnp.maximum(i * (bm // CONV_HALO) - 1, 0), 0)),
                  pl.BlockSpec((D, bn), lambda i, j: (0, col(j))),
                  pl.BlockSpec((D, bn), lambda i, j: (0, nb + col(j))),
                  pl.BlockSpec((CONV_WIDTH, bn), lambda i, j: (0, col(j))),
                  pl.BlockSpec((CONV_WIDTH, bn), lambda i, j: (0, nb + col(j))),
                  pl.BlockSpec((1, bn), lambda i, j: (0, col(j))),
                  pl.BlockSpec((1, bn), lambda i, j: (0, nb + col(j)))],
        out_specs=pl.BlockSpec((bm, bn), lambda i, j: (i, j)),
        out_shape=jax.ShapeDtypeStruct((S, f_pad), CDT),
        scratch_shapes=[pltpu.VMEM((CONV_HALO + bm, D), CDT)],
        compiler_params=_cparams("parallel", "arbitrary"),
        name="ffn_up",
    )(hn, hn, w_up, w_up, conv_w, conv_w, conv_b, conv_b)


def _ffn_down_body(a_ref, w_ref, h_ref, g_ref, o_ref):
    k = pl.program_id(1)

    @pl.when(k == 0)
    def _():
        o_ref[...] = h_ref[...]

    o_ref[...] += _dot(a_ref[...], w_ref[...])

    @pl.when(k == pl.num_programs(1) - 1)
    def _():
        x = o_ref[...]
        y = x * lax.rsqrt(jnp.mean(x * x, axis=-1, keepdims=True) + RMS_EPS)
        o_ref[...] = y * g_ref[...]


def _ffn_down(act, w_down, h, g, bm=512, bk=FFN_K_TILE):
    S, F = act.shape
    D = h.shape[1]
    return pl.pallas_call(
        _ffn_down_body,
        grid=(S // bm, F // bk),
        in_specs=[pl.BlockSpec((bm, bk), lambda i, k: (i, k)),
                  pl.BlockSpec((bk, D), lambda i, k: (k, 0)),
                  pl.BlockSpec((bm, D), lambda i, k: (i, 0)),
                  pl.BlockSpec((1, D), lambda i, k: (0, 0))],
        out_specs=pl.BlockSpec((bm, D), lambda i, k: (i, 0)),
        out_shape=jax.ShapeDtypeStruct((S, D), jnp.float32),
        compiler_params=_cparams("parallel", "arbitrary"),
        name="ffn_down",
    )(act, w_down, h, g.reshape(1, D))


def _gate_weight(w_gate_t):
    D = w_gate_t.shape[1]
    per = HEADS_PER_GROUP * N_BRANCH
    w = w_gate_t.reshape(KV_GROUPS, per, D)
    w = jnp.pad(w, ((0, 0), (0, LANES - per), (0, 0)))
    return w.reshape(KV_GROUPS * LANES, D)


def _nsa_branch(hn, w_in, w_gate, cmp_w1_k, cmp_pe_k, cmp_w2_k, cmp_w1_v, cmp_pe_v, cmp_w2_v, rel_bias):
    qkv_cols = Q_WIDTH + 6 * KV_WIDTH
    w_gate = _gate_weight(w_gate)
    bm = 512
    qkv = _proj(hn, w_in, CDT, col0=POOL_WIDTH, n_cols=qkv_cols, pad_tiles=PAD_ROWS // bm, scaled_cols=Q_WIDTH,
                scale=HEAD_DIM ** -0.5 * LOG2E, bm=bm, name="proj_qkv")
    gates = _proj(hn, w_gate, jnp.float32, sigmoid=True, bm=bm, name="proj_gate")
    w1 = jnp.stack([cmp_w1_k, cmp_w1_v]).astype(CDT).reshape(2, 2, CMP_STRIDE * HEAD_DIM, HEAD_DIM)
    pe = jnp.stack([cmp_pe_k, cmp_pe_v]).astype(CDT).reshape(2, 2, CMP_STRIDE * HEAD_DIM)
    w2 = jnp.stack([cmp_w2_k, cmp_w2_v]).astype(CDT)
    kvc = _compress(qkv, w1, pe, w2)
    tabs = _bias_tiles(rel_bias)
    return _attention(qkv, gates, kvc, tabs)


def kernel(x, norm_mix_g, w_in, pool_w, pool_scale, cmp_pe_k, cmp_w1_k, cmp_w2_k, cmp_pe_v, cmp_w1_v, cmp_w2_v,
           rel_bias, w_out, norm_ffn_g, w_up, conv_w, conv_b, w_down, norm_final_g):
    B, S, D = x.shape
    assert B == 1 and w_in.shape[0] == 1, "single sequence, single layer"
    h = x.reshape(S, D)
    hn = _rmsnorm(h, norm_mix_g[0], CDT)
    n_main = POOL_WIDTH + Q_WIDTH + 6 * KV_WIDTH
    w_in_t = w_in.transpose(0, 2, 1)
    w_all = _cast_weight(w_in_t, 0, n_main, D, bm=512)
    u_pool = _proj(hn, w_all, jnp.float32, n_cols=POOL_WIDTH, name="proj_pool")
    y_pool = _pool(u_pool, pool_w[0], pool_scale[0])
    y_nsa = _nsa_branch(hn, w_all, w_in_t[0, n_main:], cmp_w1_k[0], cmp_pe_k[0], cmp_w2_k[0],
                        cmp_w1_v[0], cmp_pe_v[0], cmp_w2_v[0], rel_bias)
    h = _outproj(y_nsa, y_pool, w_out[0].astype(CDT), h)
    hn = _rmsnorm(h, norm_ffn_g[0], CDT)
    F = w_down.shape[1]
    f_pad = -(-F // FFN_K_TILE) * FFN_K_TILE
    act = _ffn_up(hn, w_up[0], conv_w[0], conv_b[0].reshape(1, -1), f_pad)
    out = _ffn_down(act, _cast_weight(w_down, 0, f_pad, D), h, norm_final_g)
    return out.reshape(B, S, D)
```

```python
import functools
import math

import numpy as np
import jax
import jax.numpy as jnp
from jax import lax
from jax.experimental import pallas as pl
from jax.experimental.pallas import tpu as pltpu

POOL_WINDOWS = (2, 4, 8, 16)
POOL_GROUP = 256
POOL_WIDTH = POOL_GROUP * len(POOL_WINDOWS)
HEAD_DIM = 128
KV_GROUPS = 4
HEADS_PER_GROUP = 6
NSA_HEADS = KV_GROUPS * HEADS_PER_GROUP
Q_WIDTH = NSA_HEADS * HEAD_DIM
KV_WIDTH = KV_GROUPS * HEAD_DIM
CMP_BLOCK = 32
CMP_STRIDE = 16
SLC_BLOCK = 64
N_SELECT = 16
N_LOCAL_FORCED = 2
WINDOW = 512
N_BRANCH = 3
REL_BUCKETS = 32
REL_MAX_DIST = 128
CONV_WIDTH = 3
RMS_EPS = 1e-6
NEG = -1e30
LOG2E = math.log2(math.e)

LANES = 128
SUBLANES = 8
VMEM_LIMIT_BYTES = 56 * 1024 * 1024

CDT = jnp.bfloat16
QB = 128
PAD_ROWS = WINDOW
FAR_CHUNK = 512
NEAR_KEYS = FAR_CHUNK
FAR_UNROLL = 2
WIN_KEYS = WINDOW + QB
CMP_PER_TILE = QB // CMP_STRIDE
BAND_BACK = 16
BAND_ROWS = 24
BAND_TABLE = BAND_ROWS + BAND_BACK


def _cparams(*sem, flags=None):
    return pltpu.CompilerParams(dimension_semantics=sem, vmem_limit_bytes=VMEM_LIMIT_BYTES, flags=flags)


def _dot(a, b):
    return jnp.dot(a, b, preferred_element_type=jnp.float32)


def _dot_nt(a, b):
    return lax.dot_general(a, b, (((1,), (1,)), ((), ())), preferred_element_type=jnp.float32)


def _cast_body(x_ref, o_ref, *, rows_valid):
    x = x_ref[...]
    row = pl.program_id(0) * x.shape[0] + lax.broadcasted_iota(jnp.int32, x.shape, 0)
    o_ref[...] = jnp.where(row < rows_valid, x, 0.0).astype(o_ref.dtype)


def _cast_weight(w, layer, n_rows, n_cols, bm=256, bn=4096):
    rows = w.shape[1]
    bm, bn = min(bm, rows), min(bn, n_cols)
    assert n_rows % bm == 0 and n_cols % bn == 0 and (rows % bm == 0 or n_rows <= rows // bm * bm)
    last = rows // bm - 1
    return pl.pallas_call(
        functools.partial(_cast_body, rows_valid=rows),
        grid=(n_rows // bm, n_cols // bn),
        in_specs=[pl.BlockSpec((None, bm, bn), lambda i, j: (layer, jnp.minimum(i, last), j))],
        out_specs=pl.BlockSpec((bm, bn), lambda i, j: (i, j)),
        out_shape=jax.ShapeDtypeStruct((n_rows, n_cols), CDT),
        compiler_params=_cparams("parallel", "parallel"),
        name="cast_weight",
    )(w)


def _rmsnorm_body(x_ref, g_ref, o_ref):
    x = x_ref[...]
    y = x * lax.rsqrt(jnp.mean(x * x, axis=-1, keepdims=True) + RMS_EPS)
    o_ref[...] = (y * g_ref[...]).astype(o_ref.dtype)


def _rmsnorm(x, g, out_dtype, bm=256):
    S, D = x.shape
    return pl.pallas_call(
        _rmsnorm_body,
        grid=(S // bm,),
        in_specs=[pl.BlockSpec((bm, D), lambda i: (i, 0)), pl.BlockSpec((1, D), lambda i: (0, 0))],
        out_specs=pl.BlockSpec((bm, D), lambda i: (i, 0)),
        out_shape=jax.ShapeDtypeStruct((S, D), out_dtype),
        compiler_params=_cparams("parallel"),
        name="rmsnorm",
    )(x, g.reshape(1, D))


def _proj_body(a_ref, w_ref, o_ref, *, pad_tiles, n_scaled, scale, sigmoid):
    i = pl.program_id(0)
    j = pl.program_id(1)

    @pl.when(i < pad_tiles)
    def _():
        o_ref[...] = jnp.zeros_like(o_ref)

    @pl.when(i >= pad_tiles)
    def _():
        r = _dot_nt(a_ref[...], w_ref[...].astype(a_ref.dtype))
        if n_scaled:
            r = r * jnp.where(j < n_scaled, jnp.float32(scale), jnp.float32(1.0))
        if sigmoid:
            r = jax.nn.sigmoid(r)
        o_ref[...] = r.astype(o_ref.dtype)


def _proj(a, w_t, out_dtype, *, col0=0, n_cols=None, pad_tiles=0, scaled_cols=0, scale=1.0, sigmoid=False,
          bm=512, bn=1024, name="proj"):
    S, D = a.shape
    N = w_t.shape[0] if n_cols is None else n_cols
    bn = min(bn, N)
    assert scaled_cols % bn == 0 and col0 % bn == 0 and N % bn == 0
    n_scaled = scaled_cols // bn
    first = col0 // bn
    body = functools.partial(_proj_body, pad_tiles=pad_tiles, n_scaled=n_scaled, scale=scale, sigmoid=sigmoid)
    return pl.pallas_call(
        body,
        grid=(S // bm + pad_tiles, N // bn),
        in_specs=[pl.BlockSpec((bm, D), lambda i, j: (jnp.maximum(i - pad_tiles, 0), 0)),
                  pl.BlockSpec((bn, D), lambda i, j: (first + j, 0))],
        out_specs=pl.BlockSpec((bm, bn), lambda i, j: (i, j)),
        out_shape=jax.ShapeDtypeStruct((S + pad_tiles * bm, N), out_dtype),
        compiler_params=_cparams("parallel", "arbitrary"),
        name=name,
    )(a, w_t)


POOL_HALO = 16


def _pool_body(u_ref, halo_ref, w_ref, s_ref, o_ref):
    i = pl.program_id(0)
    bm = u_ref.shape[0]
    u = u_ref[...]
    halo = jnp.where(i > 0, halo_ref[...], 0.0)
    ext = jnp.concatenate([halo, u], axis=0)
    t = i * bm + lax.broadcasted_iota(jnp.int32, (bm, 1), 0)
    acc = ext
    sums = {}
    shift = 1
    while shift < POOL_WINDOWS[-1]:
        acc = acc + pltpu.roll(acc, shift, axis=0)
        shift *= 2
        sums[shift] = acc
    for gi, w in enumerate(POOL_WINDOWS):
        cols = slice(gi * POOL_GROUP, (gi + 1) * POOL_GROUP)
        cnt = jnp.minimum(t + 1, w).astype(jnp.float32)
        d = sums[w][POOL_HALO:, cols] / cnt - u[:, cols]
        y = _dot(d.astype(CDT), w_ref[gi])
        o_ref[:, cols] = (y * s_ref[:, cols]).astype(o_ref.dtype)


def _pool(u, pool_w, pool_scale, bm=512):
    S = u.shape[0]
    return pl.pallas_call(
        _pool_body,
        grid=(S // bm,),
        in_specs=[pl.BlockSpec((bm, POOL_WIDTH), lambda i: (i, 0)),
                  pl.BlockSpec((POOL_HALO, POOL_WIDTH), lambda i: (jnp.maximum(i * (bm // POOL_HALO) - 1, 0), 0)),
                  pl.BlockSpec((len(POOL_WINDOWS), POOL_GROUP, POOL_GROUP), lambda i: (0, 0, 0)),
                  pl.BlockSpec((1, POOL_WIDTH), lambda i: (0, 0))],
        out_specs=pl.BlockSpec((bm, POOL_WIDTH), lambda i: (i, 0)),
        out_shape=jax.ShapeDtypeStruct((S, POOL_WIDTH), CDT),
        compiler_params=_cparams("parallel"),
        name="pool",
    )(u, u, pool_w.astype(CDT), pool_scale.reshape(1, POOL_WIDTH))


def _compress_body(x_ref, w1_ref, pe_ref, w2_ref, o_ref, *, pad_chunks):
    x = x_ref[...]
    w_lo = w1_ref[0]
    w_hi = w1_ref[1]
    a = _dot(x, w_lo)
    b = _dot(x, w_hi)
    pe = _dot(jnp.broadcast_to(pe_ref[0:1], (SUBLANES, pe_ref.shape[1])), w_lo) \
        + _dot(jnp.broadcast_to(pe_ref[1:2], (SUBLANES, pe_ref.shape[1])), w_hi)
    n = o_ref.shape[0]
    b_next = pltpu.roll(b, b.shape[0] - 1, axis=0)[pad_chunks:pad_chunks + n]
    pre = a[pad_chunks:pad_chunks + n] + b_next + pe[0:1, :]
    h = pre * jax.nn.sigmoid(pre)
    o_ref[...] = _dot(h.astype(CDT), w2_ref[...]).astype(o_ref.dtype)


def _compress(qkv, w1, pe, w2):
    rows = qkv.shape[0]
    chunks = rows // CMP_STRIDE
    pad_chunks = PAD_ROWS // CMP_STRIDE
    n_out = chunks - pad_chunks
    G = KV_GROUPS
    x = qkv[:, Q_WIDTH:Q_WIDTH + 2 * KV_WIDTH].reshape(chunks, CMP_STRIDE, 2 * G, HEAD_DIM)
    x = x.transpose(2, 0, 1, 3).reshape(2 * G, chunks, CMP_STRIDE * HEAD_DIM)
    width = CMP_STRIDE * HEAD_DIM
    body = functools.partial(_compress_body, pad_chunks=pad_chunks)
    return pl.pallas_call(
        body,
        grid=(2, G),
        in_specs=[pl.BlockSpec((None, chunks, width), lambda kv, g: (kv * G + g, 0, 0)),
                  pl.BlockSpec((None, 2, width, HEAD_DIM), lambda kv, g: (kv, 0, 0, 0)),
                  pl.BlockSpec((None, 2, width), lambda kv, g: (kv, 0, 0)),
                  pl.BlockSpec((None, HEAD_DIM, HEAD_DIM), lambda kv, g: (kv, 0, 0))],
        out_specs=pl.BlockSpec((None, None, n_out, HEAD_DIM), lambda kv, g: (kv, g, 0, 0)),
        out_shape=jax.ShapeDtypeStruct((2, G, n_out, HEAD_DIM), CDT),
        compiler_params=_cparams("parallel", "parallel"),
        name="compress",
    )(x, w1, pe, w2)


def _rel_bucket_np(dist):
    n = np.maximum(dist, 0)
    max_exact = REL_BUCKETS // 2
    nf = np.maximum(n, 1).astype(np.float32)
    large = max_exact + (np.log(nf / max_exact) / math.log(REL_MAX_DIST / max_exact)
                         * (REL_BUCKETS - max_exact)).astype(np.int32)
    large = np.minimum(large, REL_BUCKETS - 1)
    return np.where(n < max_exact, n, large).astype(np.int32)


def _bias_index_tiles():
    assert BAND_TABLE <= LANES and BAND_BACK % SUBLANES == 0 and BAND_ROWS % SUBLANES == 0
    q = np.arange(QB)[:, None]
    dist = NEAR_KEYS - QB + q - np.arange(NEAR_KEYS)[None, :]
    near = np.where(dist >= 0, _rel_bucket_np(dist), -1)
    dist = WINDOW + q - np.arange(WIN_KEYS)[None, :]
    win = np.where((dist >= 0) & (dist < WINDOW), _rel_bucket_np(dist), -1)
    w = np.arange(LANES)[None, :]
    dist = q - CMP_STRIDE * (w - BAND_BACK) - (CMP_BLOCK - 1)
    band = np.where(dist >= 0, _rel_bucket_np(dist), -1)
    return np.concatenate([near, win, band], axis=1).T.astype(np.int32)


def _bias_body(tbl_ref, idx_ref, o_ref):
    h = pl.program_id(0)
    idx = idx_ref[...]
    far = tbl_ref[REL_BUCKETS - 1, h]
    val = jnp.full(idx.shape, NEG, jnp.float32)
    for b in range(REL_BUCKETS):
        val = jnp.where(idx == b, (tbl_ref[b, h] - far) * LOG2E, val)
    o_ref[...] = val


def _bias_tiles(rel_bias):
    idx = jnp.asarray(_bias_index_tiles())
    width = idx.shape[0]
    H = HEADS_PER_GROUP
    return pl.pallas_call(
        _bias_body,
        grid=(NSA_HEADS,),
        in_specs=[pl.BlockSpec(memory_space=pltpu.SMEM),
                  pl.BlockSpec((width, QB), lambda h: (0, 0))],
        out_specs=pl.BlockSpec((None, width, QB), lambda h: (h // H, 0, h % H)),
        out_shape=jax.ShapeDtypeStruct((KV_GROUPS, width, H * QB), jnp.float32),
        compiler_params=_cparams("arbitrary"),
        name="bias_tiles",
    )(rel_bias, idx)


def _softmax_cols(s):
    m = jnp.max(s, axis=0, keepdims=True)
    p = jnp.exp2(s - m)
    l = jnp.sum(p, axis=0, keepdims=True)
    return p, jnp.where(m > 0.5 * NEG, 1.0 / l, 0.0)


def _split3(x):
    hi = x.astype(CDT)
    r = x - hi.astype(jnp.float32)
    mid = r.astype(CDT)
    lo = (r - mid.astype(jnp.float32)).astype(CDT)
    return hi, mid, lo


def _select_blocks(slc_t, i):
    n_slc = slc_t.shape[0]
    t = i * QB + lax.broadcasted_iota(jnp.int32, (1, QB), 1)
    j_int = lax.broadcasted_iota(jnp.int32, (n_slc, QB), 0)
    j_idx = j_int.astype(jnp.float32)
    cur = t // SLC_BLOCK
    forced = (j_int == 0) | ((cur - j_int >= 0) & (cur - j_int < N_LOCAL_FORCED))
    score = jnp.where(forced, 1e9, jnp.where(j_int > cur, -1e9, slc_t))
    picked = jnp.zeros(score.shape, jnp.bool_)
    for _ in range(min(N_SELECT, n_slc)):
        m = jnp.max(score, axis=0, keepdims=True)
        first = jnp.min(jnp.where(score == m, j_idx, float(n_slc)), axis=0, keepdims=True)
        hit = j_idx == first
        picked = picked | hit
        score = jnp.where(hit, -3e38, score)
    return picked


def _attn_body(q_ref, gate_ref, kc_ref, vct_ref, ks_ref, vst_ref, kw_ref, vwt_ref, blk_ref, tab_ref, wov_ref,
               o_ref, kaug_ref, kwaug_ref, sc_ref, pc_ref, sw_ref, pw_ref, s_ref, p_ref, acc_ref):
    i = pl.program_id(1)
    H = HEADS_PER_GROUP
    n_cmp = kc_ref.shape[0]
    n_slc = wov_ref.shape[0]
    lanes = H * QB
    head = lambda x, h: x[:, h * QB:(h + 1) * QB]

    @pl.when(i == 0)
    def _():
        kaug_ref[:, :HEAD_DIM] = ks_ref[...]
        kaug_ref[:, HEAD_DIM:] = blk_ref[...]
        kwaug_ref[:, :HEAD_DIM] = kw_ref[...]
        row = lax.broadcasted_iota(jnp.int32, (kwaug_ref.shape[0], LANES), 0)
        kwaug_ref[:, HEAD_DIM:] = jnp.where(row < PAD_ROWS, 1.0, 0.0).astype(CDT)

    qb = q_ref[...]
    q_t = jnp.concatenate([qb[:, h * HEAD_DIM:(h + 1) * HEAD_DIM].astype(jnp.float32).T.astype(CDT)
                           for h in range(H)], axis=1)

    def augment(extra):
        return jnp.concatenate([q_t, extra.astype(CDT)], axis=0)

    def values_t(ref, chunk0, n):
        return jnp.concatenate([ref[chunk0 + c] for c in range(n)], axis=1)

    grp = lax.broadcasted_iota(jnp.int32, (LANES, lanes), 0)
    sc_ref[...] = _dot(kc_ref[...], augment(jnp.where(grp > i, NEG, 0.0)))
    band0 = jnp.maximum(CMP_PER_TILE * i - BAND_BACK, 0)
    tab0 = band0 - (CMP_PER_TILE * i - BAND_BACK)
    band_rows = pl.ds(pl.multiple_of(band0, SUBLANES), BAND_ROWS)
    sc_ref[band_rows, :] += tab_ref[pl.ds(pl.multiple_of(NEAR_KEYS + WIN_KEYS + tab0, SUBLANES), BAND_ROWS), :]
    s_c = sc_ref[...]
    m_c = jnp.max(s_c, axis=0, keepdims=True)
    p_c = jnp.exp2(s_c - m_c)
    l_c = jnp.sum(p_c, axis=0, keepdims=True)
    pc_ref[...] = p_c.astype(CDT)
    seen = m_c > 0.5 * NEG
    o_cmp_t = _dot(vct_ref[...], pc_ref[...]) * jnp.where(seen, 1.0 / l_c, 0.0)
    shift_c = jnp.where(seen, m_c + jnp.log2(l_c), -NEG)
    imp_t = functools.reduce(lambda a, b: a + b,
                             [jnp.exp2(sc_ref[:, h * QB:(h + 1) * QB] - head(shift_c, h)) for h in range(H)])

    wov_t = wov_ref[...]
    slc_t = sum(_dot(wov_t, part) for part in _split3(imp_t))
    drop = jnp.where(_select_blocks(slc_t, i), 0.0, 1.0)
    j_row = lax.broadcasted_iota(jnp.int32, (n_slc, QB), 0)
    near_tile = i + 1 - NEAR_KEYS // QB
    drop_far = jnp.where(j_row < near_tile * (QB // SLC_BLOCK), drop, 1.0)
    tile_heads = lambda d: jnp.concatenate([d] * H, axis=1)

    win0 = pl.multiple_of(i * QB, QB)
    first_row = lax.broadcasted_iota(jnp.int32, (LANES, lanes), 0) == 0
    sw_ref[...] = (_dot(kwaug_ref[pl.ds(win0, WIN_KEYS), :], augment(jnp.where(first_row, NEG, 0.0)))
                   + tab_ref[NEAR_KEYS:NEAR_KEYS + WIN_KEYS])
    p_w, scale_w = _softmax_cols(sw_ref[...])
    pw_ref[...] = p_w.astype(CDT)
    o_win_t = _dot(values_t(vwt_ref, win0 // LANES, WIN_KEYS // LANES), pw_ref[...]) * scale_w

    near0 = pl.multiple_of(PAD_ROWS + near_tile * QB, QB)
    s_ref[...] = _dot(kaug_ref[pl.ds(near0, NEAR_KEYS), :], augment(tile_heads(drop))) + tab_ref[:NEAR_KEYS]
    p_ref[...] = jnp.zeros(p_ref.shape, CDT)
    acc_ref[...] = jnp.zeros(acc_ref.shape, jnp.float32)
    q_far = augment(tile_heads(drop_far))

    n_chunks = (kaug_ref.shape[0] - PAD_ROWS) // FAR_CHUNK
    n_far = (jnp.maximum(near_tile, 0) * QB + FAR_CHUNK - 1) // FAR_CHUNK
    far_lanes = FAR_CHUNK // LANES

    def far_step(it, carry):
        a_prev, m, l = carry
        v0 = jnp.where(it == 1, near0 // LANES, PAD_ROWS // LANES + jnp.clip(it - 2, 0, n_chunks - 1) * far_lanes)
        acc_ref[...] = a_prev * acc_ref[...] + _dot(values_t(vst_ref, v0, far_lanes), p_ref[...])
        s_prev = s_ref[...]
        m_new = jnp.maximum(m, jnp.max(s_prev, axis=0, keepdims=True))
        a = jnp.exp2(m - m_new)
        p = jnp.exp2(s_prev - m_new)
        l = a * l + jnp.sum(p, axis=0, keepdims=True)
        p_ref[...] = p.astype(CDT)
        k0 = pl.multiple_of(jnp.where(it < n_far, PAD_ROWS + it * FAR_CHUNK, 0), FAR_CHUNK)
        s_ref[...] = _dot(kaug_ref[pl.ds(k0, FAR_CHUNK), :], q_far)
        return a, m_new, l

    def far_trip(k, carry):
        for u in range(FAR_UNROLL):
            carry = far_step(FAR_UNROLL * k + u, carry)
        return carry

    zero = scale_w - scale_w
    steps = n_far + 2
    trips = steps // FAR_UNROLL
    carry = lax.fori_loop(0, trips, far_trip, (zero + 1.0, zero + 0.1 * NEG, zero))
    for u in range(FAR_UNROLL - 1):
        it = FAR_UNROLL * trips + u
        carry = lax.cond(it < steps, functools.partial(far_step, it), lambda c: c, carry)
    l_s = carry[2]
    o_slc_t = acc_ref[...] * (1.0 / l_s)

    gates_t = gate_ref[...].T
    for h in range(H):
        g = lambda b: gates_t[h * N_BRANCH + b:h * N_BRANCH + b + 1, :]
        o_t = g(0) * head(o_cmp_t, h) + g(1) * head(o_slc_t, h) + g(2) * head(o_win_t, h)
        o_ref[:, h * HEAD_DIM:(h + 1) * HEAD_DIM] = o_t.T.astype(o_ref.dtype)


def _block_membership(rows, n_slc):
    key = np.arange(rows)[:, None] - PAD_ROWS
    j = np.arange(n_slc)[None, :]
    member = np.where(key >= 0, key // SLC_BLOCK == j, True)
    return jnp.asarray(np.where(member, NEG, 0.0), dtype=CDT)


def _overlap_weights(n_cmp, n_slc):
    r = SLC_BLOCK // CMP_STRIDE
    lead = -(-CMP_BLOCK // CMP_STRIDE) - 1
    w = np.zeros((n_cmp, n_slc), np.float32)
    for o in range(-lead, r):
        s0 = o * CMP_STRIDE
        ov = max(0, min(s0 + CMP_BLOCK, SLC_BLOCK) - max(s0, 0))
        for j in range(n_slc):
            n = r * j + o
            if ov > 0 and 0 <= n < n_cmp - 1:
                w[n, j] = ov / CMP_STRIDE
    return jnp.asarray(w.T, dtype=CDT)


def _tile_group_columns(n_cmp):
    n = np.arange(n_cmp)[:, None]
    return jnp.asarray(n // CMP_PER_TILE == np.arange(LANES)[None, :], dtype=CDT)


def _attention(qkv, gates, kvc, tabs):
    rows = qkv.shape[0]
    S = rows - PAD_ROWS
    n_cmp = kvc.shape[2]
    n_slc = S // SLC_BLOCK
    assert n_cmp // CMP_PER_TILE <= LANES
    first = Q_WIDTH // LANES
    G = KV_GROUPS
    lanes = HEADS_PER_GROUP * QB
    kv_spec = lambda which: pl.BlockSpec((rows, LANES), lambda g, i: (0, first + which * G + g))
    vt_spec = lambda which: pl.BlockSpec((None, None, rows // LANES, HEAD_DIM, LANES), lambda g, i: (which, g, 0, 0, 0))
    width = tabs.shape[1]
    kc_aug = jnp.concatenate([kvc[0], jnp.broadcast_to(_tile_group_columns(n_cmp), (G, n_cmp, LANES))], axis=-1)
    vc_t = kvc[1].transpose(0, 2, 1)
    v_cols = jnp.stack([qkv[:, (first + w * G) * LANES:(first + (w + 1) * G) * LANES] for w in (3, 5)])
    v_t = v_cols.reshape(2, rows // LANES, LANES, G, HEAD_DIM).transpose(0, 3, 1, 4, 2)
    return pl.pallas_call(
        _attn_body,
        grid=(G, S // QB),
        in_specs=[pl.BlockSpec((QB, HEADS_PER_GROUP * HEAD_DIM), lambda g, i: (i + PAD_ROWS // QB, g)),
                  pl.BlockSpec((QB, LANES), lambda g, i: (i, g)),
                  pl.BlockSpec((None, n_cmp, HEAD_DIM + LANES), lambda g, i: (g, 0, 0)),
                  pl.BlockSpec((None, HEAD_DIM, n_cmp), lambda g, i: (g, 0, 0)),
                  kv_spec(2), vt_spec(0), kv_spec(4), vt_spec(1),
                  pl.BlockSpec((rows, n_slc), lambda g, i: (0, 0)),
                  pl.BlockSpec((None, width, lanes), lambda g, i: (g, 0, 0)),
                  pl.BlockSpec((n_slc, n_cmp), lambda g, i: (0, 0))],
        out_specs=pl.BlockSpec((QB, HEADS_PER_GROUP * HEAD_DIM), lambda g, i: (i, g)),
        out_shape=jax.ShapeDtypeStruct((S, Q_WIDTH), CDT),
        scratch_shapes=[pltpu.VMEM((rows, HEAD_DIM + n_slc), CDT),
                        pltpu.VMEM((rows, HEAD_DIM + LANES), CDT),
                        pltpu.VMEM((n_cmp, lanes), jnp.float32),
                        pltpu.VMEM((n_cmp, lanes), CDT),
                        pltpu.VMEM((WIN_KEYS, lanes), jnp.float32),
                        pltpu.VMEM((WIN_KEYS, lanes), CDT),
                        pltpu.VMEM((FAR_CHUNK, lanes), jnp.float32),
                        pltpu.VMEM((FAR_CHUNK, lanes), CDT),
                        pltpu.VMEM((HEAD_DIM, lanes), jnp.float32)],
        compiler_params=_cparams("parallel", "arbitrary"),
        name="nsa",
    )(qkv, gates, kc_aug, vc_t, qkv, v_t, qkv, v_t, _block_membership(rows, n_slc), tabs,
      _overlap_weights(n_cmp, n_slc))


def _outproj_body(a1_ref, a2_ref, w1_ref, w2_ref, x_ref, o_ref):
    o_ref[...] = x_ref[...] + _dot(a1_ref[...], w1_ref[...]) + _dot(a2_ref[...], w2_ref[...])


def _outproj(y_nsa, y_pool, w_out, x, bm=512, bn=1024):
    S, D = x.shape
    bn = min(bn, D)
    k1, k2 = y_nsa.shape[1], y_pool.shape[1]
    return pl.pallas_call(
        _outproj_body,
        grid=(D // bn, S // bm),
        in_specs=[pl.BlockSpec((bm, k1), lambda j, i: (i, 0)),
                  pl.BlockSpec((bm, k2), lambda j, i: (i, 0)),
                  pl.BlockSpec((k1, bn), lambda j, i: (0, j)),
                  pl.BlockSpec((k2, bn), lambda j, i: (k1 // k2, j)),
                  pl.BlockSpec((bm, bn), lambda j, i: (i, j))],
        out_specs=pl.BlockSpec((bm, bn), lambda j, i: (i, j)),
        out_shape=jax.ShapeDtypeStruct((S, D), jnp.float32),
        compiler_params=_cparams("parallel", "arbitrary"),
        name="outproj",
    )(y_nsa, y_pool, w_out, w_out, x)


CONV_HALO = 16
FFN_SUBTILES = 4
FFN_K_TILE = 512


def _ffn_up_body(a_ref, halo_ref, wa_ref, wb_ref, cwa_ref, cwb_ref, cba_ref, cbb_ref, o_ref, lhs_ref, *, nb):
    i = pl.program_id(0)
    j = pl.program_id(1)

    @pl.when(j == 0)
    def _():
        lhs_ref[:CONV_HALO] = jnp.where(i > 0, halo_ref[...], jnp.zeros_like(halo_ref))
        lhs_ref[CONV_HALO:] = a_ref[...]

    def conv(lhs, w, cw_ref, cb_ref):
        u = _dot(lhs, w)
        c = cb_ref[...] + cw_ref[CONV_WIDTH - 1:CONV_WIDTH, :] * u[CONV_HALO:]
        for k in range(1, CONV_WIDTH):
            c = c + cw_ref[CONV_WIDTH - 1 - k:CONV_WIDTH - k, :] * pltpu.roll(u, k, axis=0)[CONV_HALO:]
        return c

    @pl.when(j < nb)
    def _():
        wa = wa_ref[...].astype(CDT)
        wb = wb_ref[...].astype(CDT)
        sub = o_ref.shape[0] // FFN_SUBTILES
        for s in range(FFN_SUBTILES):
            lhs = lhs_ref[s * sub:s * sub + CONV_HALO + sub]
            ca = conv(lhs, wa, cwa_ref, cba_ref)
            cb = conv(lhs, wb, cwb_ref, cbb_ref)
            o_ref[s * sub:(s + 1) * sub] = (ca * jax.nn.sigmoid(ca) * cb).astype(o_ref.dtype)

    @pl.when(j >= nb)
    def _():
        o_ref[...] = jnp.zeros_like(o_ref)


def _ffn_up(hn, w_up, conv_w, conv_b, f_pad, bm=2048, bn=256):
    S, D = hn.shape
    F = w_up.shape[1] // 2
    nb = F // bn
    col = lambda j: jnp.minimum(j, nb - 1)
    return pl.pallas_call(
        functools.partial(_ffn_up_body, nb=nb),
        grid=(S // bm, f_pad // bn),
        in_specs=[pl.BlockSpec((bm, D), lambda i, j: (i, 0), pipeline_mode=pl.Buffered(1)),
                  pl.BlockSpec((CONV_HALO, D), lambda i, j: (jnp.maximum(i * (bm // CONV_HALO) - 1, 0), 0)),
                  pl.BlockSpec((D, bn), lambda i, j: (0, col(j))),
                  pl.BlockSpec((D, bn), lambda i, j: (0, nb + col(j))),
                  pl.BlockSpec((CONV_WIDTH, bn), lambda i, j: (0, col(j))),
                  pl.BlockSpec((CONV_WIDTH, bn), lambda i, j: (0, nb + col(j))),
                  pl.BlockSpec((1, bn), lambda i, j: (0, col(j))),
                  pl.BlockSpec((1, bn), lambda i, j: (0, nb + col(j)))],
        out_specs=pl.BlockSpec((bm, bn), lambda i, j: (i, j)),
        out_shape=jax.ShapeDtypeStruct((S, f_pad), CDT),
        scratch_shapes=[pltpu.VMEM((CONV_HALO + bm, D), CDT)],
        compiler_params=_cparams("parallel", "arbitrary"),
        name="ffn_up",
    )(hn, hn, w_up, w_up, conv_w, conv_w, conv_b, conv_b)


def _ffn_down_body(a_ref, w_ref, h_ref, g_ref, o_ref):
    k = pl.program_id(1)

    @pl.when(k == 0)
    def _():
        o_ref[...] = h_ref[...]

    o_ref[...] += _dot(a_ref[...], w_ref[...])

    @pl.when(k == pl.num_programs(1) - 1)
    def _():
        x = o_ref[...]
        y = x * lax.rsqrt(jnp.mean(x * x, axis=-1, keepdims=True) + RMS_EPS)
        o_ref[...] = y * g_ref[...]


def _ffn_down(act, w_down, h, g, bm=512, bk=FFN_K_TILE):
    S, F = act.shape
    D = h.shape[1]
    return pl.pallas_call(
        _ffn_down_body,
        grid=(S // bm, F // bk),
        in_specs=[pl.BlockSpec((bm, bk), lambda i, k: (i, k)),
                  pl.BlockSpec((bk, D), lambda i, k: (k, 0)),
                  pl.BlockSpec((bm, D), lambda i, k: (i, 0)),
                  pl.BlockSpec((1, D), lambda i, k: (0, 0))],
        out_specs=pl.BlockSpec((bm, D), lambda i, k: (i, 0)),
        out_shape=jax.ShapeDtypeStruct((S, D), jnp.float32),
        compiler_params=_cparams("parallel", "arbitrary"),
        name="ffn_down",
    )(act, w_down, h, g.reshape(1, D))


def _gate_weight(w_gate_t):
    D = w_gate_t.shape[1]
    per = HEADS_PER_GROUP * N_BRANCH
    w = w_gate_t.reshape(KV_GROUPS, per, D)
    w = jnp.pad(w, ((0, 0), (0, LANES - per), (0, 0)))
    return w.reshape(KV_GROUPS * LANES, D)


def _nsa_branch(hn, w_in, w_gate, cmp_w1_k, cmp_pe_k, cmp_w2_k, cmp_w1_v, cmp_pe_v, cmp_w2_v, rel_bias):
    qkv_cols = Q_WIDTH + 6 * KV_WIDTH
    w_gate = _gate_weight(w_gate)
    bm = 512
    qkv = _proj(hn, w_in, CDT, col0=POOL_WIDTH, n_cols=qkv_cols, pad_tiles=PAD_ROWS // bm, scaled_cols=Q_WIDTH,
                scale=HEAD_DIM ** -0.5 * LOG2E, bm=bm, name="proj_qkv")
    gates = _proj(hn, w_gate, jnp.float32, sigmoid=True, bm=bm, name="proj_gate")
    w1 = jnp.stack([cmp_w1_k, cmp_w1_v]).astype(CDT).reshape(2, 2, CMP_STRIDE * HEAD_DIM, HEAD_DIM)
    pe = jnp.stack([cmp_pe_k, cmp_pe_v]).astype(CDT).reshape(2, 2, CMP_STRIDE * HEAD_DIM)
    w2 = jnp.stack([cmp_w2_k, cmp_w2_v]).astype(CDT)
    kvc = _compress(qkv, w1, pe, w2)
    tabs = _bias_tiles(rel_bias)
    return _attention(qkv, gates, kvc, tabs)


def kernel(x, norm_mix_g, w_in, pool_w, pool_scale, cmp_pe_k, cmp_w1_k, cmp_w2_k, cmp_pe_v, cmp_w1_v, cmp_w2_v,
           rel_bias, w_out, norm_ffn_g, w_up, conv_w, conv_b, w_down, norm_final_g):
    B, S, D = x.shape
    assert B == 1 and w_in.shape[0] == 1, "single sequence, single layer"
    h = x.reshape(S, D)
    hn = _rmsnorm(h, norm_mix_g[0], CDT)
    n_main = POOL_WIDTH + Q_WIDTH + 6 * KV_WIDTH
    w_in_t = w_in.transpose(0, 2, 1)
    w_all = _cast_weight(w_in_t, 0, n_main, D, bm=512)
    u_pool = _proj(hn, w_all, jnp.float32, n_cols=POOL_WIDTH, name="proj_pool")
    y_pool = _pool(u_pool, pool_w[0], pool_scale[0])
    y_nsa = _nsa_branch(hn, w_all, w_in_t[0, n_main:], cmp_w1_k[0], cmp_pe_k[0], cmp_w2_k[0],
                        cmp_w1_v[0], cmp_pe_v[0], cmp_w2_v[0], rel_bias)
    h = _outproj(y_nsa, y_pool, w_out[0].astype(CDT), h)
    hn = _rmsnorm(h, norm_ffn_g[0], CDT)
    F = w_down.shape[1]
    f_pad = -(-F // FFN_K_TILE) * FFN_K_TILE
    act = _ffn_up(hn, w_up[0], conv_w[0], conv_b[0].reshape(1, -1), f_pad)
    out = _ffn_down(act, _cast_weight(w_down, 0, f_pad, D), h, norm_final_g)
    return out.reshape(B, S, D)
```
